```python
import jax
import jax.numpy as jnp
from jax import lax
import numpy as np


D_MODEL = 1024
BATCH = 8
SEQ = 4096
DEPTH = 2

CHUNK = 64
HEAD_DIM = 64
F32 = jnp.float32
RMS_EPS = 1e-6
GN_EPS = 64e-5

A_HEADS = 8
A_WIDTH = A_HEADS * HEAD_DIM
DECAY_LORA = 64
AAA_LORA = 64
A_SPLITS = (A_WIDTH, DECAY_LORA, A_WIDTH, A_WIDTH, AAA_LORA, A_WIDTH)
A_COLS = A_WIDTH * 4 + DECAY_LORA + AAA_LORA

B_HEADS = 8
B_WIDTH = B_HEADS * HEAD_DIM
PAST_CHUNKS = 8
BAND = (PAST_CHUNKS + 1) * CHUNK
REL_CLIP = 256
N_REL = (CHUNK - 1) + REL_CLIP + 1
B_SPLITS = (B_WIDTH, B_WIDTH, B_WIDTH, B_WIDTH)
B_COLS = B_WIDTH * 4

C_HEADS = 8
C_KEY_DIM = 64
C_VAL_DIM = 64
C_FWIDTH = C_HEADS * C_KEY_DIM
C_WIDTH = C_HEADS * C_VAL_DIM
C_SPLITS = (C_FWIDTH, C_FWIDTH, C_WIDTH, C_WIDTH)
C_COLS = C_FWIDTH * 2 + C_WIDTH * 2

N_BRANCH = 3
MERGE_COLS = N_BRANCH * D_MODEL
IN_COLS = A_COLS + B_COLS + C_COLS + MERGE_COLS

kernel_name = 'hybrid_rwkv7_chunkattn_hgrn2_gated'


def _split(z, sizes):
    out, start = [], 0
    for s in sizes:
        out.append(z[..., start:start + s])
        start += s
    return out


def _rmsnorm(x, g, eps=RMS_EPS):
    xf = x.astype(F32)
    y = xf * lax.rsqrt(jnp.mean(xf * xf, axis=-1, keepdims=True) + eps)
    return (y * g.astype(F32)).astype(x.dtype)


def _rwkv7_scan(r, decay, k, v, kk, b):
    Bsz, T, H, N = r.shape
    seq = tuple(jnp.moveaxis(t, 1, 0) for t in (r, decay, k, v, kk, b))

    def step(S, inp):
        r_t, w_t, k_t, v_t, kk_t, b_t = inp
        sa = jnp.einsum('bhij,bhj->bhi', S, -kk_t)
        S = S * w_t[:, :, None, :] + sa[..., None] * b_t[:, :, None, :] + v_t[..., None] * k_t[:, :, None, :]
        return S, jnp.einsum('bhij,bhj->bhi', S, r_t)

    S0 = jnp.zeros((Bsz, H, N, N), F32)
    _, y = lax.scan(step, S0, seq)
    return jnp.moveaxis(y, 0, 1)


def _rwkv7_mixer(za, mu, w0, w2, a0, a2, k_k, k_a, r_k, ln_w, ln_b):
    Bsz, T, _ = za.shape
    za_prev = jnp.pad(za, ((0, 0), (1, 0), (0, 0)))[:, :T]
    za = za + (za_prev - za) * mu
    r, wl, k, v, al, gate = _split(za, A_SPLITS)
    w_log = -jax.nn.softplus(-(w0 + jnp.tanh(wl) @ w2).astype(F32)) - 0.5
    decay = jnp.exp(-jnp.exp(w_log))
    a = jax.nn.sigmoid((a0 + al @ a2).astype(F32))

    def heads(t):
        return t.astype(F32).reshape(t.shape[:-1] + (A_HEADS, HEAD_DIM))

    r, k, v, decay, a = heads(r), heads(k), heads(v), heads(decay), heads(a)
    kk = k * heads(k_k)
    kk = kk / jnp.maximum(jnp.sqrt(jnp.sum(kk * kk, axis=-1, keepdims=True)), 1e-12)
    k = k * (1.0 + (a - 1.0) * heads(k_a))
    y = _rwkv7_scan(r, decay, k, v, kk, kk * a)
    mean = jnp.mean(y, axis=-1, keepdims=True)
    var = jnp.mean(jnp.square(y - mean), axis=-1, keepdims=True)
    y = (y - mean) * lax.rsqrt(var + GN_EPS) * heads(ln_w) + heads(ln_b)
    y = y + jnp.sum(r * k * r_k.astype(F32), axis=-1, keepdims=True) * v
    y = y.reshape(Bsz, T, A_WIDTH).astype(gate.dtype)
    return y * jax.nn.silu(gate)


def _chunk_attention(q, k, v, gate, q_g, k_g, rel_bias):
    Bsz, T, _ = q.shape
    n_chunks = T // CHUNK

    def heads(t):
        return t.reshape(Bsz, T, B_HEADS, HEAD_DIM)

    q = _rmsnorm(heads(q), q_g).astype(F32) * (HEAD_DIM ** -0.5)
    k = _rmsnorm(heads(k), k_g).astype(F32)
    v = heads(v).astype(F32)
    pad = ((0, 0), (PAST_CHUNKS * CHUNK, 0), (0, 0), (0, 0))
    k_pad = jnp.pad(k, pad)
    v_pad = jnp.pad(v, pad)
    qi = np.arange(CHUNK)[:, None]
    kj = np.arange(BAND)[None, :]
    rel = np.clip(PAST_CHUNKS * CHUNK + qi - kj, -(CHUNK - 1), REL_CLIP) + (CHUNK - 1)
    bias = rel_bias.astype(F32)[:, rel]
    key_slot = jnp.arange(BAND)
    q_chunks = jnp.moveaxis(q.reshape(Bsz, n_chunks, CHUNK, B_HEADS, HEAD_DIM), 1, 0)

    def one_chunk(args):
        n, q_c = args
        start = n * CHUNK
        k_b = lax.dynamic_slice_in_dim(k_pad, start, BAND, axis=1)
        v_b = lax.dynamic_slice_in_dim(v_pad, start, BAND, axis=1)
        s = jnp.einsum('bqhd,bkhd->bhqk', q_c, k_b) + bias
        valid = key_slot >= (PAST_CHUNKS - n) * CHUNK
        s = jnp.where(valid, s, -jnp.inf)
        p = jax.nn.softmax(s, axis=-1)
        return jnp.einsum('bhqk,bkhd->bqhd', p, v_b)

    o = lax.map(one_chunk, (jnp.arange(n_chunks), q_chunks))
    o = jnp.moveaxis(o, 0, 1).reshape(Bsz, T, B_WIDTH).astype(gate.dtype)
    return o * jax.nn.silu(gate)


def _hgrn2_chunk_scan(q, k, g, v):
    Bsz, T, H, DK = q.shape
    DV = v.shape[-1]
    n = T // CHUNK

    def to_chunks(t):
        return t.reshape(Bsz, n, CHUNK, H, t.shape[-1]).transpose(1, 0, 3, 2, 4)

    causal = jnp.asarray(np.tril(np.ones((CHUNK, CHUNK), dtype=bool)))

    def step(S, inp):
        q_c, k_c, g_c, v_c = inp
        b = jnp.cumsum(g_c, axis=2)
        o_inter = jnp.einsum('bhtd,bhde->bhte', q_c * jnp.exp(b), S)
        diff = b[:, :, :, None, :] - b[:, :, None, :, :]
        dec = jnp.exp(jnp.where(causal[:, :, None], diff, -jnp.inf))
        att = jnp.einsum('bhtd,bhsd,bhtsd->bhts', q_c, k_c, dec)
        o = o_inter + jnp.einsum('bhts,bhse->bhte', att, v_c)
        b_end = b[:, :, -1:, :]
        S = jnp.exp(b_end[:, :, 0, :])[..., None] * S + jnp.einsum('bhsd,bhse->bhde', k_c * jnp.exp(b_end - b), v_c)
        return S, o

    S0 = jnp.zeros((Bsz, H, DK, DV), F32)
    _, o = lax.scan(step, S0, (to_chunks(q), to_chunks(k), to_chunks(g), to_chunks(v)))
    return o.transpose(1, 0, 3, 2, 4).reshape(Bsz, T, H, DV)


def _hgrn2_mixer(q, f, i, gate, lb, norm_g):
    Bsz, T, _ = q.shape
    lbf = lb.astype(F32)
    fg = lbf + (1.0 - lbf) * jax.nn.sigmoid(f.astype(F32))
    key = 1.0 - fg
    logf = jnp.log(fg)
    qh = jax.nn.silu(q.astype(F32)).reshape(Bsz, T, C_HEADS, C_KEY_DIM)
    o = _hgrn2_chunk_scan(qh, key.reshape(Bsz, T, C_HEADS, C_KEY_DIM), logf.reshape(Bsz, T, C_HEADS, C_KEY_DIM),
                          i.astype(F32).reshape(Bsz, T, C_HEADS, C_VAL_DIM))
    o = _rmsnorm(o, norm_g).reshape(Bsz, T, C_WIDTH).astype(gate.dtype)
    return o * jax.nn.silu(gate)


def setup_inputs(seed: int = 0) -> dict:
    key = jax.random.key(seed)
    ks = jax.random.split(key, 24)
    L = DEPTH

    def nrm(k, shape, s):
        return jax.random.normal(k, shape, F32) * s

    frac = jnp.arange(A_WIDTH, dtype=F32) / (A_WIDTH - 1)
    return {
        'x': nrm(ks[0], (BATCH, SEQ, D_MODEL), 1.0),
        'norm_g': 1.0 + nrm(ks[1], (L, D_MODEL), 0.02),
        'w_in': nrm(ks[2], (L, D_MODEL, IN_COLS), D_MODEL ** -0.5),
        'rwkv_mu': jax.random.uniform(ks[3], (L, A_COLS), F32),
        'rwkv_w0': -5.5 + 5.0 * frac ** 0.85 + nrm(ks[4], (L, A_WIDTH), 0.1),
        'rwkv_w2': nrm(ks[5], (L, DECAY_LORA, A_WIDTH), 0.1),
        'rwkv_a0': nrm(ks[6], (L, A_WIDTH), 0.1),
        'rwkv_a2': nrm(ks[7], (L, AAA_LORA, A_WIDTH), 0.1),
        'rwkv_k_k': 0.85 + nrm(ks[8], (L, A_WIDTH), 0.02),
        'rwkv_k_a': 1.0 + nrm(ks[9], (L, A_WIDTH), 0.02),
        'rwkv_r_k': nrm(ks[10], (L, A_HEADS, HEAD_DIM), 0.1),
        'rwkv_ln_w': 1.0 + nrm(ks[11], (L, A_WIDTH), 0.02),
        'rwkv_ln_b': nrm(ks[12], (L, A_WIDTH), 0.02),
        'attn_q_norm': 1.0 + nrm(ks[13], (L, HEAD_DIM), 0.02),
        'attn_k_norm': 1.0 + nrm(ks[14], (L, HEAD_DIM), 0.02),
        'attn_rel_bias': nrm(ks[15], (L, B_HEADS, N_REL), 0.1),
        'hgrn_lb': nrm(ks[16], (L, C_FWIDTH), 0.1),
        'hgrn_norm': 1.0 + nrm(ks[17], (L, C_VAL_DIM), 0.02),
        'proj_a': nrm(ks[18], (L, A_WIDTH, D_MODEL), A_WIDTH ** -0.5),
        'proj_b': nrm(ks[19], (L, B_WIDTH, D_MODEL), B_WIDTH ** -0.5),
        'proj_c': nrm(ks[20], (L, C_WIDTH, D_MODEL), C_WIDTH ** -0.5),
        'w_out': nrm(ks[21], (L, D_MODEL, D_MODEL), D_MODEL ** -0.5),
    }


def reference(x, norm_g, w_in, rwkv_mu, rwkv_w0, rwkv_w2, rwkv_a0, rwkv_a2, rwkv_k_k, rwkv_k_a, rwkv_r_k,
              rwkv_ln_w, rwkv_ln_b, attn_q_norm, attn_k_norm, attn_rel_bias, hgrn_lb, hgrn_norm,
              proj_a, proj_b, proj_c, w_out):
    lb_all = jax.nn.softmax(hgrn_lb.astype(F32), axis=0)
    lb_all = jnp.cumsum(lb_all, axis=0) - lb_all[0]
    for l in range(DEPTH):
        h = _rmsnorm(x, norm_g[l])
        z = h @ w_in[l]
        za, zb, zc, zg = _split(z, (A_COLS, B_COLS, C_COLS, MERGE_COLS))
        y_a = _rwkv7_mixer(za, rwkv_mu[l], rwkv_w0[l], rwkv_w2[l], rwkv_a0[l], rwkv_a2[l], rwkv_k_k[l],
                           rwkv_k_a[l], rwkv_r_k[l], rwkv_ln_w[l], rwkv_ln_b[l])
        qb, kb, vb, gb = _split(zb, B_SPLITS)
        y_b = _chunk_attention(qb, kb, vb, gb, attn_q_norm[l], attn_k_norm[l], attn_rel_bias[l])
        qc, fc, ic, gc = _split(zc, C_SPLITS)
        y_c = _hgrn2_mixer(qc, fc, ic, gc, lb_all[l], hgrn_norm[l])
        g_a, g_b, g_c = _split(jax.nn.sigmoid(zg), (D_MODEL, D_MODEL, D_MODEL))
        m = g_a * (y_a @ proj_a[l]) + g_b * (y_b @ proj_b[l]) + g_c * (y_c @ proj_c[l])
        x = x + (m @ w_out[l]).astype(x.dtype)
    return x
```

```python
import functools

import numpy as np
import jax
import jax.numpy as jnp
from jax import lax
from jax.experimental import pallas as pl
from jax.experimental.pallas import tpu as pltpu

F32 = jnp.float32
BF16 = jnp.bfloat16

D_MODEL = 1024
CHUNK = 64
HEAD_DIM = 64
HEADS = 8
WIDTH = HEADS * HEAD_DIM
LORA = 64
A_COLS = 4 * WIDTH + 2 * LORA
PAST_CHUNKS = 8
BAND = (PAST_CHUNKS + 1) * CHUNK
PAD_ROWS = PAST_CHUNKS * CHUNK
REL_CLIP = 256
RMS_EPS = 1e-6
GN_EPS = 64e-5
LEVELS = (32, 16, 8, 4, 2, 1)

TOKEN_TILE = 256
VMEM_LIMIT = 56 * 1024 * 1024

_NN = (((1,), (0,)), ((), ()))
_NT = (((1,), (1,)), ((), ()))
_TN = (((0,), (0,)), ((), ()))


def _dg(a, b, dims):
    return lax.dot_general(a, b, dims, preferred_element_type=F32)


def _hi_lo(a):
    hi = a.astype(BF16)
    lo = (a - hi.astype(F32)).astype(BF16)
    return hi, lo


def _mm3(a, b, dims=_NN):
    a1, a2 = _hi_lo(a)
    b1, b2 = _hi_lo(b)
    return _dg(a1, b1, dims) + (_dg(a1, b2, dims) + _dg(a2, b1, dims))


def _head_sum(a, ones_bd):
    a1, a2 = _hi_lo(a)
    return _dg(a1, ones_bd, _NN) + _dg(a2, ones_bd, _NN)


def _sigmoid(x):
    return 1.0 / (1.0 + jnp.exp(-x))


def _cumsum_rows(x):
    n = x.shape[0]
    rows = lax.broadcasted_iota(jnp.int32, x.shape, 0)
    s = 1
    while s < n:
        x = x + jnp.where(rows >= s, pltpu.roll(x, s, 0), 0.0)
        s *= 2
    return x


def _norm_proj(x_ref, g_ref, w_ref):
    x = x_ref[...]
    ms = jnp.mean(x * x, axis=-1, keepdims=True)
    h = x * lax.rsqrt(ms + RMS_EPS) * g_ref[...]
    return jnp.dot(h.astype(BF16), w_ref[...], preferred_element_type=F32)


def _rwkv_prep_kernel(x_ref, g_ref, w_ref, mu_ref, pa_ref, w2_ref, a2_ref, ones_ref,
                      r_o, lw_o, k_o, v_o, kk_o, b_o, bonus_o, gate_o, carry_ref):
    t = pl.program_id(1)
    za = _norm_proj(x_ref, g_ref, w_ref)
    tm = za.shape[0]

    @pl.when(t == 0)
    def _():
        carry_ref[...] = jnp.zeros_like(carry_ref)

    row = lax.broadcasted_iota(jnp.int32, (tm, 1), 0)
    prev = jnp.where(row == 0, carry_ref[7:8, :], pltpu.roll(za, 1, 0))
    carry_ref[...] = za[tm - 8:, :]
    xs = za + (prev - za) * mu_ref[...]

    r = xs[:, 0:WIDTH]
    k = xs[:, WIDTH:2 * WIDTH]
    v = xs[:, 2 * WIDTH:3 * WIDTH]
    gate = xs[:, 3 * WIDTH:4 * WIDTH]
    wl = xs[:, 4 * WIDTH:4 * WIDTH + LORA]
    al = xs[:, 4 * WIDTH + LORA:4 * WIDTH + 2 * LORA]
    w0, a0, k_k, k_a, r_k = (pa_ref[i:i + 1, :] for i in range(5))
    ones_bd = ones_ref[...]

    u = -(w0 + _mm3(jnp.tanh(wl), w2_ref[...]))
    softplus = jnp.maximum(u, 0.0) + jnp.log(1.0 + jnp.exp(-jnp.abs(u)))
    lw = -jnp.exp(-softplus - 0.5)
    a = _sigmoid(a0 + _mm3(al, a2_ref[...]))

    kkp = k * k_k
    kk = kkp / jnp.maximum(jnp.sqrt(_head_sum(kkp * kkp, ones_bd)), 1e-12)
    k2 = k * (1.0 + (a - 1.0) * k_a)

    r_o[...] = r
    lw_o[...] = lw
    k_o[...] = k2
    v_o[...] = v
    kk_o[...] = kk
    b_o[...] = kk * a
    bonus_o[...] = _head_sum(r * k2 * r_k, ones_bd) * v
    gate_o[...] = gate * _sigmoid(gate)


def _rwkv_scan_kernel(r_ref, lw_ref, k_ref, v_ref, kk_ref, b_ref, bonus_ref, gate_ref, ln_ref,
                      y_o, s_ref):
    c = pl.program_id(1)

    @pl.when(c == 0)
    def _():
        s_ref[...] = jnp.zeros_like(s_ref)

    lw = lw_ref[...]
    cum = _cumsum_rows(lw)
    cend = cum[CHUNK - 1:CHUNK, :]
    p = jnp.exp(cum)
    pinv = jnp.exp(-cum)
    pprev = jnp.exp(cum - lw)
    dend = jnp.exp(cend - cum)
    pend = jnp.exp(cend)

    kk = kk_ref[...]
    bv = b_ref[...]
    k2 = k_ref[...]
    v = v_ref[...]
    a_t = -kk * pprev
    b_t = bv * pinv
    k_t = k2 * pinv
    r_t = r_ref[...] * p
    b_h = bv * dend
    k_h = k2 * dend

    ri = lax.broadcasted_iota(jnp.int32, (CHUNK, CHUNK), 0)
    ci = lax.broadcasted_iota(jnp.int32, (CHUNK, CHUNK), 1)
    strict = ri > ci
    incl = ri >= ci

    for h in range(HEADS):
        sl = slice(h * HEAD_DIM, (h + 1) * HEAD_DIM)
        ah, bh, kh, rh, vh = a_t[:, sl], b_t[:, sl], k_t[:, sl], r_t[:, sl], v[:, sl]
        m_ab = jnp.where(strict, _mm3(ah, bh, _NT), 0.0)
        m_ak = jnp.where(strict, _mm3(ah, kh, _NT), 0.0)
        l_rb = jnp.where(incl, _mm3(rh, bh, _NT), 0.0)
        l_rk = jnp.where(incl, _mm3(rh, kh, _NT), 0.0)

        a2 = ah
        u0 = _mm3(m_ak, vh)
        m = m_ab
        for i in range(6):
            a2 = a2 + _mm3(m, a2)
            u0 = u0 + _mm3(m, u0)
            if i < 5:
                m = _mm3(m, m)

        s = s_ref[h]
        r2 = rh + _mm3(l_rb, a2)
        y = _mm3(r2, s, _NT) + _mm3(l_rb, u0) + _mm3(l_rk, vh)
        bhh, khh = b_h[:, sl], k_h[:, sl]
        q = _mm3(a2, bhh, _TN)
        s_ref[h] = s * pend[:, sl] + _mm3(s, q) + _mm3(u0, bhh, _TN) + _mm3(vh, khh, _TN)

        mean = jnp.mean(y, axis=-1, keepdims=True)
        yc = y - mean
        var = jnp.mean(yc * yc, axis=-1, keepdims=True)
        y = yc * lax.rsqrt(var + GN_EPS) * ln_ref[0:1, sl] + ln_ref[1:2, sl] + bonus_ref[:, sl]
        y_o[:, sl] = (y * gate_ref[:, sl]).astype(y_o.dtype)


def _attn_prep_kernel(x_ref, g_ref, w_ref, qk_ref, ones_ref, q_o, k_o, v_o, gate_o, *, pad_tiles):
    i = pl.program_id(1)

    @pl.when(i < pad_tiles)
    def _():
        k_o[...] = jnp.zeros_like(k_o)
        v_o[...] = jnp.zeros_like(v_o)

    @pl.when(i >= pad_tiles)
    def _():
        zb = _norm_proj(x_ref, g_ref, w_ref)
        ones_bd = ones_ref[...]
        q = zb[:, 0:WIDTH]
        k = zb[:, WIDTH:2 * WIDTH]
        gate = zb[:, 3 * WIDTH:4 * WIDTH]
        qn = q * lax.rsqrt(_head_sum(q * q, ones_bd) * (1.0 / HEAD_DIM) + RMS_EPS) * qk_ref[0:1, :]
        kn = k * lax.rsqrt(_head_sum(k * k, ones_bd) * (1.0 / HEAD_DIM) + RMS_EPS) * qk_ref[1:2, :]
        q_o[...] = (qn * HEAD_DIM ** -0.5).astype(q_o.dtype)
        k_o[...] = kn.astype(k_o.dtype)
        v_o[...] = zb[:, 2 * WIDTH:3 * WIDTH].astype(v_o.dtype)
        gate_o[...] = gate * _sigmoid(gate)


def _attn_band_kernel(q_ref, k_ref, v_ref, bias_ref, gate_ref, y_o):
    n = pl.program_id(1)
    start = pl.multiple_of(n * CHUNK, CHUNK)
    q = q_ref[...]
    kb = k_ref[pl.ds(start, BAND), :]
    vb = v_ref[pl.ds(start, BAND), :]
    col = lax.broadcasted_iota(jnp.int32, (1, BAND), 1)
    valid = col >= (PAST_CHUNKS - n) * CHUNK
    for h in range(HEADS):
        sl = slice(h * HEAD_DIM, (h + 1) * HEAD_DIM)
        s = _dg(q[:, sl], kb[:, sl], _NT) + bias_ref[h]
        s = jnp.where(valid, s, -jnp.inf)
        p = jnp.exp(s - jnp.max(s, axis=-1, keepdims=True))
        o = _dg(p.astype(BF16), vb[:, sl], _NN) / jnp.sum(p, axis=-1, keepdims=True)
        y_o[:, sl] = (o * gate_ref[:, sl]).astype(y_o.dtype)


def _hgrn_prep_kernel(x_ref, g_ref, w_ref, lb_ref, q_o, k_o, lf_o, v_o, gate_o, *, layer):
    zc = _norm_proj(x_ref, g_ref, w_ref)
    lbp = lb_ref[...]
    e = jnp.exp(lbp - jnp.max(lbp, axis=0, keepdims=True))
    sm = e / jnp.sum(e, axis=0, keepdims=True)
    lb = jnp.sum(sm[0:layer + 1], axis=0, keepdims=True) - sm[0:1]
    q = zc[:, 0:WIDTH]
    gate = zc[:, 3 * WIDTH:4 * WIDTH]
    fg = lb + (1.0 - lb) * _sigmoid(zc[:, WIDTH:2 * WIDTH])
    q_o[...] = q * _sigmoid(q)
    k_o[...] = 1.0 - fg
    lf_o[...] = jnp.log(fg)
    v_o[...] = zc[:, 2 * WIDTH:3 * WIDTH]
    gate_o[...] = gate * _sigmoid(gate)


def _hgrn_scan_kernel(q_ref, k_ref, lf_ref, v_ref, gate_ref, sel_ref, ones_ref, ng_ref, y_o, s_ref):
    c = pl.program_id(1)

    @pl.when(c == 0)
    def _():
        s_ref[...] = jnp.zeros_like(s_ref)

    q = q_ref[...]
    k = k_ref[...]
    v = v_ref[...]
    b = _cumsum_rows(lf_ref[...])
    bend = b[CHUNK - 1:CHUNK, :]

    b1 = b.astype(BF16)
    rem = b - b1.astype(F32)
    b2 = rem.astype(BF16)
    b3 = (rem - b2.astype(F32)).astype(BF16)
    edges = _dg(sel_ref[...], jnp.concatenate([b1, b2, b3], axis=0), _NN)

    row = lax.broadcasted_iota(jnp.int32, (CHUNK, 1), 0)
    ri = lax.broadcasted_iota(jnp.int32, (CHUNK, CHUNK), 0)
    ci = lax.broadcasted_iota(jnp.int32, (CHUNK, CHUNK), 1)
    n_lv = len(LEVELS)
    q_lv, k_lv, same_lv = [], [], []
    for i, w in enumerate(LEVELS):
        before = edges[i * CHUNK:(i + 1) * CHUNK, :]
        last = edges[(n_lv + i) * CHUNK:(n_lv + i + 1) * CHUNK, :]
        second = ((row // w) % 2) == 1
        q_lv.append(jnp.where(second, q * jnp.exp(b - before), 0.0))
        k_lv.append(jnp.where(second, 0.0, k * jnp.exp(last - b)))
        same_lv.append((ri // (2 * w)) == (ci // (2 * w)))

    qk_diag = _head_sum(q * k, ones_ref[...])
    q_in = q * jnp.exp(b)
    k_out = k * jnp.exp(bend - b)
    dec = jnp.exp(bend)

    for h in range(HEADS):
        sl = slice(h * HEAD_DIM, (h + 1) * HEAD_DIM)
        att = jnp.zeros((CHUNK, CHUNK), F32)
        for i in range(n_lv):
            att = att + jnp.where(same_lv[i], _mm3(q_lv[i][:, sl], k_lv[i][:, sl], _NT), 0.0)
        vh = v[:, sl]
        s = s_ref[h]
        o = _mm3(att, vh) + qk_diag[:, sl] * vh + _mm3(q_in[:, sl], s, _NT)
        s_ref[h] = s * dec[:, sl] + _mm3(vh, k_out[:, sl], _TN)
        ms = jnp.mean(o * o, axis=-1, keepdims=True)
        o = o * lax.rsqrt(ms + RMS_EPS) * ng_ref[:, sl]
        y_o[:, sl] = (o * gate_ref[:, sl]).astype(y_o.dtype)


def _merge_kernel(x_ref, g_ref, wg_ref, ya_ref, yb_ref, yc_ref, pa_ref, pb_ref, pc_ref, wo_ref, o_ref):
    gates = _sigmoid(_norm_proj(x_ref, g_ref, wg_ref))
    m = (gates[:, 0:D_MODEL] * jnp.dot(ya_ref[...], pa_ref[...], preferred_element_type=F32)
         + gates[:, D_MODEL:2 * D_MODEL] * jnp.dot(yb_ref[...], pb_ref[...], preferred_element_type=F32)
         + gates[:, 2 * D_MODEL:3 * D_MODEL] * jnp.dot(yc_ref[...], pc_ref[...], preferred_element_type=F32))
    o_ref[...] = x_ref[...] + jnp.dot(m.astype(BF16), wo_ref[...], preferred_element_type=F32)


def _params(*sem):
    return pltpu.CompilerParams(dimension_semantics=sem, vmem_limit_bytes=VMEM_LIMIT)


def _full(shape):
    return pl.BlockSpec(shape, lambda *_: (0,) * len(shape))


def _edge_selectors():
    n_lv = len(LEVELS)
    sel = np.zeros((2 * n_lv * CHUNK, CHUNK), np.float32)
    for i, w in enumerate(LEVELS):
        for t in range(CHUNK):
            first = (t // w) * w
            if first > 0:
                sel[i * CHUNK + t, first - 1] = 1.0
            sel[(n_lv + i) * CHUNK + t, first + w - 1] = 1.0
    return jnp.asarray(np.concatenate([sel, sel, sel], axis=1), BF16)


def _band_bias(rel_bias):
    qi = np.arange(CHUNK)[:, None]
    kj = np.arange(BAND)[None, :]
    rel = np.clip(PAD_ROWS + qi - kj, -(CHUNK - 1), REL_CLIP) + (CHUNK - 1)
    return rel_bias.astype(F32)[:, rel]


def _layer(x, l, p, consts):
    bsz, seq, _ = x.shape
    tm = min(TOKEN_TILE, seq)
    n_t = seq // tm
    n_c = seq // CHUNK
    ones_bd, sel = consts
    g = p['norm_g'][l][None, :]

    tok = lambda w: pl.BlockSpec((None, tm, w), lambda b, t: (b, t, 0))
    chk = lambda w: pl.BlockSpec((None, CHUNK, w), lambda b, c: (b, c, 0))
    act = lambda dt: jax.ShapeDtypeStruct((bsz, seq, WIDTH), dt)

    w_in = p['w_in'][l]
    wa = w_in[:, :A_COLS]
    cols = np.r_[0:WIDTH, WIDTH + LORA:2 * WIDTH + LORA, 2 * WIDTH + LORA:3 * WIDTH + LORA,
                 3 * WIDTH + 2 * LORA:4 * WIDTH + 2 * LORA, WIDTH:WIDTH + LORA,
                 3 * WIDTH + LORA:3 * WIDTH + 2 * LORA]
    pa = jnp.stack([p['rwkv_w0'][l], p['rwkv_a0'][l], p['rwkv_k_k'][l], p['rwkv_k_a'][l],
                    p['rwkv_r_k'][l].reshape(WIDTH)] + [jnp.zeros((WIDTH,), F32)] * 3)
    r, lw, k2, v, kk, bv, bonus, gate_a = pl.pallas_call(
        _rwkv_prep_kernel,
        grid=(bsz, n_t),
        in_specs=[tok(D_MODEL), _full((1, D_MODEL)), _full((D_MODEL, A_COLS)), _full((1, A_COLS)),
                  _full((8, WIDTH)), _full((LORA, WIDTH)), _full((LORA, WIDTH)), _full((WIDTH, WIDTH))],
        out_specs=[tok(WIDTH)] * 8,
        out_shape=[act(F32)] * 8,
        scratch_shapes=[pltpu.VMEM((8, A_COLS), F32)],
        compiler_params=_params("parallel", "arbitrary"),
        name="rwkv_prep",
    )(x, g, wa[:, cols].astype(BF16), p['rwkv_mu'][l][cols][None, :], pa,
      p['rwkv_w2'][l], p['rwkv_a2'][l], ones_bd)

    ln = jnp.stack([p['rwkv_ln_w'][l], p['rwkv_ln_b'][l]] + [jnp.zeros((WIDTH,), F32)] * 6)
    y_a = pl.pallas_call(
        _rwkv_scan_kernel,
        grid=(bsz, n_c),
        in_specs=[chk(WIDTH)] * 8 + [_full((8, WIDTH))],
        out_specs=chk(WIDTH),
        out_shape=act(BF16),
        scratch_shapes=[pltpu.VMEM((HEADS, HEAD_DIM, HEAD_DIM), F32)],
        compiler_params=_params("parallel", "arbitrary"),
        name="rwkv_scan",
    )(r, lw, k2, v, kk, bv, bonus, gate_a, ln)

    pad_tiles = PAD_ROWS // tm
    wb = w_in[:, A_COLS:A_COLS + 4 * WIDTH].astype(BF16)
    qk = jnp.stack([jnp.tile(p['attn_q_norm'][l], HEADS), jnp.tile(p['attn_k_norm'][l], HEADS)]
                   + [jnp.zeros((WIDTH,), F32)] * 6)
    shift = lambda b, i: (b, jnp.maximum(i - pad_tiles, 0), 0)
    q, k_pad, v_pad, gate_b = pl.pallas_call(
        functools.partial(_attn_prep_kernel, pad_tiles=pad_tiles),
        grid=(bsz, n_t + pad_tiles),
        in_specs=[pl.BlockSpec((None, tm, D_MODEL), shift), _full((1, D_MODEL)),
                  _full((D_MODEL, 4 * WIDTH)), _full((8, WIDTH)), _full((WIDTH, WIDTH))],
        out_specs=[pl.BlockSpec((None, tm, WIDTH), shift), tok(WIDTH), tok(WIDTH),
                   pl.BlockSpec((None, tm, WIDTH), shift)],
        out_shape=[act(BF16), jax.ShapeDtypeStruct((bsz, seq + PAD_ROWS, WIDTH), BF16),
                   jax.ShapeDtypeStruct((bsz, seq + PAD_ROWS, WIDTH), BF16), act(F32)],
        compiler_params=_params("parallel", "arbitrary"),
        name="attn_prep",
    )(x, g, wb, qk, ones_bd)

    band = pl.BlockSpec((None, seq + PAD_ROWS, WIDTH), lambda b, c: (b, 0, 0))
    y_b = pl.pallas_call(
        _attn_band_kernel,
        grid=(bsz, n_c),
        in_specs=[chk(WIDTH), band, band, _full((HEADS, CHUNK, BAND)), chk(WIDTH)],
        out_specs=chk(WIDTH),
        out_shape=act(BF16),
        compiler_params=_params("parallel", "arbitrary"),
        name="attn_band",
    )(q, k_pad, v_pad, _band_bias(p['attn_rel_bias'][l]), gate_b)

    wc = w_in[:, A_COLS + 4 * WIDTH:A_COLS + 8 * WIDTH].astype(BF16)
    n_layers = p['hgrn_lb'].shape[0]
    qc, kc, lf, vc, gate_c = pl.pallas_call(
        functools.partial(_hgrn_prep_kernel, layer=l),
        grid=(bsz, n_t),
        in_specs=[tok(D_MODEL), _full((1, D_MODEL)), _full((D_MODEL, 4 * WIDTH)), _full((n_layers, WIDTH))],
        out_specs=[tok(WIDTH)] * 5,
        out_shape=[act(F32)] * 5,
        compiler_params=_params("parallel", "parallel"),
        name="hgrn_prep",
    )(x, g, wc, p['hgrn_lb'])

    y_c = pl.pallas_call(
        _hgrn_scan_kernel,
        grid=(bsz, n_c),
        in_specs=[chk(WIDTH)] * 5 + [_full(sel.shape), _full((WIDTH, WIDTH)), _full((1, WIDTH))],
        out_specs=chk(WIDTH),
        out_shape=act(BF16),
        scratch_shapes=[pltpu.VMEM((HEADS, HEAD_DIM, HEAD_DIM), F32)],
        compiler_params=_params("parallel", "arbitrary"),
        name="hgrn_scan",
    )(qc, kc, lf, vc, gate_c, sel, ones_bd, jnp.tile(p['hgrn_norm'][l], HEADS)[None, :])

    wg = w_in[:, A_COLS + 8 * WIDTH:].astype(BF16)
    return pl.pallas_call(
        _merge_kernel,
        grid=(bsz, n_t),
        in_specs=[tok(D_MODEL), _full((1, D_MODEL)), _full((D_MODEL, 3 * D_MODEL)),
                  tok(WIDTH), tok(WIDTH), tok(WIDTH),
                  _full((WIDTH, D_MODEL)), _full((WIDTH, D_MODEL)), _full((WIDTH, D_MODEL)),
                  _full((D_MODEL, D_MODEL))],
        out_specs=tok(D_MODEL),
        out_shape=jax.ShapeDtypeStruct(x.shape, x.dtype),
        compiler_params=_params("parallel", "parallel"),
        name="merge",
    )(x, g, wg, y_a, y_b, y_c, p['proj_a'][l].astype(BF16), p['proj_b'][l].astype(BF16),
      p['proj_c'][l].astype(BF16), p['w_out'][l].astype(BF16))


def kernel(x, norm_g, w_in, rwkv_mu, rwkv_w0, rwkv_w2, rwkv_a0, rwkv_a2, rwkv_k_k, rwkv_k_a, rwkv_r_k,
           rwkv_ln_w, rwkv_ln_b, attn_q_norm, attn_k_norm, attn_rel_bias, hgrn_lb, hgrn_norm,
           proj_a, proj_b, proj_c, w_out):
    p = dict(norm_g=norm_g, w_in=w_in, rwkv_mu=rwkv_mu, rwkv_w0=rwkv_w0, rwkv_w2=rwkv_w2,
             rwkv_a0=rwkv_a0, rwkv_a2=rwkv_a2, rwkv_k_k=rwkv_k_k, rwkv_k_a=rwkv_k_a, rwkv_r_k=rwkv_r_k,
             rwkv_ln_w=rwkv_ln_w, rwkv_ln_b=rwkv_ln_b, attn_q_norm=attn_q_norm, attn_k_norm=attn_k_norm,
             attn_rel_bias=attn_rel_bias, hgrn_lb=hgrn_lb, hgrn_norm=hgrn_norm,
             proj_a=proj_a, proj_b=proj_b, proj_c=proj_c, w_out=w_out)
    head = np.arange(WIDTH) // HEAD_DIM
    ones_bd = jnp.asarray(head[:, None] == head[None, :], BF16)
    consts = (ones_bd, _edge_selectors())
    for l in range(norm_g.shape[0]):
        x = _layer(x, l, p, consts)
    return x
```

```python
import functools

import numpy as np
import jax
import jax.numpy as jnp
from jax import lax
from jax.experimental import pallas as pl
from jax.experimental.pallas import tpu as pltpu

F32 = jnp.float32
BF16 = jnp.bfloat16

D_MODEL = 1024
CHUNK = 64
HEAD_DIM = 64
HEADS = 8
WIDTH = HEADS * HEAD_DIM
LORA = 64
A_COLS = 4 * WIDTH + 2 * LORA
PAST_CHUNKS = 8
BAND = (PAST_CHUNKS + 1) * CHUNK
PAD_ROWS = PAST_CHUNKS * CHUNK
REL_CLIP = 256
RMS_EPS = 1e-6
GN_EPS = 64e-5
LEVELS = (32, 16, 8, 4, 2, 1)

TOKEN_TILE = 256
VMEM_LIMIT = 56 * 1024 * 1024

_NN = (((1,), (0,)), ((), ()))
_NT = (((1,), (1,)), ((), ()))
_TN = (((0,), (0,)), ((), ()))


def _dg(a, b, dims):
    return lax.dot_general(a, b, dims, preferred_element_type=F32)


def _hi_lo(a):
    hi = a.astype(BF16)
    lo = (a - hi.astype(F32)).astype(BF16)
    return hi, lo


def _mm3(a, b, dims=_NN):
    a1, a2 = _hi_lo(a)
    b1, b2 = _hi_lo(b)
    return _dg(a1, b1, dims) + (_dg(a1, b2, dims) + _dg(a2, b1, dims))


def _head_sum(a, ones_bd):
    a1, a2 = _hi_lo(a)
    return _dg(a1, ones_bd, _NN) + _dg(a2, ones_bd, _NN)


def _sigmoid(x):
    return 1.0 / (1.0 + jnp.exp(-x))


def _cumsum_rows(x):
    n = x.shape[0]
    rows = lax.broadcasted_iota(jnp.int32, x.shape, 0)
    s = 1
    while s < n:
        x = x + jnp.where(rows >= s, pltpu.roll(x, s, 0), 0.0)
        s *= 2
    return x


def _norm_proj(x_ref, g_ref, w_ref):
    x = x_ref[...]
    ms = jnp.mean(x * x, axis=-1, keepdims=True)
    h = x * lax.rsqrt(ms + RMS_EPS) * g_ref[...]
    return jnp.dot(h.astype(BF16), w_ref[...], preferred_element_type=F32)


def _rwkv_prep_kernel(x_ref, g_ref, w_ref, mu_ref, pa_ref, w2_ref, a2_ref, ones_ref,
                      r_o, lw_o, k_o, v_o, kk_o, b_o, bonus_o, gate_o, carry_ref):
    t = pl.program_id(1)
    za = _norm_proj(x_ref, g_ref, w_ref)
    tm = za.shape[0]

    @pl.when(t == 0)
    def _():
        carry_ref[...] = jnp.zeros_like(carry_ref)

    row = lax.broadcasted_iota(jnp.int32, (tm, 1), 0)
    prev = jnp.where(row == 0, carry_ref[7:8, :], pltpu.roll(za, 1, 0))
    carry_ref[...] = za[tm - 8:, :]
    xs = za + (prev - za) * mu_ref[...]

    r = xs[:, 0:WIDTH]
    k = xs[:, WIDTH:2 * WIDTH]
    v = xs[:, 2 * WIDTH:3 * WIDTH]
    gate = xs[:, 3 * WIDTH:4 * WIDTH]
    wl = xs[:, 4 * WIDTH:4 * WIDTH + LORA]
    al = xs[:, 4 * WIDTH + LORA:4 * WIDTH + 2 * LORA]
    w0, a0, k_k, k_a, r_k = (pa_ref[i:i + 1, :] for i in range(5))
    ones_bd = ones_ref[...]

    u = -(w0 + _mm3(jnp.tanh(wl), w2_ref[...]))
    softplus = jnp.maximum(u, 0.0) + jnp.log(1.0 + jnp.exp(-jnp.abs(u)))
    lw = -jnp.exp(-softplus - 0.5)
    a = _sigmoid(a0 + _mm3(al, a2_ref[...]))

    kkp = k * k_k
    kk = kkp / jnp.maximum(jnp.sqrt(_head_sum(kkp * kkp, ones_bd)), 1e-12)
    k2 = k * (1.0 + (a - 1.0) * k_a)

    r_o[...] = r
    lw_o[...] = lw
    k_o[...] = k2
    v_o[...] = v
    kk_o[...] = kk
    b_o[...] = kk * a
    bonus_o[...] = _head_sum(r * k2 * r_k, ones_bd) * v
    gate_o[...] = gate * _sigmoid(gate)


def _rwkv_scan_kernel(r_ref, lw_ref, k_ref, v_ref, kk_ref, b_ref, bonus_ref, gate_ref, ln_ref,
                      y_o, s_ref):
    c = pl.program_id(1)

    @pl.when(c == 0)
    def _():
        s_ref[...] = jnp.zeros_like(s_ref)

    lw = lw_ref[...]
    cum = _cumsum_rows(lw)
    cend = cum[CHUNK - 1:CHUNK, :]
    p = jnp.exp(cum)
    pinv = jnp.exp(-cum)
    pprev = jnp.exp(cum - lw)
    dend = jnp.exp(cend - cum)
    pend = jnp.exp(cend)

    kk = kk_ref[...]
    bv = b_ref[...]
    k2 = k_ref[...]
    v = v_ref[...]
    a_t = (-kk * pprev).astype(BF16)
    b_t = (bv * pinv).astype(BF16)
    k_t = (k2 * pinv).astype(BF16)
    r_f = r_ref[...] * p
    r_t = r_f.astype(BF16)
    b_h = (bv * dend).astype(BF16)
    k_h = (k2 * dend).astype(BF16)
    v_b = v.astype(BF16)

    ri = lax.broadcasted_iota(jnp.int32, (CHUNK, CHUNK), 0)
    ci = lax.broadcasted_iota(jnp.int32, (CHUNK, CHUNK), 1)
    strict = ri > ci
    incl = ri >= ci
    heads = range(HEADS)
    sls = [slice(h * HEAD_DIM, (h + 1) * HEAD_DIM) for h in heads]
    lo, hi = slice(0, HEAD_DIM), slice(HEAD_DIM, 2 * HEAD_DIM)

    sc = [_dg(jnp.concatenate([a_t[:, sl], r_t[:, sl]], axis=0),
              jnp.concatenate([b_t[:, sl], k_t[:, sl]], axis=0), _NT) for sl in sls]
    m = [jnp.where(strict, sc[h][lo, lo], 0.0) for h in heads]
    m_ak = [jnp.where(strict, sc[h][lo, hi], 0.0).astype(BF16) for h in heads]
    l_rb = [jnp.where(incl, sc[h][hi, lo], 0.0).astype(BF16) for h in heads]
    l_rk = [jnp.where(incl, sc[h][hi, hi], 0.0).astype(BF16) for h in heads]

    u0 = [_dg(m_ak[h], v_b[:, sls[h]], _NN) for h in heads]
    x = [jnp.concatenate([a_t[:, sls[h]].astype(F32), u0[h]], axis=1) for h in heads]
    for i in range(6):
        mb = [m[h].astype(BF16) for h in heads]
        x = [x[h] + _dg(mb[h], x[h].astype(BF16), _NN) for h in heads]
        if i < 5:
            m = [_dg(mb[h], mb[h], _NN) for h in heads]
    xb = [x[h].astype(BF16) for h in heads]

    s = [s_ref[h] for h in heads]
    sb = [s[h].astype(BF16) for h in heads]
    lx = [_dg(l_rb[h], xb[h], _NN) for h in heads]
    lv = [_dg(l_rk[h], v_b[:, sls[h]], _NN) for h in heads]
    r2 = [(r_f[:, sls[h]] + lx[h][:, lo]).astype(BF16) for h in heads]
    y = [_dg(r2[h], sb[h], _NT) + lx[h][:, hi] + lv[h] for h in heads]

    xq = [_dg(xb[h], b_h[:, sls[h]], _TN) for h in heads]
    vk = [_dg(v_b[:, sls[h]], k_h[:, sls[h]], _TN) for h in heads]
    sq = [_dg(sb[h], xq[h][lo, :].astype(BF16), _NN) for h in heads]
    for h in heads:
        s_ref[h] = s[h] * pend[:, sls[h]] + sq[h] + xq[h][hi, :] + vk[h]

    for h in heads:
        sl = sls[h]
        mean = jnp.mean(y[h], axis=-1, keepdims=True)
        yc = y[h] - mean
        var = jnp.mean(yc * yc, axis=-1, keepdims=True)
        yn = yc * lax.rsqrt(var + GN_EPS) * ln_ref[0:1, sl] + ln_ref[1:2, sl] + bonus_ref[:, sl]
        y_o[:, sl] = (yn * gate_ref[:, sl]).astype(y_o.dtype)


def _attn_prep_kernel(x_ref, g_ref, w_ref, qk_ref, ones_ref, q_o, k_o, v_o, gate_o, *, pad_tiles):
    i = pl.program_id(1)

    @pl.when(i < pad_tiles)
    def _():
        k_o[...] = jnp.zeros_like(k_o)
        v_o[...] = jnp.zeros_like(v_o)

    @pl.when(i >= pad_tiles)
    def _():
        zb = _norm_proj(x_ref, g_ref, w_ref)
        ones_bd = ones_ref[...]
        q = zb[:, 0:WIDTH]
        k = zb[:, WIDTH:2 * WIDTH]
        gate = zb[:, 3 * WIDTH:4 * WIDTH]
        qn = q * lax.rsqrt(_head_sum(q * q, ones_bd) * (1.0 / HEAD_DIM) + RMS_EPS) * qk_ref[0:1, :]
        kn = k * lax.rsqrt(_head_sum(k * k, ones_bd) * (1.0 / HEAD_DIM) + RMS_EPS) * qk_ref[1:2, :]
        q_o[...] = (qn * HEAD_DIM ** -0.5).astype(q_o.dtype)
        k_o[...] = kn.astype(k_o.dtype)
        v_o[...] = zb[:, 2 * WIDTH:3 * WIDTH].astype(v_o.dtype)
        gate_o[...] = gate * _sigmoid(gate)


def _attn_band_kernel(q_ref, k_ref, v_ref, bias_ref, gate_ref, y_o):
    n = pl.program_id(1)
    start = pl.multiple_of(n * CHUNK, CHUNK)
    q = q_ref[...]
    kb = k_ref[pl.ds(start, BAND), :]
    vb = v_ref[pl.ds(start, BAND), :]
    col = lax.broadcasted_iota(jnp.int32, (1, BAND), 1)
    valid = col >= (PAST_CHUNKS - n) * CHUNK
    heads = range(HEADS)
    sls = [slice(h * HEAD_DIM, (h + 1) * HEAD_DIM) for h in heads]
    s = [_dg(q[:, sl], kb[:, sl], _NT) for sl in sls]
    s = [jnp.where(valid, s[h] + bias_ref[h], -jnp.inf) for h in heads]
    p = [jnp.exp(s[h] - jnp.max(s[h], axis=-1, keepdims=True)) for h in heads]
    o = [_dg(p[h].astype(BF16), vb[:, sls[h]], _NN) for h in heads]
    for h in heads:
        oh = o[h] / jnp.sum(p[h], axis=-1, keepdims=True)
        y_o[:, sls[h]] = (oh * gate_ref[:, sls[h]]).astype(y_o.dtype)


def _hgrn_prep_kernel(x_ref, g_ref, w_ref, lb_ref, q_o, k_o, lf_o, v_o, gate_o, *, layer):
    zc = _norm_proj(x_ref, g_ref, w_ref)
    lbp = lb_ref[...]
    e = jnp.exp(lbp - jnp.max(lbp, axis=0, keepdims=True))
    sm = e / jnp.sum(e, axis=0, keepdims=True)
    lb = jnp.sum(sm[0:layer + 1], axis=0, keepdims=True) - sm[0:1]
    q = zc[:, 0:WIDTH]
    gate = zc[:, 3 * WIDTH:4 * WIDTH]
    fg = lb + (1.0 - lb) * _sigmoid(zc[:, WIDTH:2 * WIDTH])
    q_o[...] = q * _sigmoid(q)
    k_o[...] = 1.0 - fg
    lf_o[...] = jnp.log(fg)
    v_o[...] = zc[:, 2 * WIDTH:3 * WIDTH]
    gate_o[...] = gate * _sigmoid(gate)


def _hgrn_scan_kernel(q_ref, k_ref, lf_ref, v_ref, gate_ref, sel_ref, ones_ref, ng_ref, y_o, s_ref):
    c = pl.program_id(1)

    @pl.when(c == 0)
    def _():
        s_ref[...] = jnp.zeros_like(s_ref)

    q = q_ref[...]
    k = k_ref[...]
    v = v_ref[...]
    b = _cumsum_rows(lf_ref[...])
    bend = b[CHUNK - 1:CHUNK, :]

    b1 = b.astype(BF16)
    rem = b - b1.astype(F32)
    b2 = rem.astype(BF16)
    b3 = (rem - b2.astype(F32)).astype(BF16)
    edges = _dg(sel_ref[...], jnp.concatenate([b1, b2, b3], axis=0), _NN)

    row = lax.broadcasted_iota(jnp.int32, (CHUNK, 1), 0)
    ri = lax.broadcasted_iota(jnp.int32, (CHUNK, CHUNK), 0)
    ci = lax.broadcasted_iota(jnp.int32, (CHUNK, CHUNK), 1)
    n_lv = len(LEVELS)
    q_lv, k_lv, same_lv = [], [], []
    for i, w in enumerate(LEVELS):
        before = edges[i * CHUNK:(i + 1) * CHUNK, :]
        last = edges[(n_lv + i) * CHUNK:(n_lv + i + 1) * CHUNK, :]
        second = ((row // w) % 2) == 1
        q_lv.append(jnp.where(second, q * jnp.exp(b - before), 0.0))
        k_lv.append(jnp.where(second, 0.0, k * jnp.exp(last - b)))
        same_lv.append((ri // (2 * w)) == (ci // (2 * w)))

    qk_diag = _head_sum(q * k, ones_ref[...])
    q_in = (q * jnp.exp(b)).astype(BF16)
    k_out = (k * jnp.exp(bend - b)).astype(BF16)
    dec = jnp.exp(bend)
    q_lv = [t.astype(BF16) for t in q_lv]
    k_lv = [t.astype(BF16) for t in k_lv]
    v_b = v.astype(BF16)

    heads = range(HEADS)
    sls = [slice(h * HEAD_DIM, (h + 1) * HEAD_DIM) for h in heads]
    att = [jnp.zeros((CHUNK, CHUNK), F32) for _ in heads]
    for i in range(n_lv):
        pr = [_dg(q_lv[i][:, sl], k_lv[i][:, sl], _NT) for sl in sls]
        att = [att[h] + jnp.where(same_lv[i], pr[h], 0.0) for h in heads]
    s = [s_ref[h] for h in heads]
    o_in = [_dg(q_in[:, sls[h]], s[h].astype(BF16), _NT) for h in heads]
    o_at = [_dg(att[h].astype(BF16), v_b[:, sls[h]], _NN) for h in heads]
    vk = [_dg(v_b[:, sls[h]], k_out[:, sls[h]], _TN) for h in heads]
    for h in heads:
        s_ref[h] = s[h] * dec[:, sls[h]] + vk[h]
    for h in heads:
        sl = sls[h]
        o = o_at[h] + o_in[h] + qk_diag[:, sl] * v[:, sl]
        ms = jnp.mean(o * o, axis=-1, keepdims=True)
        o = o * lax.rsqrt(ms + RMS_EPS) * ng_ref[:, sl]
        y_o[:, sl] = (o * gate_ref[:, sl]).astype(y_o.dtype)


def _merge_kernel(x_ref, g_ref, wg_ref, ya_ref, yb_ref, yc_ref, pa_ref, pb_ref, pc_ref, wo_ref, o_ref):
    gates = _sigmoid(_norm_proj(x_ref, g_ref, wg_ref))
    m = (gates[:, 0:D_MODEL] * jnp.dot(ya_ref[...], pa_ref[...], preferred_element_type=F32)
         + gates[:, D_MODEL:2 * D_MODEL] * jnp.dot(yb_ref[...], pb_ref[...], preferred_element_type=F32)
         + gates[:, 2 * D_MODEL:3 * D_MODEL] * jnp.dot(yc_ref[...], pc_ref[...], preferred_element_type=F32))
    o_ref[...] = x_ref[...] + jnp.dot(m.astype(BF16), wo_ref[...], preferred_element_type=F32)


def _params(*sem):
    return pltpu.CompilerParams(dimension_semantics=sem, vmem_limit_bytes=VMEM_LIMIT)


def _full(shape):
    return pl.BlockSpec(shape, lambda *_: (0,) * len(shape))


def _edge_selectors():
    n_lv = len(LEVELS)
    sel = np.zeros((2 * n_lv * CHUNK, CHUNK), np.float32)
    for i, w in enumerate(LEVELS):
        for t in range(CHUNK):
            first = (t // w) * w
            if first > 0:
                sel[i * CHUNK + t, first - 1] = 1.0
            sel[(n_lv + i) * CHUNK + t, first + w - 1] = 1.0
    return jnp.asarray(np.concatenate([sel, sel, sel], axis=1), BF16)


def _band_bias(rel_bias):
    qi = np.arange(CHUNK)[:, None]
    kj = np.arange(BAND)[None, :]
    rel = np.clip(PAD_ROWS + qi - kj, -(CHUNK - 1), REL_CLIP) + (CHUNK - 1)
    return rel_bias.astype(F32)[:, rel]


def _layer(x, l, p, consts):
    bsz, seq, _ = x.shape
    tm = min(TOKEN_TILE, seq)
    n_t = seq // tm
    n_c = seq // CHUNK
    ones_bd, sel = consts
    g = p['norm_g'][l][None, :]

    tok = lambda w: pl.BlockSpec((None, tm, w), lambda b, t: (b, t, 0))
    chk = lambda w: pl.BlockSpec((None, CHUNK, w), lambda b, c: (b, c, 0))
    act = lambda dt: jax.ShapeDtypeStruct((bsz, seq, WIDTH), dt)

    w_in = p['w_in'][l]
    wa = w_in[:, :A_COLS]
    cols = np.r_[0:WIDTH, WIDTH + LORA:2 * WIDTH + LORA, 2 * WIDTH + LORA:3 * WIDTH + LORA,
                 3 * WIDTH + 2 * LORA:4 * WIDTH + 2 * LORA, WIDTH:WIDTH + LORA,
                 3 * WIDTH + LORA:3 * WIDTH + 2 * LORA]
    pa = jnp.stack([p['rwkv_w0'][l], p['rwkv_a0'][l], p['rwkv_k_k'][l], p['rwkv_k_a'][l],
                    p['rwkv_r_k'][l].reshape(WIDTH)] + [jnp.zeros((WIDTH,), F32)] * 3)
    r, lw, k2, v, kk, bv, bonus, gate_a = pl.pallas_call(
        _rwkv_prep_kernel,
        grid=(bsz, n_t),
        in_specs=[tok(D_MODEL), _full((1, D_MODEL)), _full((D_MODEL, A_COLS)), _full((1, A_COLS)),
                  _full((8, WIDTH)), _full((LORA, WIDTH)), _full((LORA, WIDTH)), _full((WIDTH, WIDTH))],
        out_specs=[tok(WIDTH)] * 8,
        out_shape=[act(F32)] * 8,
        scratch_shapes=[pltpu.VMEM((8, A_COLS), F32)],
        compiler_params=_params("parallel", "arbitrary"),
        name="rwkv_prep",
    )(x, g, wa[:, cols].astype(BF16), p['rwkv_mu'][l][cols][None, :], pa,
      p['rwkv_w2'][l], p['rwkv_a2'][l], ones_bd)

    ln = jnp.stack([p['rwkv_ln_w'][l], p['rwkv_ln_b'][l]] + [jnp.zeros((WIDTH,), F32)] * 6)
    y_a = pl.pallas_call(
        _rwkv_scan_kernel,
        grid=(bsz, n_c),
        in_specs=[chk(WIDTH)] * 8 + [_full((8, WIDTH))],
        out_specs=chk(WIDTH),
        out_shape=act(BF16),
        scratch_shapes=[pltpu.VMEM((HEADS, HEAD_DIM, HEAD_DIM), F32)],
        compiler_params=_params("parallel", "arbitrary"),
        name="rwkv_scan",
    )(r, lw, k2, v, kk, bv, bonus, gate_a, ln)

    pad_tiles = PAD_ROWS // tm
    wb = w_in[:, A_COLS:A_COLS + 4 * WIDTH].astype(BF16)
    qk = jnp.stack([jnp.tile(p['attn_q_norm'][l], HEADS), jnp.tile(p['attn_k_norm'][l], HEADS)]
                   + [jnp.zeros((WIDTH,), F32)] * 6)
    shift = lambda b, i: (b, jnp.maximum(i - pad_tiles, 0), 0)
    q, k_pad, v_pad, gate_b = pl.pallas_call(
        functools.partial(_attn_prep_kernel, pad_tiles=pad_tiles),
        grid=(bsz, n_t + pad_tiles),
        in_specs=[pl.BlockSpec((None, tm, D_MODEL), shift), _full((1, D_MODEL)),
                  _full((D_MODEL, 4 * WIDTH)), _full((8, WIDTH)), _full((WIDTH, WIDTH))],
        out_specs=[pl.BlockSpec((None, tm, WIDTH), shift), tok(WIDTH), tok(WIDTH),
                   pl.BlockSpec((None, tm, WIDTH), shift)],
        out_shape=[act(BF16), jax.ShapeDtypeStruct((bsz, seq + PAD_ROWS, WIDTH), BF16),
                   jax.ShapeDtypeStruct((bsz, seq + PAD_ROWS, WIDTH), BF16), act(F32)],
        compiler_params=_params("parallel", "arbitrary"),
        name="attn_prep",
    )(x, g, wb, qk, ones_bd)

    band = pl.BlockSpec((None, seq + PAD_ROWS, WIDTH), lambda b, c: (b, 0, 0))
    y_b = pl.pallas_call(
        _attn_band_kernel,
        grid=(bsz, n_c),
        in_specs=[chk(WIDTH), band, band, _full((HEADS, CHUNK, BAND)), chk(WIDTH)],
        out_specs=chk(WIDTH),
        out_shape=act(BF16),
        compiler_params=_params("parallel", "arbitrary"),
        name="attn_band",
    )(q, k_pad, v_pad, _band_bias(p['attn_rel_bias'][l]), gate_b)

    wc = w_in[:, A_COLS + 4 * WIDTH:A_COLS + 8 * WIDTH].astype(BF16)
    n_layers = p['hgrn_lb'].shape[0]
    qc, kc, lf, vc, gate_c = pl.pallas_call(
        functools.partial(_hgrn_prep_kernel, layer=l),
        grid=(bsz, n_t),
        in_specs=[tok(D_MODEL), _full((1, D_MODEL)), _full((D_MODEL, 4 * WIDTH)), _full((n_layers, WIDTH))],
        out_specs=[tok(WIDTH)] * 5,
        out_shape=[act(F32)] * 5,
        compiler_params=_params("parallel", "parallel"),
        name="hgrn_prep",
    )(x, g, wc, p['hgrn_lb'])

    y_c = pl.pallas_call(
        _hgrn_scan_kernel,
        grid=(bsz, n_c),
        in_specs=[chk(WIDTH)] * 5 + [_full(sel.shape), _full((WIDTH, WIDTH)), _full((1, WIDTH))],
        out_specs=chk(WIDTH),
        out_shape=act(BF16),
        scratch_shapes=[pltpu.VMEM((HEADS, HEAD_DIM, HEAD_DIM), F32)],
        compiler_params=_params("parallel", "arbitrary"),
        name="hgrn_scan",
    )(qc, kc, lf, vc, gate_c, sel, ones_bd, jnp.tile(p['hgrn_norm'][l], HEADS)[None, :])

    wg = w_in[:, A_COLS + 8 * WIDTH:].astype(BF16)
    return pl.pallas_call(
        _merge_kernel,
        grid=(bsz, n_t),
        in_specs=[tok(D_MODEL), _full((1, D_MODEL)), _full((D_MODEL, 3 * D_MODEL)),
                  tok(WIDTH), tok(WIDTH), tok(WIDTH),
                  _full((WIDTH, D_MODEL)), _full((WIDTH, D_MODEL)), _full((WIDTH, D_MODEL)),
                  _full((D_MODEL, D_MODEL))],
        out_specs=tok(D_MODEL),
        out_shape=jax.ShapeDtypeStruct(x.shape, x.dtype),
        compiler_params=_params("parallel", "parallel"),
        name="merge",
    )(x, g, wg, y_a, y_b, y_c, p['proj_a'][l].astype(BF16), p['proj_b'][l].astype(BF16),
      p['proj_c'][l].astype(BF16), p['w_out'][l].astype(BF16))


def kernel(x, norm_g, w_in, rwkv_mu, rwkv_w0, rwkv_w2, rwkv_a0, rwkv_a2, rwkv_k_k, rwkv_k_a, rwkv_r_k,
           rwkv_ln_w, rwkv_ln_b, attn_q_norm, attn_k_norm, attn_rel_bias, hgrn_lb, hgrn_norm,
           proj_a, proj_b, proj_c, w_out):
    p = dict(norm_g=norm_g, w_in=w_in, rwkv_mu=rwkv_mu, rwkv_w0=rwkv_w0, rwkv_w2=rwkv_w2,
             rwkv_a0=rwkv_a0, rwkv_a2=rwkv_a2, rwkv_k_k=rwkv_k_k, rwkv_k_a=rwkv_k_a, rwkv_r_k=rwkv_r_k,
             rwkv_ln_w=rwkv_ln_w, rwkv_ln_b=rwkv_ln_b, attn_q_norm=attn_q_norm, attn_k_norm=attn_k_norm,
             attn_rel_bias=attn_rel_bias, hgrn_lb=hgrn_lb, hgrn_norm=hgrn_norm,
             proj_a=proj_a, proj_b=proj_b, proj_c=proj_c, w_out=w_out)
    head = np.arange(WIDTH) // HEAD_DIM
    ones_bd = jnp.asarray(head[:, None] == head[None, :], BF16)
    consts = (ones_bd, _edge_selectors())
    for l in range(norm_g.shape[0]):
        x = _layer(x, l, p, consts)
    return x
```

```python
import functools

import numpy as np
import jax
import jax.numpy as jnp
from jax import lax
from jax.experimental import pallas as pl
from jax.experimental.pallas import tpu as pltpu

F32 = jnp.float32
BF16 = jnp.bfloat16

D_MODEL = 1024
CHUNK = 64
HEAD_DIM = 64
HEADS = 8
WIDTH = HEADS * HEAD_DIM
LORA = 64
A_COLS = 4 * WIDTH + 2 * LORA
PAST_CHUNKS = 8
PAD_ROWS = PAST_CHUNKS * CHUNK
REL_CLIP = 256
RMS_EPS = 1e-6
GN_EPS = 64e-5
LEVELS = (32, 16, 8, 4, 2, 1)

TOKEN_TILE = 512
RWKV_TOKEN_TILE = 256
SCAN_ROWS = 2
ATTN_Q_CHUNKS = 2
ATTN_HEAD_GROUP = 2
VMEM_LIMIT = 56 * 1024 * 1024

_NN = (((1,), (0,)), ((), ()))
_NT = (((1,), (1,)), ((), ()))
_TN = (((0,), (0,)), ((), ()))


def _dg(a, b, dims):
    return lax.dot_general(a, b, dims, preferred_element_type=F32)


def _hi_lo(a):
    hi = a.astype(BF16)
    lo = (a - hi.astype(F32)).astype(BF16)
    return hi, lo


def _mm3(a, b, dims=_NN):
    a1, a2 = _hi_lo(a)
    b1, b2 = _hi_lo(b)
    return _dg(a1, b1, dims) + (_dg(a1, b2, dims) + _dg(a2, b1, dims))


def _head_sum(a, ones_bd):
    return _dg(a.astype(BF16), ones_bd, _NN)


def _sigmoid(x):
    return 1.0 / (1.0 + jnp.exp(-x))


def _cumsum_rows(x):
    n = x.shape[0]
    rows = lax.broadcasted_iota(jnp.int32, x.shape, 0)
    s = 1
    while s < n:
        x = x + jnp.where(rows >= s, pltpu.roll(x, s, 0), 0.0)
        s *= 2
    return x


def _norm_proj(x_ref, g_ref, w_ref):
    x = x_ref[...]
    ms = jnp.mean(x * x, axis=-1, keepdims=True)
    h = x * lax.rsqrt(ms + RMS_EPS) * g_ref[...]
    return jnp.dot(h.astype(BF16), w_ref[...], preferred_element_type=F32)


def _rwkv_prep_kernel(x_ref, g_ref, w_ref, mu_ref, pa_ref, w2_ref, a2_ref, ones_ref,
                      r_o, lw_o, k_o, v_o, kk_o, b_o, bonus_o, gate_o, carry_ref):
    t = pl.program_id(1)
    za = _norm_proj(x_ref, g_ref, w_ref)
    tm = za.shape[0]

    @pl.when(t == 0)
    def _():
        carry_ref[...] = jnp.zeros_like(carry_ref)

    row = lax.broadcasted_iota(jnp.int32, (tm, 1), 0)
    prev = jnp.where(row == 0, carry_ref[7:8, :], pltpu.roll(za, 1, 0))
    carry_ref[...] = za[tm - 8:, :]
    xs = za + (prev - za) * mu_ref[...]

    r = xs[:, 0:WIDTH]
    k = xs[:, WIDTH:2 * WIDTH]
    v = xs[:, 2 * WIDTH:3 * WIDTH]
    gate = xs[:, 3 * WIDTH:4 * WIDTH]
    wl = xs[:, 4 * WIDTH:4 * WIDTH + LORA]
    al = xs[:, 4 * WIDTH + LORA:4 * WIDTH + 2 * LORA]
    w0, a0, k_k, k_a, r_k = (pa_ref[i:i + 1, :] for i in range(5))
    ones_bd = ones_ref[...]

    u = -(w0 + _mm3(jnp.tanh(wl), w2_ref[...]))
    softplus = jnp.maximum(u, 0.0) + jnp.log(1.0 + jnp.exp(-jnp.abs(u)))
    lw = -jnp.exp(-softplus - 0.5)
    a = _sigmoid(a0 + _mm3(al, a2_ref[...]))

    kkp = k * k_k
    kk = kkp / jnp.maximum(jnp.sqrt(_head_sum(kkp * kkp, ones_bd)), 1e-12)
    k2 = k * (1.0 + (a - 1.0) * k_a)

    r_o[...] = r.astype(r_o.dtype)
    lw_o[...] = lw
    k_o[...] = k2.astype(k_o.dtype)
    v_o[...] = v.astype(v_o.dtype)
    kk_o[...] = kk.astype(kk_o.dtype)
    b_o[...] = (kk * a).astype(b_o.dtype)
    bonus_o[...] = (_head_sum(r * k2 * r_k, ones_bd) * v).astype(bonus_o.dtype)
    gate_o[...] = (gate * _sigmoid(gate)).astype(gate_o.dtype)


def _rwkv_scan_kernel(r_ref, lw_ref, k_ref, v_ref, kk_ref, b_ref, bonus_ref, gate_ref, ln_ref,
                      y_o, s_ref):
    c = pl.program_id(1)

    @pl.when(c == 0)
    def _():
        s_ref[...] = jnp.zeros_like(s_ref)

    rows = r_ref.shape[0]
    a_t, b_t, k_t, r_f, r_t, b_h, k_h, v_b, pend = ([] for _ in range(9))
    for j in range(rows):
        lw = lw_ref[j]
        cum = _cumsum_rows(lw)
        cend = cum[CHUNK - 1:CHUNK, :]
        pinv = jnp.exp(-cum)
        dend = jnp.exp(cend - cum)
        kk, bv, k2 = kk_ref[j].astype(F32), b_ref[j].astype(F32), k_ref[j].astype(F32)
        a_t.append((-kk * jnp.exp(cum - lw)).astype(BF16))
        b_t.append((bv * pinv).astype(BF16))
        k_t.append((k2 * pinv).astype(BF16))
        r_f.append(r_ref[j].astype(F32) * jnp.exp(cum))
        r_t.append(r_f[j].astype(BF16))
        b_h.append((bv * dend).astype(BF16))
        k_h.append((k2 * dend).astype(BF16))
        v_b.append(v_ref[j])
        pend.append(jnp.exp(cend))

    ri = lax.broadcasted_iota(jnp.int32, (CHUNK, CHUNK), 0)
    ci = lax.broadcasted_iota(jnp.int32, (CHUNK, CHUNK), 1)
    strict = ri > ci
    incl = ri >= ci
    units = [(j, slice(h * HEAD_DIM, (h + 1) * HEAD_DIM)) for j in range(rows) for h in range(HEADS)]
    un = range(len(units))
    lo, hi = slice(0, HEAD_DIM), slice(HEAD_DIM, 2 * HEAD_DIM)

    sc = [_dg(jnp.concatenate([a_t[j][:, sl], r_t[j][:, sl]], axis=0),
              jnp.concatenate([b_t[j][:, sl], k_t[j][:, sl]], axis=0), _NT) for j, sl in units]
    m = [jnp.where(strict, sc[u][lo, lo], 0.0) for u in un]
    m_ak = [jnp.where(strict, sc[u][lo, hi], 0.0).astype(BF16) for u in un]
    l_rb = [jnp.where(incl, sc[u][hi, lo], 0.0).astype(BF16) for u in un]
    l_rk = [jnp.where(incl, sc[u][hi, hi], 0.0).astype(BF16) for u in un]
    vh = [v_b[j][:, sl] for j, sl in units]

    u0 = [_dg(m_ak[u], vh[u], _NN) for u in un]
    x = [jnp.concatenate([a_t[j][:, sl].astype(F32), u0[u]], axis=1) for u, (j, sl) in enumerate(units)]
    for i in range(6):
        mb = [m[u].astype(BF16) for u in un]
        x = [x[u] + _dg(mb[u], x[u].astype(BF16), _NN) for u in un]
        if i < 5:
            m = [_dg(mb[u], mb[u], _NN) for u in un]
    xb = [x[u].astype(BF16) for u in un]

    s = [s_ref[u] for u in un]
    sb = [s[u].astype(BF16) for u in un]
    lx = [_dg(l_rb[u], xb[u], _NN) for u in un]
    lv = [_dg(l_rk[u], vh[u], _NN) for u in un]
    r2 = [(r_f[j][:, sl] + lx[u][:, lo]).astype(BF16) for u, (j, sl) in enumerate(units)]
    y = [_dg(r2[u], sb[u], _NT) + lx[u][:, hi] + lv[u] for u in un]

    xq = [_dg(xb[u], b_h[j][:, sl], _TN) for u, (j, sl) in enumerate(units)]
    vk = [_dg(vh[u], k_h[j][:, sl], _TN) for u, (j, sl) in enumerate(units)]
    sq = [_dg(sb[u], xq[u][lo, :].astype(BF16), _NN) for u in un]
    for u, (j, sl) in enumerate(units):
        s_ref[u] = s[u] * pend[j][:, sl] + sq[u] + xq[u][hi, :] + vk[u]

    bonus = [bonus_ref[j].astype(F32) for j in range(rows)]
    gate = [gate_ref[j].astype(F32) for j in range(rows)]
    for u, (j, sl) in enumerate(units):
        mean = jnp.mean(y[u], axis=-1, keepdims=True)
        yc = y[u] - mean
        var = jnp.mean(yc * yc, axis=-1, keepdims=True)
        yn = yc * lax.rsqrt(var + GN_EPS) * ln_ref[0:1, sl] + ln_ref[1:2, sl] + bonus[j][:, sl]
        y_o[j, :, sl] = (yn * gate[j][:, sl]).astype(y_o.dtype)


def _attn_prep_kernel(x_ref, g_ref, w_ref, qk_ref, ones_ref, q_o, k_o, v_o, gate_o, *, pad_tiles):
    i = pl.program_id(1)

    @pl.when(i < pad_tiles)
    def _():
        k_o[...] = jnp.zeros_like(k_o)
        v_o[...] = jnp.zeros_like(v_o)

    @pl.when(i >= pad_tiles)
    def _():
        zb = _norm_proj(x_ref, g_ref, w_ref)
        ones_bd = ones_ref[...]
        q = zb[:, 0:WIDTH]
        k = zb[:, WIDTH:2 * WIDTH]
        gate = zb[:, 3 * WIDTH:4 * WIDTH]
        qn = q * lax.rsqrt(_head_sum(q * q, ones_bd) * (1.0 / HEAD_DIM) + RMS_EPS) * qk_ref[0:1, :]
        kn = k * lax.rsqrt(_head_sum(k * k, ones_bd) * (1.0 / HEAD_DIM) + RMS_EPS) * qk_ref[1:2, :]
        q_o[...] = (qn * HEAD_DIM ** -0.5).astype(q_o.dtype)
        k_o[...] = kn.astype(k_o.dtype)
        v_o[...] = zb[:, 2 * WIDTH:3 * WIDTH].astype(v_o.dtype)
        gate_o[...] = (gate * _sigmoid(gate)).astype(gate_o.dtype)


def _attn_band_kernel(q_ref, k_ref, v_ref, bias_ref, gate_ref, y_o):
    n = pl.program_id(1) * ATTN_Q_CHUNKS
    nq = ATTN_Q_CHUNKS * CHUNK
    nk = (ATTN_Q_CHUNKS + PAST_CHUNKS) * CHUNK
    start = pl.multiple_of(n * CHUNK, CHUNK)
    q = q_ref[...]
    kb = k_ref[pl.ds(start, nk), :]
    vb = v_ref[pl.ds(start, nk), :]
    qc = lax.broadcasted_iota(jnp.int32, (nq, nk), 0) // CHUNK
    kc = lax.broadcasted_iota(jnp.int32, (nq, nk), 1) // CHUNK
    valid = (kc >= qc) & (kc <= qc + PAST_CHUNKS) & (kc >= PAST_CHUNKS - n)
    gate = gate_ref[...].astype(F32)
    for h0 in range(0, HEADS, ATTN_HEAD_GROUP):
        heads = range(h0, h0 + ATTN_HEAD_GROUP)
        sls = {h: slice(h * HEAD_DIM, (h + 1) * HEAD_DIM) for h in heads}
        s = {h: _dg(q[:, sls[h]], kb[:, sls[h]], _NT) for h in heads}
        s = {h: jnp.where(valid, s[h] + bias_ref[h], -jnp.inf) for h in heads}
        p = {h: jnp.exp(s[h] - jnp.max(s[h], axis=-1, keepdims=True)) for h in heads}
        o = {h: _dg(p[h].astype(BF16), vb[:, sls[h]], _NN) for h in heads}
        for h in heads:
            oh = o[h] / jnp.sum(p[h], axis=-1, keepdims=True)
            y_o[:, sls[h]] = (oh * gate[:, sls[h]]).astype(y_o.dtype)


def _hgrn_prep_kernel(x_ref, g_ref, w_ref, lb_ref, q_o, k_o, lf_o, v_o, gate_o, *, layer):
    zc = _norm_proj(x_ref, g_ref, w_ref)
    lbp = lb_ref[...]
    e = jnp.exp(lbp - jnp.max(lbp, axis=0, keepdims=True))
    sm = e / jnp.sum(e, axis=0, keepdims=True)
    lb = jnp.sum(sm[0:layer + 1], axis=0, keepdims=True) - sm[0:1]
    q = zc[:, 0:WIDTH]
    gate = zc[:, 3 * WIDTH:4 * WIDTH]
    fg = lb + (1.0 - lb) * _sigmoid(zc[:, WIDTH:2 * WIDTH])
    q_o[...] = (q * _sigmoid(q)).astype(q_o.dtype)
    k_o[...] = (1.0 - fg).astype(k_o.dtype)
    lf_o[...] = jnp.log(fg)
    v_o[...] = zc[:, 2 * WIDTH:3 * WIDTH].astype(v_o.dtype)
    gate_o[...] = (gate * _sigmoid(gate)).astype(gate_o.dtype)


def _hgrn_scan_kernel(q_ref, k_ref, lf_ref, v_ref, gate_ref, sel_ref, ones_ref, ng_ref, y_o, s_ref):
    c = pl.program_id(1)

    @pl.when(c == 0)
    def _():
        s_ref[...] = jnp.zeros_like(s_ref)

    rows = q_ref.shape[0]
    n_lv = len(LEVELS)
    row = lax.broadcasted_iota(jnp.int32, (CHUNK, 1), 0)
    ri = lax.broadcasted_iota(jnp.int32, (CHUNK, CHUNK), 0)
    ci = lax.broadcasted_iota(jnp.int32, (CHUNK, CHUNK), 1)
    second = [((row // w) % 2) == 1 for w in LEVELS]
    same = [(ri // (2 * w)) == (ci // (2 * w)) for w in LEVELS]

    q_lv, k_lv, qk_diag, q_in, k_out, dec, v_f, v_b = ([] for _ in range(8))
    for j in range(rows):
        q, k, v_b16 = q_ref[j].astype(F32), k_ref[j].astype(F32), v_ref[j]
        b = _cumsum_rows(lf_ref[j])
        bend = b[CHUNK - 1:CHUNK, :]
        b1 = b.astype(BF16)
        rem = b - b1.astype(F32)
        b2 = rem.astype(BF16)
        b3 = (rem - b2.astype(F32)).astype(BF16)
        edges = _dg(sel_ref[...], jnp.concatenate([b1, b2, b3], axis=0), _NN)
        ql, kl = [], []
        for i in range(n_lv):
            before = edges[i * CHUNK:(i + 1) * CHUNK, :]
            last = edges[(n_lv + i) * CHUNK:(n_lv + i + 1) * CHUNK, :]
            ql.append(jnp.where(second[i], q * jnp.exp(b - before), 0.0).astype(BF16))
            kl.append(jnp.where(second[i], 0.0, k * jnp.exp(last - b)).astype(BF16))
        q_lv.append(ql)
        k_lv.append(kl)
        qk_diag.append(_head_sum(q * k, ones_ref[...]))
        q_in.append((q * jnp.exp(b)).astype(BF16))
        k_out.append((k * jnp.exp(bend - b)).astype(BF16))
        dec.append(jnp.exp(bend))
        v_f.append(v_b16.astype(F32))
        v_b.append(v_b16)

    units = [(j, slice(h * HEAD_DIM, (h + 1) * HEAD_DIM)) for j in range(rows) for h in range(HEADS)]
    un = range(len(units))
    att = [jnp.zeros((CHUNK, CHUNK), F32) for _ in un]
    for i in range(n_lv):
        pr = [_dg(q_lv[j][i][:, sl], k_lv[j][i][:, sl], _NT) for j, sl in units]
        att = [att[u] + jnp.where(same[i], pr[u], 0.0) for u in un]
    s = [s_ref[u] for u in un]
    o_in = [_dg(q_in[j][:, sl], s[u].astype(BF16), _NT) for u, (j, sl) in enumerate(units)]
    o_at = [_dg(att[u].astype(BF16), v_b[j][:, sl], _NN) for u, (j, sl) in enumerate(units)]
    vk = [_dg(v_b[j][:, sl], k_out[j][:, sl], _TN) for j, sl in units]
    for u, (j, sl) in enumerate(units):
        s_ref[u] = s[u] * dec[j][:, sl] + vk[u]
    gate = [gate_ref[j].astype(F32) for j in range(rows)]
    for u, (j, sl) in enumerate(units):
        o = o_at[u] + o_in[u] + qk_diag[j][:, sl] * v_f[j][:, sl]
        ms = jnp.mean(o * o, axis=-1, keepdims=True)
        o = o * lax.rsqrt(ms + RMS_EPS) * ng_ref[:, sl]
        y_o[j, :, sl] = (o * gate[j][:, sl]).astype(y_o.dtype)


def _merge_kernel(x_ref, g_ref, wg_ref, ya_ref, yb_ref, yc_ref, pa_ref, pb_ref, pc_ref, wo_ref, o_ref):
    gates = _sigmoid(_norm_proj(x_ref, g_ref, wg_ref))
    m = (gates[:, 0:D_MODEL] * jnp.dot(ya_ref[...], pa_ref[...], preferred_element_type=F32)
         + gates[:, D_MODEL:2 * D_MODEL] * jnp.dot(yb_ref[...], pb_ref[...], preferred_element_type=F32)
         + gates[:, 2 * D_MODEL:3 * D_MODEL] * jnp.dot(yc_ref[...], pc_ref[...], preferred_element_type=F32))
    o_ref[...] = x_ref[...] + jnp.dot(m.astype(BF16), wo_ref[...], preferred_element_type=F32)


def _params(*sem):
    return pltpu.CompilerParams(dimension_semantics=sem, vmem_limit_bytes=VMEM_LIMIT)


def _full(shape):
    return pl.BlockSpec(shape, lambda *_: (0,) * len(shape))


def _edge_selectors():
    n_lv = len(LEVELS)
    sel = np.zeros((2 * n_lv * CHUNK, CHUNK), np.float32)
    for i, w in enumerate(LEVELS):
        for t in range(CHUNK):
            first = (t // w) * w
            if first > 0:
                sel[i * CHUNK + t, first - 1] = 1.0
            sel[(n_lv + i) * CHUNK + t, first + w - 1] = 1.0
    return jnp.asarray(np.concatenate([sel, sel, sel], axis=1), BF16)


def _band_bias(rel_bias):
    qi = np.arange(ATTN_Q_CHUNKS * CHUNK)[:, None]
    kj = np.arange((ATTN_Q_CHUNKS + PAST_CHUNKS) * CHUNK)[None, :]
    rel = np.clip(PAD_ROWS + qi - kj, -(CHUNK - 1), REL_CLIP) + (CHUNK - 1)
    return rel_bias.astype(F32)[:, rel]


def _layer(x, l, p, consts):
    bsz, seq, _ = x.shape
    tm = min(TOKEN_TILE, seq)
    n_t = seq // tm
    n_c = seq // CHUNK
    ones_bd, sel = consts
    g = p['norm_g'][l][None, :]

    tma = min(RWKV_TOKEN_TILE, seq)
    tok = lambda w: pl.BlockSpec((None, tm, w), lambda b, t: (b, t, 0))
    toka = lambda w: pl.BlockSpec((None, tma, w), lambda b, t: (b, t, 0))
    rows = min(SCAN_ROWS, bsz)
    chk = lambda w: pl.BlockSpec((rows, CHUNK, w), lambda b, c: (b, c, 0))
    qck = lambda w: pl.BlockSpec((None, ATTN_Q_CHUNKS * CHUNK, w), lambda b, c: (b, c, 0))
    act = lambda dt: jax.ShapeDtypeStruct((bsz, seq, WIDTH), dt)

    w_in = p['w_in'][l]
    wa = w_in[:, :A_COLS]
    cols = np.r_[0:WIDTH, WIDTH + LORA:2 * WIDTH + LORA, 2 * WIDTH + LORA:3 * WIDTH + LORA,
                 3 * WIDTH + 2 * LORA:4 * WIDTH + 2 * LORA, WIDTH:WIDTH + LORA,
                 3 * WIDTH + LORA:3 * WIDTH + 2 * LORA]
    pa = jnp.stack([p['rwkv_w0'][l], p['rwkv_a0'][l], p['rwkv_k_k'][l], p['rwkv_k_a'][l],
                    p['rwkv_r_k'][l].reshape(WIDTH)] + [jnp.zeros((WIDTH,), F32)] * 3)
    r, lw, k2, v, kk, bv, bonus, gate_a = pl.pallas_call(
        _rwkv_prep_kernel,
        grid=(bsz, seq // tma),
        in_specs=[toka(D_MODEL), _full((1, D_MODEL)), _full((D_MODEL, A_COLS)), _full((1, A_COLS)),
                  _full((8, WIDTH)), _full((LORA, WIDTH)), _full((LORA, WIDTH)), _full((WIDTH, WIDTH))],
        out_specs=[toka(WIDTH)] * 8,
        out_shape=[act(BF16), act(F32)] + [act(BF16)] * 6,
        scratch_shapes=[pltpu.VMEM((8, A_COLS), F32)],
        compiler_params=_params("parallel", "arbitrary"),
        name="rwkv_prep",
    )(x, g, wa[:, cols].astype(BF16), p['rwkv_mu'][l][cols][None, :], pa,
      p['rwkv_w2'][l], p['rwkv_a2'][l], ones_bd)

    ln = jnp.stack([p['rwkv_ln_w'][l], p['rwkv_ln_b'][l]] + [jnp.zeros((WIDTH,), F32)] * 6)
    y_a = pl.pallas_call(
        _rwkv_scan_kernel,
        grid=(bsz // rows, n_c),
        in_specs=[chk(WIDTH)] * 8 + [_full((8, WIDTH))],
        out_specs=chk(WIDTH),
        out_shape=act(BF16),
        scratch_shapes=[pltpu.VMEM((rows * HEADS, HEAD_DIM, HEAD_DIM), F32)],
        compiler_params=_params("parallel", "arbitrary"),
        name="rwkv_scan",
    )(r, lw, k2, v, kk, bv, bonus, gate_a, ln)

    pad_tiles = PAD_ROWS // tm
    wb = w_in[:, A_COLS:A_COLS + 4 * WIDTH].astype(BF16)
    qk = jnp.stack([jnp.tile(p['attn_q_norm'][l], HEADS), jnp.tile(p['attn_k_norm'][l], HEADS)]
                   + [jnp.zeros((WIDTH,), F32)] * 6)
    shift = lambda b, i: (b, jnp.maximum(i - pad_tiles, 0), 0)
    q, k_pad, v_pad, gate_b = pl.pallas_call(
        functools.partial(_attn_prep_kernel, pad_tiles=pad_tiles),
        grid=(bsz, n_t + pad_tiles),
        in_specs=[pl.BlockSpec((None, tm, D_MODEL), shift), _full((1, D_MODEL)),
                  _full((D_MODEL, 4 * WIDTH)), _full((8, WIDTH)), _full((WIDTH, WIDTH))],
        out_specs=[pl.BlockSpec((None, tm, WIDTH), shift), tok(WIDTH), tok(WIDTH),
                   pl.BlockSpec((None, tm, WIDTH), shift)],
        out_shape=[act(BF16), jax.ShapeDtypeStruct((bsz, seq + PAD_ROWS, WIDTH), BF16),
                   jax.ShapeDtypeStruct((bsz, seq + PAD_ROWS, WIDTH), BF16), act(BF16)],
        compiler_params=_params("parallel", "arbitrary"),
        name="attn_prep",
    )(x, g, wb, qk, ones_bd)

    band = pl.BlockSpec((None, seq + PAD_ROWS, WIDTH), lambda b, c: (b, 0, 0))
    y_b = pl.pallas_call(
        _attn_band_kernel,
        grid=(bsz, n_c // ATTN_Q_CHUNKS),
        in_specs=[qck(WIDTH), band, band,
                  _full((HEADS, ATTN_Q_CHUNKS * CHUNK, (ATTN_Q_CHUNKS + PAST_CHUNKS) * CHUNK)), qck(WIDTH)],
        out_specs=qck(WIDTH),
        out_shape=act(BF16),
        compiler_params=_params("parallel", "arbitrary"),
        name="attn_band",
    )(q, k_pad, v_pad, _band_bias(p['attn_rel_bias'][l]), gate_b)

    wc = w_in[:, A_COLS + 4 * WIDTH:A_COLS + 8 * WIDTH].astype(BF16)
    n_layers = p['hgrn_lb'].shape[0]
    qc, kc, lf, vc, gate_c = pl.pallas_call(
        functools.partial(_hgrn_prep_kernel, layer=l),
        grid=(bsz, n_t),
        in_specs=[tok(D_MODEL), _full((1, D_MODEL)), _full((D_MODEL, 4 * WIDTH)), _full((n_layers, WIDTH))],
        out_specs=[tok(WIDTH)] * 5,
        out_shape=[act(BF16), act(BF16), act(F32), act(BF16), act(BF16)],
        compiler_params=_params("parallel", "parallel"),
        name="hgrn_prep",
    )(x, g, wc, p['hgrn_lb'])

    y_c = pl.pallas_call(
        _hgrn_scan_kernel,
        grid=(bsz // rows, n_c),
        in_specs=[chk(WIDTH)] * 5 + [_full(sel.shape), _full((WIDTH, WIDTH)), _full((1, WIDTH))],
        out_specs=chk(WIDTH),
        out_shape=act(BF16),
        scratch_shapes=[pltpu.VMEM((rows * HEADS, HEAD_DIM, HEAD_DIM), F32)],
        compiler_params=_params("parallel", "arbitrary"),
        name="hgrn_scan",
    )(qc, kc, lf, vc, gate_c, sel, ones_bd, jnp.tile(p['hgrn_norm'][l], HEADS)[None, :])

    wg = w_in[:, A_COLS + 8 * WIDTH:].astype(BF16)
    return pl.pallas_call(
        _merge_kernel,
        grid=(bsz, n_t),
        in_specs=[tok(D_MODEL), _full((1, D_MODEL)), _full((D_MODEL, 3 * D_MODEL)),
                  tok(WIDTH), tok(WIDTH), tok(WIDTH),
                  _full((WIDTH, D_MODEL)), _full((WIDTH, D_MODEL)), _full((WIDTH, D_MODEL)),
                  _full((D_MODEL, D_MODEL))],
        out_specs=tok(D_MODEL),
        out_shape=jax.ShapeDtypeStruct(x.shape, x.dtype),
        compiler_params=_params("parallel", "parallel"),
        name="merge",
    )(x, g, wg, y_a, y_b, y_c, p['proj_a'][l].astype(BF16), p['proj_b'][l].astype(BF16),
      p['proj_c'][l].astype(BF16), p['w_out'][l].astype(BF16))


def kernel(x, norm_g, w_in, rwkv_mu, rwkv_w0, rwkv_w2, rwkv_a0, rwkv_a2, rwkv_k_k, rwkv_k_a, rwkv_r_k,
           rwkv_ln_w, rwkv_ln_b, attn_q_norm, attn_k_norm, attn_rel_bias, hgrn_lb, hgrn_norm,
           proj_a, proj_b, proj_c, w_out):
    p = dict(norm_g=norm_g, w_in=w_in, rwkv_mu=rwkv_mu, rwkv_w0=rwkv_w0, rwkv_w2=rwkv_w2,
             rwkv_a0=rwkv_a0, rwkv_a2=rwkv_a2, rwkv_k_k=rwkv_k_k, rwkv_k_a=rwkv_k_a, rwkv_r_k=rwkv_r_k,
             rwkv_ln_w=rwkv_ln_w, rwkv_ln_b=rwkv_ln_b, attn_q_norm=attn_q_norm, attn_k_norm=attn_k_norm,
             attn_rel_bias=attn_rel_bias, hgrn_lb=hgrn_lb, hgrn_norm=hgrn_norm,
             proj_a=proj_a, proj_b=proj_b, proj_c=proj_c, w_out=w_out)
    head = np.arange(WIDTH) // HEAD_DIM
    ones_bd = jnp.asarray(head[:, None] == head[None, :], BF16)
    consts = (ones_bd, _edge_selectors())
    for l in range(norm_g.shape[0]):
        x = _layer(x, l, p, consts)
    return x
```

```python
import functools

import numpy as np
import jax
import jax.numpy as jnp
from jax import lax
from jax.experimental import pallas as pl
from jax.experimental.pallas import tpu as pltpu

F32 = jnp.float32
BF16 = jnp.bfloat16

D_MODEL = 1024
CHUNK = 64
HEAD_DIM = 64
HEADS = 8
WIDTH = HEADS * HEAD_DIM
LORA = 64
A_COLS = 4 * WIDTH + 2 * LORA
PAST_CHUNKS = 8
PAD_ROWS = PAST_CHUNKS * CHUNK
REL_CLIP = 256
RMS_EPS = 1e-6
GN_EPS = 64e-5
LEVELS = (32, 16, 8, 4, 2, 1)

TOKEN_TILE = 512
RWKV_TOKEN_TILE = 256
SCAN_ROWS = 2
SCAN_CHUNKS = 8
ATTN_Q_CHUNKS = 2
ATTN_HEAD_GROUP = 2
VMEM_LIMIT = 56 * 1024 * 1024

_NN = (((1,), (0,)), ((), ()))
_NT = (((1,), (1,)), ((), ()))
_TN = (((0,), (0,)), ((), ()))


def _dg(a, b, dims):
    return lax.dot_general(a, b, dims, preferred_element_type=F32)


def _hi_lo(a):
    hi = a.astype(BF16)
    lo = (a - hi.astype(F32)).astype(BF16)
    return hi, lo


def _mm3(a, b, dims=_NN):
    a1, a2 = _hi_lo(a)
    b1, b2 = _hi_lo(b)
    return _dg(a1, b1, dims) + (_dg(a1, b2, dims) + _dg(a2, b1, dims))


def _head_sum(a, ones_bd):
    return _dg(a.astype(BF16), ones_bd, _NN)


def _sigmoid(x):
    return 1.0 / (1.0 + jnp.exp(-x))


def _cumsum_rows(x):
    n = x.shape[0]
    rows = lax.broadcasted_iota(jnp.int32, x.shape, 0)
    s = 1
    while s < n:
        x = x + jnp.where(rows >= s, pltpu.roll(x, s, 0), 0.0)
        s *= 2
    return x


def _chunk_loop(n_chunks, step, body):
    def it(i, carry):
        body(pl.multiple_of(i * (step * CHUNK), step * CHUNK))
        return carry
    lax.fori_loop(0, n_chunks // step, it, 0)


def _norm_proj(x_ref, g_ref, w_ref):
    x = x_ref[...]
    ms = jnp.mean(x * x, axis=-1, keepdims=True)
    h = x * lax.rsqrt(ms + RMS_EPS) * g_ref[...]
    return jnp.dot(h.astype(BF16), w_ref[...], preferred_element_type=F32)


def _rwkv_prep_kernel(x_ref, g_ref, w_ref, mu_ref, pa_ref, w2_ref, a2_ref, ones_ref,
                      r_o, lw_o, k_o, v_o, kk_o, b_o, bonus_o, gate_o, carry_ref):
    t = pl.program_id(1)
    za = _norm_proj(x_ref, g_ref, w_ref)
    tm = za.shape[0]

    @pl.when(t == 0)
    def _():
        carry_ref[...] = jnp.zeros_like(carry_ref)

    row = lax.broadcasted_iota(jnp.int32, (tm, 1), 0)
    prev = jnp.where(row == 0, carry_ref[7:8, :], pltpu.roll(za, 1, 0))
    carry_ref[...] = za[tm - 8:, :]
    xs = za + (prev - za) * mu_ref[...]

    r = xs[:, 0:WIDTH]
    k = xs[:, WIDTH:2 * WIDTH]
    v = xs[:, 2 * WIDTH:3 * WIDTH]
    gate = xs[:, 3 * WIDTH:4 * WIDTH]
    wl = xs[:, 4 * WIDTH:4 * WIDTH + LORA]
    al = xs[:, 4 * WIDTH + LORA:4 * WIDTH + 2 * LORA]
    w0, a0, k_k, k_a, r_k = (pa_ref[i:i + 1, :] for i in range(5))
    ones_bd = ones_ref[...]

    u = -(w0 + _mm3(jnp.tanh(wl), w2_ref[...]))
    softplus = jnp.maximum(u, 0.0) + jnp.log(1.0 + jnp.exp(-jnp.abs(u)))
    lw = -jnp.exp(-softplus - 0.5)
    a = _sigmoid(a0 + _mm3(al, a2_ref[...]))

    kkp = k * k_k
    kk = kkp / jnp.maximum(jnp.sqrt(_head_sum(kkp * kkp, ones_bd)), 1e-12)
    k2 = k * (1.0 + (a - 1.0) * k_a)

    r_o[...] = r.astype(r_o.dtype)
    lw_o[...] = lw
    k_o[...] = k2.astype(k_o.dtype)
    v_o[...] = v.astype(v_o.dtype)
    kk_o[...] = kk.astype(kk_o.dtype)
    b_o[...] = (kk * a).astype(b_o.dtype)
    bonus_o[...] = (_head_sum(r * k2 * r_k, ones_bd) * v).astype(bonus_o.dtype)
    gate_o[...] = (gate * _sigmoid(gate)).astype(gate_o.dtype)


def _rwkv_scan_kernel(r_ref, *refs):
    s_ref = refs[-1]

    @pl.when(pl.program_id(1) == 0)
    def _():
        s_ref[...] = jnp.zeros_like(s_ref)

    _chunk_loop(r_ref.shape[1] // CHUNK, 1, functools.partial(_rwkv_chunk, r_ref, *refs))


def _rwkv_chunk(r_ref, lw_ref, k_ref, v_ref, kk_ref, b_ref, bonus_ref, gate_ref, ln_ref,
                y_o, s_ref, off):
    rows = r_ref.shape[0]
    at = pl.ds(off, CHUNK)
    a_t, b_t, k_t, r_f, r_t, b_h, k_h, v_b, pend = ([] for _ in range(9))
    for j in range(rows):
        lw = lw_ref[j, at, :]
        cum = _cumsum_rows(lw)
        cend = cum[CHUNK - 1:CHUNK, :]
        pinv = jnp.exp(-cum)
        dend = jnp.exp(cend - cum)
        kk, bv, k2 = kk_ref[j, at, :].astype(F32), b_ref[j, at, :].astype(F32), k_ref[j, at, :].astype(F32)
        a_t.append((-kk * jnp.exp(cum - lw)).astype(BF16))
        b_t.append((bv * pinv).astype(BF16))
        k_t.append((k2 * pinv).astype(BF16))
        r_f.append(r_ref[j, at, :].astype(F32) * jnp.exp(cum))
        r_t.append(r_f[j].astype(BF16))
        b_h.append((bv * dend).astype(BF16))
        k_h.append((k2 * dend).astype(BF16))
        v_b.append(v_ref[j, at, :])
        pend.append(jnp.exp(cend))

    ri = lax.broadcasted_iota(jnp.int32, (CHUNK, CHUNK), 0)
    ci = lax.broadcasted_iota(jnp.int32, (CHUNK, CHUNK), 1)
    strict = ri > ci
    incl = ri >= ci
    units = [(j, slice(h * HEAD_DIM, (h + 1) * HEAD_DIM)) for j in range(rows) for h in range(HEADS)]
    un = range(len(units))
    lo, hi = slice(0, HEAD_DIM), slice(HEAD_DIM, 2 * HEAD_DIM)

    sc = [_dg(jnp.concatenate([a_t[j][:, sl], r_t[j][:, sl]], axis=0),
              jnp.concatenate([b_t[j][:, sl], k_t[j][:, sl]], axis=0), _NT) for j, sl in units]
    m = [jnp.where(strict, sc[u][lo, lo], 0.0) for u in un]
    m_ak = [jnp.where(strict, sc[u][lo, hi], 0.0).astype(BF16) for u in un]
    l_rb = [jnp.where(incl, sc[u][hi, lo], 0.0).astype(BF16) for u in un]
    l_rk = [jnp.where(incl, sc[u][hi, hi], 0.0).astype(BF16) for u in un]
    vh = [v_b[j][:, sl] for j, sl in units]

    u0 = [_dg(m_ak[u], vh[u], _NN) for u in un]
    x = [jnp.concatenate([a_t[j][:, sl].astype(F32), u0[u]], axis=1) for u, (j, sl) in enumerate(units)]
    for i in range(6):
        mb = [m[u].astype(BF16) for u in un]
        x = [x[u] + _dg(mb[u], x[u].astype(BF16), _NN) for u in un]
        if i < 5:
            m = [_dg(mb[u], mb[u], _NN) for u in un]
    xb = [x[u].astype(BF16) for u in un]

    s = [s_ref[u] for u in un]
    sb = [s[u].astype(BF16) for u in un]
    lx = [_dg(l_rb[u], xb[u], _NN) for u in un]
    lv = [_dg(l_rk[u], vh[u], _NN) for u in un]
    r2 = [(r_f[j][:, sl] + lx[u][:, lo]).astype(BF16) for u, (j, sl) in enumerate(units)]
    y = [_dg(r2[u], sb[u], _NT) + lx[u][:, hi] + lv[u] for u in un]

    xq = [_dg(xb[u], b_h[j][:, sl], _TN) for u, (j, sl) in enumerate(units)]
    vk = [_dg(vh[u], k_h[j][:, sl], _TN) for u, (j, sl) in enumerate(units)]
    sq = [_dg(sb[u], xq[u][lo, :].astype(BF16), _NN) for u in un]
    for u, (j, sl) in enumerate(units):
        s_ref[u] = s[u] * pend[j][:, sl] + sq[u] + xq[u][hi, :] + vk[u]

    bonus = [bonus_ref[j, at, :].astype(F32) for j in range(rows)]
    gate = [gate_ref[j, at, :].astype(F32) for j in range(rows)]
    for u, (j, sl) in enumerate(units):
        mean = jnp.mean(y[u], axis=-1, keepdims=True)
        yc = y[u] - mean
        var = jnp.mean(yc * yc, axis=-1, keepdims=True)
        yn = yc * lax.rsqrt(var + GN_EPS) * ln_ref[0:1, sl] + ln_ref[1:2, sl] + bonus[j][:, sl]
        y_o[j, at, sl] = (yn * gate[j][:, sl]).astype(y_o.dtype)


def _attn_prep_kernel(x_ref, g_ref, w_ref, qk_ref, ones_ref, q_o, k_o, v_o, gate_o, *, pad_tiles):
    i = pl.program_id(1)

    @pl.when(i < pad_tiles)
    def _():
        k_o[...] = jnp.zeros_like(k_o)
        v_o[...] = jnp.zeros_like(v_o)

    @pl.when(i >= pad_tiles)
    def _():
        zb = _norm_proj(x_ref, g_ref, w_ref)
        ones_bd = ones_ref[...]
        q = zb[:, 0:WIDTH]
        k = zb[:, WIDTH:2 * WIDTH]
        gate = zb[:, 3 * WIDTH:4 * WIDTH]
        qn = q * lax.rsqrt(_head_sum(q * q, ones_bd) * (1.0 / HEAD_DIM) + RMS_EPS) * qk_ref[0:1, :]
        kn = k * lax.rsqrt(_head_sum(k * k, ones_bd) * (1.0 / HEAD_DIM) + RMS_EPS) * qk_ref[1:2, :]
        q_o[...] = (qn * HEAD_DIM ** -0.5).astype(q_o.dtype)
        k_o[...] = kn.astype(k_o.dtype)
        v_o[...] = zb[:, 2 * WIDTH:3 * WIDTH].astype(v_o.dtype)
        gate_o[...] = (gate * _sigmoid(gate)).astype(gate_o.dtype)


def _attn_band_kernel(q_ref, *refs):
    n_chunks = q_ref.shape[0] // CHUNK
    first = pl.program_id(1) * n_chunks
    _chunk_loop(n_chunks, ATTN_Q_CHUNKS, functools.partial(_attn_group, first, q_ref, *refs))


def _attn_group(first, q_ref, k_ref, v_ref, bias_ref, gate_ref, y_o, off):
    nq = ATTN_Q_CHUNKS * CHUNK
    nk = (ATTN_Q_CHUNKS + PAST_CHUNKS) * CHUNK
    at = pl.ds(off, nq)
    n = first + off // CHUNK
    start = pl.multiple_of(first * CHUNK + off, CHUNK)
    q = q_ref[at, :]
    kb = k_ref[pl.ds(start, nk), :]
    vb = v_ref[pl.ds(start, nk), :]
    qc = lax.broadcasted_iota(jnp.int32, (nq, nk), 0) // CHUNK
    kc = lax.broadcasted_iota(jnp.int32, (nq, nk), 1) // CHUNK
    valid = (kc >= qc) & (kc <= qc + PAST_CHUNKS) & (kc >= PAST_CHUNKS - n)
    gate = gate_ref[at, :].astype(F32)
    for h0 in range(0, HEADS, ATTN_HEAD_GROUP):
        heads = range(h0, h0 + ATTN_HEAD_GROUP)
        sls = {h: slice(h * HEAD_DIM, (h + 1) * HEAD_DIM) for h in heads}
        s = {h: _dg(q[:, sls[h]], kb[:, sls[h]], _NT) for h in heads}
        s = {h: jnp.where(valid, s[h] + bias_ref[h], -jnp.inf) for h in heads}
        p = {h: jnp.exp(s[h] - jnp.max(s[h], axis=-1, keepdims=True)) for h in heads}
        o = {h: _dg(p[h].astype(BF16), vb[:, sls[h]], _NN) for h in heads}
        for h in heads:
            oh = o[h] / jnp.sum(p[h], axis=-1, keepdims=True)
            y_o[at, sls[h]] = (oh * gate[:, sls[h]]).astype(y_o.dtype)


def _hgrn_prep_kernel(x_ref, g_ref, w_ref, lb_ref, q_o, k_o, lf_o, v_o, gate_o, *, layer):
    zc = _norm_proj(x_ref, g_ref, w_ref)
    lbp = lb_ref[...]
    e = jnp.exp(lbp - jnp.max(lbp, axis=0, keepdims=True))
    sm = e / jnp.sum(e, axis=0, keepdims=True)
    lb = jnp.sum(sm[0:layer + 1], axis=0, keepdims=True) - sm[0:1]
    q = zc[:, 0:WIDTH]
    gate = zc[:, 3 * WIDTH:4 * WIDTH]
    fg = lb + (1.0 - lb) * _sigmoid(zc[:, WIDTH:2 * WIDTH])
    q_o[...] = (q * _sigmoid(q)).astype(q_o.dtype)
    k_o[...] = (1.0 - fg).astype(k_o.dtype)
    lf_o[...] = jnp.log(fg)
    v_o[...] = zc[:, 2 * WIDTH:3 * WIDTH].astype(v_o.dtype)
    gate_o[...] = (gate * _sigmoid(gate)).astype(gate_o.dtype)


def _hgrn_scan_kernel(q_ref, *refs):
    s_ref = refs[-1]

    @pl.when(pl.program_id(1) == 0)
    def _():
        s_ref[...] = jnp.zeros_like(s_ref)

    _chunk_loop(q_ref.shape[1] // CHUNK, 1, functools.partial(_hgrn_chunk, q_ref, *refs))


def _hgrn_chunk(q_ref, k_ref, lf_ref, v_ref, gate_ref, sel_ref, ones_ref, ng_ref, y_o, s_ref, off):
    rows = q_ref.shape[0]
    at = pl.ds(off, CHUNK)
    n_lv = len(LEVELS)
    row = lax.broadcasted_iota(jnp.int32, (CHUNK, 1), 0)
    ri = lax.broadcasted_iota(jnp.int32, (CHUNK, CHUNK), 0)
    ci = lax.broadcasted_iota(jnp.int32, (CHUNK, CHUNK), 1)
    second = [((row // w) % 2) == 1 for w in LEVELS]
    same = [(ri // (2 * w)) == (ci // (2 * w)) for w in LEVELS]

    q_lv, k_lv, qk_diag, q_in, k_out, dec, v_f, v_b = ([] for _ in range(8))
    for j in range(rows):
        q, k, v_b16 = q_ref[j, at, :].astype(F32), k_ref[j, at, :].astype(F32), v_ref[j, at, :]
        b = _cumsum_rows(lf_ref[j, at, :])
        bend = b[CHUNK - 1:CHUNK, :]
        b1 = b.astype(BF16)
        rem = b - b1.astype(F32)
        b2 = rem.astype(BF16)
        b3 = (rem - b2.astype(F32)).astype(BF16)
        edges = _dg(sel_ref[...], jnp.concatenate([b1, b2, b3], axis=0), _NN)
        ql, kl = [], []
        for i in range(n_lv):
            before = edges[i * CHUNK:(i + 1) * CHUNK, :]
            last = edges[(n_lv + i) * CHUNK:(n_lv + i + 1) * CHUNK, :]
            e = jnp.exp(jnp.where(second[i], b - before, last - b))
            ql.append(jnp.where(second[i], q * e, 0.0).astype(BF16))
            kl.append(jnp.where(second[i], 0.0, k * e).astype(BF16))
        q_lv.append(ql)
        k_lv.append(kl)
        qk_diag.append(_head_sum(q * k, ones_ref[...]))
        q_in.append((q * jnp.exp(b)).astype(BF16))
        k_out.append((k * jnp.exp(bend - b)).astype(BF16))
        dec.append(jnp.exp(bend))
        v_f.append(v_b16.astype(F32))
        v_b.append(v_b16)

    units = [(j, slice(h * HEAD_DIM, (h + 1) * HEAD_DIM)) for j in range(rows) for h in range(HEADS)]
    un = range(len(units))
    att = [jnp.zeros((CHUNK, CHUNK), F32) for _ in un]
    for i in range(n_lv):
        pr = [_dg(q_lv[j][i][:, sl], k_lv[j][i][:, sl], _NT) for j, sl in units]
        att = [att[u] + jnp.where(same[i], pr[u], 0.0) for u in un]
    s = [s_ref[u] for u in un]
    o_in = [_dg(q_in[j][:, sl], s[u].astype(BF16), _NT) for u, (j, sl) in enumerate(units)]
    o_at = [_dg(att[u].astype(BF16), v_b[j][:, sl], _NN) for u, (j, sl) in enumerate(units)]
    vk = [_dg(v_b[j][:, sl], k_out[j][:, sl], _TN) for j, sl in units]
    for u, (j, sl) in enumerate(units):
        s_ref[u] = s[u] * dec[j][:, sl] + vk[u]
    gate = [gate_ref[j, at, :].astype(F32) for j in range(rows)]
    for u, (j, sl) in enumerate(units):
        o = o_at[u] + o_in[u] + qk_diag[j][:, sl] * v_f[j][:, sl]
        ms = jnp.mean(o * o, axis=-1, keepdims=True)
        o = o * lax.rsqrt(ms + RMS_EPS) * ng_ref[:, sl]
        y_o[j, at, sl] = (o * gate[j][:, sl]).astype(y_o.dtype)


def _merge_kernel(x_ref, g_ref, wg_ref, ya_ref, yb_ref, yc_ref, pa_ref, pb_ref, pc_ref, wo_ref, o_ref):
    gates = _sigmoid(_norm_proj(x_ref, g_ref, wg_ref))
    m = (gates[:, 0:D_MODEL] * jnp.dot(ya_ref[...], pa_ref[...], preferred_element_type=F32)
         + gates[:, D_MODEL:2 * D_MODEL] * jnp.dot(yb_ref[...], pb_ref[...], preferred_element_type=F32)
         + gates[:, 2 * D_MODEL:3 * D_MODEL] * jnp.dot(yc_ref[...], pc_ref[...], preferred_element_type=F32))
    o_ref[...] = x_ref[...] + jnp.dot(m.astype(BF16), wo_ref[...], preferred_element_type=F32)


def _params(*sem):
    return pltpu.CompilerParams(dimension_semantics=sem, vmem_limit_bytes=VMEM_LIMIT)


def _full(shape):
    return pl.BlockSpec(shape, lambda *_: (0,) * len(shape))


def _edge_selectors():
    n_lv = len(LEVELS)
    sel = np.zeros((2 * n_lv * CHUNK, CHUNK), np.float32)
    for i, w in enumerate(LEVELS):
        for t in range(CHUNK):
            first = (t // w) * w
            if first > 0:
                sel[i * CHUNK + t, first - 1] = 1.0
            sel[(n_lv + i) * CHUNK + t, first + w - 1] = 1.0
    return jnp.asarray(np.concatenate([sel, sel, sel], axis=1), BF16)


def _band_bias(rel_bias):
    nq = ATTN_Q_CHUNKS * CHUNK
    nk = (ATTN_Q_CHUNKS + PAST_CHUNKS) * CHUNK
    period = nk + nq
    j = np.arange(period)
    far = REL_CLIP + CHUNK - 1
    w = rel_bias.astype(F32)[:, np.where(j < nk, np.clip(PAD_ROWS - j, -(CHUNK - 1), REL_CLIP) + CHUNK - 1, far)]
    heads = w.shape[0]
    return jnp.tile(w, (1, nq))[:, :nq * (period - 1)].reshape(heads, nq, period - 1)[:, :, :nk]


def _layer(x, l, p, consts):
    bsz, seq, _ = x.shape
    tm = min(TOKEN_TILE, seq)
    n_t = seq // tm
    n_c = seq // CHUNK
    ones_bd, sel = consts
    g = p['norm_g'][l][None, :]

    tma = min(RWKV_TOKEN_TILE, seq)
    tok = lambda w: pl.BlockSpec((None, tm, w), lambda b, t: (b, t, 0))
    toka = lambda w: pl.BlockSpec((None, tma, w), lambda b, t: (b, t, 0))
    rows = min(SCAN_ROWS, bsz)
    blk = min(SCAN_CHUNKS, n_c) * CHUNK
    chk = lambda w: pl.BlockSpec((rows, blk, w), lambda b, c: (b, c, 0))
    qck = lambda w: pl.BlockSpec((None, blk, w), lambda b, c: (b, c, 0))
    act = lambda dt: jax.ShapeDtypeStruct((bsz, seq, WIDTH), dt)

    w_in = p['w_in'][l]
    wa = w_in[:, :A_COLS]
    cols = np.r_[0:WIDTH, WIDTH + LORA:2 * WIDTH + LORA, 2 * WIDTH + LORA:3 * WIDTH + LORA,
                 3 * WIDTH + 2 * LORA:4 * WIDTH + 2 * LORA, WIDTH:WIDTH + LORA,
                 3 * WIDTH + LORA:3 * WIDTH + 2 * LORA]
    pa = jnp.stack([p['rwkv_w0'][l], p['rwkv_a0'][l], p['rwkv_k_k'][l], p['rwkv_k_a'][l],
                    p['rwkv_r_k'][l].reshape(WIDTH)] + [jnp.zeros((WIDTH,), F32)] * 3)
    r, lw, k2, v, kk, bv, bonus, gate_a = pl.pallas_call(
        _rwkv_prep_kernel,
        grid=(bsz, seq // tma),
        in_specs=[toka(D_MODEL), _full((1, D_MODEL)), _full((D_MODEL, A_COLS)), _full((1, A_COLS)),
                  _full((8, WIDTH)), _full((LORA, WIDTH)), _full((LORA, WIDTH)), _full((WIDTH, WIDTH))],
        out_specs=[toka(WIDTH)] * 8,
        out_shape=[act(BF16), act(F32)] + [act(BF16)] * 6,
        scratch_shapes=[pltpu.VMEM((8, A_COLS), F32)],
        compiler_params=_params("parallel", "arbitrary"),
        name="rwkv_prep",
    )(x, g, wa[:, cols].astype(BF16), p['rwkv_mu'][l][cols][None, :], pa,
      p['rwkv_w2'][l], p['rwkv_a2'][l], ones_bd)

    ln = jnp.stack([p['rwkv_ln_w'][l], p['rwkv_ln_b'][l]] + [jnp.zeros((WIDTH,), F32)] * 6)
    y_a = pl.pallas_call(
        _rwkv_scan_kernel,
        grid=(bsz // rows, seq // blk),
        in_specs=[chk(WIDTH)] * 8 + [_full((8, WIDTH))],
        out_specs=chk(WIDTH),
        out_shape=act(BF16),
        scratch_shapes=[pltpu.VMEM((rows * HEADS, HEAD_DIM, HEAD_DIM), F32)],
        compiler_params=_params("parallel", "arbitrary"),
        name="rwkv_scan",
    )(r, lw, k2, v, kk, bv, bonus, gate_a, ln)

    pad_tiles = PAD_ROWS // tm
    wb = w_in[:, A_COLS:A_COLS + 4 * WIDTH].astype(BF16)
    qk = jnp.stack([jnp.tile(p['attn_q_norm'][l], HEADS), jnp.tile(p['attn_k_norm'][l], HEADS)]
                   + [jnp.zeros((WIDTH,), F32)] * 6)
    shift = lambda b, i: (b, jnp.maximum(i - pad_tiles, 0), 0)
    q, k_pad, v_pad, gate_b = pl.pallas_call(
        functools.partial(_attn_prep_kernel, pad_tiles=pad_tiles),
        grid=(bsz, n_t + pad_tiles),
        in_specs=[pl.BlockSpec((None, tm, D_MODEL), shift), _full((1, D_MODEL)),
                  _full((D_MODEL, 4 * WIDTH)), _full((8, WIDTH)), _full((WIDTH, WIDTH))],
        out_specs=[pl.BlockSpec((None, tm, WIDTH), shift), tok(WIDTH), tok(WIDTH),
                   pl.BlockSpec((None, tm, WIDTH), shift)],
        out_shape=[act(BF16), jax.ShapeDtypeStruct((bsz, seq + PAD_ROWS, WIDTH), BF16),
                   jax.ShapeDtypeStruct((bsz, seq + PAD_ROWS, WIDTH), BF16), act(BF16)],
        compiler_params=_params("parallel", "arbitrary"),
        name="attn_prep",
    )(x, g, wb, qk, ones_bd)

    band = pl.BlockSpec((None, seq + PAD_ROWS, WIDTH), lambda b, c: (b, 0, 0))
    y_b = pl.pallas_call(
        _attn_band_kernel,
        grid=(bsz, seq // blk),
        in_specs=[qck(WIDTH), band, band,
                  _full((HEADS, ATTN_Q_CHUNKS * CHUNK, (ATTN_Q_CHUNKS + PAST_CHUNKS) * CHUNK)), qck(WIDTH)],
        out_specs=qck(WIDTH),
        out_shape=act(BF16),
        compiler_params=_params("parallel", "arbitrary"),
        name="attn_band",
    )(q, k_pad, v_pad, _band_bias(p['attn_rel_bias'][l]), gate_b)

    wc = w_in[:, A_COLS + 4 * WIDTH:A_COLS + 8 * WIDTH].astype(BF16)
    n_layers = p['hgrn_lb'].shape[0]
    qc, kc, lf, vc, gate_c = pl.pallas_call(
        functools.partial(_hgrn_prep_kernel, layer=l),
        grid=(bsz, n_t),
        in_specs=[tok(D_MODEL), _full((1, D_MODEL)), _full((D_MODEL, 4 * WIDTH)), _full((n_layers, WIDTH))],
        out_specs=[tok(WIDTH)] * 5,
        out_shape=[act(BF16), act(BF16), act(F32), act(BF16), act(BF16)],
        compiler_params=_params("parallel", "parallel"),
        name="hgrn_prep",
    )(x, g, wc, p['hgrn_lb'])

    y_c = pl.pallas_call(
        _hgrn_scan_kernel,
        grid=(bsz // rows, seq // blk),
        in_specs=[chk(WIDTH)] * 5 + [_full(sel.shape), _full((WIDTH, WIDTH)), _full((1, WIDTH))],
        out_specs=chk(WIDTH),
        out_shape=act(BF16),
        scratch_shapes=[pltpu.VMEM((rows * HEADS, HEAD_DIM, HEAD_DIM), F32)],
        compiler_params=_params("parallel", "arbitrary"),
        name="hgrn_scan",
    )(qc, kc, lf, vc, gate_c, sel, ones_bd, jnp.tile(p['hgrn_norm'][l], HEADS)[None, :])

    wg = w_in[:, A_COLS + 8 * WIDTH:].astype(BF16)
    return pl.pallas_call(
        _merge_kernel,
        grid=(bsz, n_t),
        in_specs=[tok(D_MODEL), _full((1, D_MODEL)), _full((D_MODEL, 3 * D_MODEL)),
                  tok(WIDTH), tok(WIDTH), tok(WIDTH),
                  _full((WIDTH, D_MODEL)), _full((WIDTH, D_MODEL)), _full((WIDTH, D_MODEL)),
                  _full((D_MODEL, D_MODEL))],
        out_specs=tok(D_MODEL),
        out_shape=jax.ShapeDtypeStruct(x.shape, x.dtype),
        compiler_params=_params("parallel", "parallel"),
        name="merge",
    )(x, g, wg, y_a, y_b, y_c, p['proj_a'][l].astype(BF16), p['proj_b'][l].astype(BF16),
      p['proj_c'][l].astype(BF16), p['w_out'][l].astype(BF16))


def kernel(x, norm_g, w_in, rwkv_mu, rwkv_w0, rwkv_w2, rwkv_a0, rwkv_a2, rwkv_k_k, rwkv_k_a, rwkv_r_k,
           rwkv_ln_w, rwkv_ln_b, attn_q_norm, attn_k_norm, attn_rel_bias, hgrn_lb, hgrn_norm,
           proj_a, proj_b, proj_c, w_out):
    p = dict(norm_g=norm_g, w_in=w_in, rwkv_mu=rwkv_mu, rwkv_w0=rwkv_w0, rwkv_w2=rwkv_w2,
             rwkv_a0=rwkv_a0, rwkv_a2=rwkv_a2, rwkv_k_k=rwkv_k_k, rwkv_k_a=rwkv_k_a, rwkv_r_k=rwkv_r_k,
             rwkv_ln_w=rwkv_ln_w, rwkv_ln_b=rwkv_ln_b, attn_q_norm=attn_q_norm, attn_k_norm=attn_k_norm,
             attn_rel_bias=attn_rel_bias, hgrn_lb=hgrn_lb, hgrn_norm=hgrn_norm,
             proj_a=proj_a, proj_b=proj_b, proj_c=proj_c, w_out=w_out)
    head = np.arange(WIDTH) // HEAD_DIM
    ones_bd = jnp.asarray(head[:, None] == head[None, :], BF16)
    consts = (ones_bd, _edge_selectors())
    for l in range(norm_g.shape[0]):
        x = _layer(x, l, p, consts)
    return x
```

```python
import functools

import numpy as np
import jax
import jax.numpy as jnp
from jax import lax
from jax.experimental import pallas as pl
from jax.experimental.pallas import tpu as pltpu

F32 = jnp.float32
BF16 = jnp.bfloat16

D_MODEL = 1024
CHUNK = 64
HEAD_DIM = 64
HEADS = 8
WIDTH = HEADS * HEAD_DIM
LORA = 64
A_COLS = 4 * WIDTH + 2 * LORA
PAST_CHUNKS = 8
PAD_ROWS = PAST_CHUNKS * CHUNK
REL_CLIP = 256
RMS_EPS = 1e-6
GN_EPS = 64e-5
LEVELS = (32, 16, 8, 4, 2, 1)

TOKEN_TILE = 512
PREP_SUB_ROWS = 256
SCAN_ROWS = 2
SCAN_CHUNKS = 8
ATTN_Q_CHUNKS = 2
ATTN_HEAD_GROUP = 2
VMEM_LIMIT = 56 * 1024 * 1024

_NN = (((1,), (0,)), ((), ()))
_NT = (((1,), (1,)), ((), ()))
_TN = (((0,), (0,)), ((), ()))


def _dg(a, b, dims):
    return lax.dot_general(a, b, dims, preferred_element_type=F32)


def _hi_lo(a):
    hi = a.astype(BF16)
    lo = (a - hi.astype(F32)).astype(BF16)
    return hi, lo


def _mm3(a, b, dims=_NN):
    a1, a2 = _hi_lo(a)
    b1, b2 = _hi_lo(b)
    return _dg(a1, b1, dims) + (_dg(a1, b2, dims) + _dg(a2, b1, dims))


def _head_sum(a, ones_bd):
    return _dg(a.astype(BF16), ones_bd, _NN)


def _sigmoid(x):
    return 0.5 * jnp.tanh(0.5 * x) + 0.5


def _cumsum_rows(x):
    n = x.shape[0]
    rows = lax.broadcasted_iota(jnp.int32, x.shape, 0)
    s = 1
    while s < n:
        x = x + jnp.where(rows >= s, pltpu.roll(x, s, 0), 0.0)
        s *= 2
    return x


def _chunk_loop(n_chunks, step, body):
    def it(i, carry):
        body(pl.multiple_of(i * (step * CHUNK), step * CHUNK))
        return carry
    lax.fori_loop(0, n_chunks // step, it, 0)


def _sub_blocks(n_rows):
    sub = min(PREP_SUB_ROWS, n_rows)
    return [slice(i * sub, (i + 1) * sub) for i in range(n_rows // sub)]


def _norm_proj(x, g_ref, w_ref):
    ms = jnp.mean(x * x, axis=-1, keepdims=True)
    h = x * lax.rsqrt(ms + RMS_EPS) * g_ref[...]
    return jnp.dot(h.astype(BF16), w_ref[...], preferred_element_type=F32)


def _rwkv_prep_kernel(x_ref, g_ref, w_ref, mu_ref, pa_ref, w2_ref, a2_ref, ones_ref,
                      r_o, lw_o, k_o, v_o, kk_o, b_o, bonus_o, gate_o, carry_ref):
    @pl.when(pl.program_id(1) == 0)
    def _():
        carry_ref[...] = jnp.zeros_like(carry_ref)

    w0, a0, k_k, k_a, r_k = (pa_ref[i:i + 1, :] for i in range(5))
    ones_bd = ones_ref[...]
    last = carry_ref[7:8, :]
    for rs in _sub_blocks(x_ref.shape[0]):
        za = _norm_proj(x_ref[rs, :], g_ref, w_ref)
        sub = za.shape[0]
        row = lax.broadcasted_iota(jnp.int32, (sub, 1), 0)
        prev = jnp.where(row == 0, last, pltpu.roll(za, 1, 0))
        last = za[sub - 1:sub, :]
        tail = za[sub - 8:, :]
        xs = za + (prev - za) * mu_ref[...]

        r = xs[:, 0:WIDTH]
        k = xs[:, WIDTH:2 * WIDTH]
        v = xs[:, 2 * WIDTH:3 * WIDTH]
        gate = xs[:, 3 * WIDTH:4 * WIDTH]
        wl = xs[:, 4 * WIDTH:4 * WIDTH + LORA]
        al = xs[:, 4 * WIDTH + LORA:4 * WIDTH + 2 * LORA]

        u = -(w0 + _mm3(jnp.tanh(wl), w2_ref[...]))
        softplus = jnp.maximum(u, 0.0) + jnp.log(1.0 + jnp.exp(-jnp.abs(u)))
        lw = -jnp.exp(-softplus - 0.5)
        a = _sigmoid(a0 + _mm3(al, a2_ref[...]))

        kkp = k * k_k
        kk = kkp * lax.rsqrt(jnp.maximum(_head_sum(kkp * kkp, ones_bd), 1e-24))
        k2 = k * (1.0 + (a - 1.0) * k_a)

        r_o[rs, :] = r.astype(r_o.dtype)
        lw_o[rs, :] = lw
        k_o[rs, :] = k2.astype(k_o.dtype)
        v_o[rs, :] = v.astype(v_o.dtype)
        kk_o[rs, :] = kk.astype(kk_o.dtype)
        b_o[rs, :] = (kk * a).astype(b_o.dtype)
        bonus_o[rs, :] = (_head_sum(r * k2 * r_k, ones_bd) * v).astype(bonus_o.dtype)
        gate_o[rs, :] = (gate * _sigmoid(gate)).astype(gate_o.dtype)
    carry_ref[...] = tail


def _rwkv_scan_kernel(r_ref, *refs):
    s_ref = refs[-1]

    @pl.when(pl.program_id(1) == 0)
    def _():
        s_ref[...] = jnp.zeros_like(s_ref)

    _chunk_loop(r_ref.shape[1] // CHUNK, 1, functools.partial(_rwkv_chunk, r_ref, *refs))


def _rwkv_chunk(r_ref, lw_ref, k_ref, v_ref, kk_ref, b_ref, bonus_ref, gate_ref, ln_ref,
                y_o, s_ref, off):
    rows = r_ref.shape[0]
    at = pl.ds(off, CHUNK)
    a_t, b_t, k_t, r_f, r_t, b_h, k_h, v_b, pend = ([] for _ in range(9))
    for j in range(rows):
        lw = lw_ref[j, at, :]
        cum = _cumsum_rows(lw)
        cend = cum[CHUNK - 1:CHUNK, :]
        pinv = jnp.exp(-cum)
        dend = jnp.exp(cend - cum)
        kk, bv, k2 = kk_ref[j, at, :].astype(F32), b_ref[j, at, :].astype(F32), k_ref[j, at, :].astype(F32)
        a_t.append((-kk * jnp.exp(cum - lw)).astype(BF16))
        b_t.append((bv * pinv).astype(BF16))
        k_t.append((k2 * pinv).astype(BF16))
        r_f.append(r_ref[j, at, :].astype(F32) * jnp.exp(cum))
        r_t.append(r_f[j].astype(BF16))
        b_h.append((bv * dend).astype(BF16))
        k_h.append((k2 * dend).astype(BF16))
        v_b.append(v_ref[j, at, :])
        pend.append(jnp.exp(cend))

    ri = lax.broadcasted_iota(jnp.int32, (CHUNK, CHUNK), 0)
    ci = lax.broadcasted_iota(jnp.int32, (CHUNK, CHUNK), 1)
    strict = ri > ci
    incl = ri >= ci
    units = [(j, slice(h * HEAD_DIM, (h + 1) * HEAD_DIM)) for j in range(rows) for h in range(HEADS)]
    un = range(len(units))
    lo, hi = slice(0, HEAD_DIM), slice(HEAD_DIM, 2 * HEAD_DIM)

    sc = [_dg(jnp.concatenate([a_t[j][:, sl], r_t[j][:, sl]], axis=0),
              jnp.concatenate([b_t[j][:, sl], k_t[j][:, sl]], axis=0), _NT) for j, sl in units]
    m = [jnp.where(strict, sc[u][lo, lo], 0.0) for u in un]
    m_ak = [jnp.where(strict, sc[u][lo, hi], 0.0).astype(BF16) for u in un]
    l_rb = [jnp.where(incl, sc[u][hi, lo], 0.0).astype(BF16) for u in un]
    l_rk = [jnp.where(incl, sc[u][hi, hi], 0.0).astype(BF16) for u in un]
    vh = [v_b[j][:, sl] for j, sl in units]

    u0 = [_dg(m_ak[u], vh[u], _NN) for u in un]
    t = [jnp.where(ri == ci, 1.0, m[u]) for u in un]
    mb = [m[u].astype(BF16) for u in un]
    mb = [_dg(mb[u], mb[u], _NN).astype(BF16) for u in un]
    for i in range(1, 6):
        if i < 5:
            tm = [_dg(jnp.concatenate([t[u].astype(BF16), mb[u]], axis=0), mb[u], _NN) for u in un]
            t = [t[u] + tm[u][lo, :] for u in un]
            mb = [tm[u][hi, :].astype(BF16) for u in un]
        else:
            t = [t[u] + _dg(t[u].astype(BF16), mb[u], _NN) for u in un]
    xb = [_dg(t[u].astype(BF16),
              jnp.concatenate([a_t[j][:, sl], u0[u].astype(BF16)], axis=1), _NN).astype(BF16)
          for u, (j, sl) in enumerate(units)]

    s = [s_ref[u] for u in un]
    sb = [s[u].astype(BF16) for u in un]
    lx = [_dg(l_rb[u], xb[u], _NN) for u in un]
    lv = [_dg(l_rk[u], vh[u], _NN) for u in un]
    r2 = [(r_f[j][:, sl] + lx[u][:, lo]).astype(BF16) for u, (j, sl) in enumerate(units)]
    y = [_dg(r2[u], sb[u], _NT) + lx[u][:, hi] + lv[u] for u in un]

    xq = [_dg(xb[u], b_h[j][:, sl], _TN) for u, (j, sl) in enumerate(units)]
    vk = [_dg(vh[u], k_h[j][:, sl], _TN) for u, (j, sl) in enumerate(units)]
    sq = [_dg(sb[u], xq[u][lo, :].astype(BF16), _NN) for u in un]
    for u, (j, sl) in enumerate(units):
        s_ref[u] = s[u] * pend[j][:, sl] + sq[u] + xq[u][hi, :] + vk[u]

    bonus = [bonus_ref[j, at, :].astype(F32) for j in range(rows)]
    gate = [gate_ref[j, at, :].astype(F32) for j in range(rows)]
    for u, (j, sl) in enumerate(units):
        mean = jnp.mean(y[u], axis=-1, keepdims=True)
        yc = y[u] - mean
        var = jnp.mean(yc * yc, axis=-1, keepdims=True)
        yn = yc * lax.rsqrt(var + GN_EPS) * ln_ref[0:1, sl] + ln_ref[1:2, sl] + bonus[j][:, sl]
        y_o[j, at, sl] = (yn * gate[j][:, sl]).astype(y_o.dtype)


def _attn_prep_kernel(x_ref, g_ref, w_ref, qk_ref, ones_ref, q_o, k_o, v_o, gate_o, *, pad_tiles):
    i = pl.program_id(1)

    @pl.when(i < pad_tiles)
    def _():
        k_o[...] = jnp.zeros_like(k_o)
        v_o[...] = jnp.zeros_like(v_o)

    @pl.when(i >= pad_tiles)
    def _():
        zb = _norm_proj(x_ref[...], g_ref, w_ref)
        ones_bd = ones_ref[...]
        q = zb[:, 0:WIDTH]
        k = zb[:, WIDTH:2 * WIDTH]
        gate = zb[:, 3 * WIDTH:4 * WIDTH]
        qn = q * lax.rsqrt(_head_sum(q * q, ones_bd) * (1.0 / HEAD_DIM) + RMS_EPS) * qk_ref[0:1, :]
        kn = k * lax.rsqrt(_head_sum(k * k, ones_bd) * (1.0 / HEAD_DIM) + RMS_EPS) * qk_ref[1:2, :]
        q_o[...] = (qn * HEAD_DIM ** -0.5).astype(q_o.dtype)
        k_o[...] = kn.astype(k_o.dtype)
        v_o[...] = zb[:, 2 * WIDTH:3 * WIDTH].astype(v_o.dtype)
        gate_o[...] = (gate * _sigmoid(gate)).astype(gate_o.dtype)


def _attn_band_kernel(q_ref, *refs):
    n_chunks = q_ref.shape[0] // CHUNK
    first = pl.program_id(1) * n_chunks
    _chunk_loop(n_chunks, ATTN_Q_CHUNKS, functools.partial(_attn_group, first, q_ref, *refs))


def _attn_group(first, q_ref, k_ref, v_ref, bias_ref, gate_ref, y_o, off):
    nq = ATTN_Q_CHUNKS * CHUNK
    nk = (ATTN_Q_CHUNKS + PAST_CHUNKS) * CHUNK
    at = pl.ds(off, nq)
    n = first + off // CHUNK
    start = pl.multiple_of(first * CHUNK + off, CHUNK)
    q = q_ref[at, :]
    kb = k_ref[pl.ds(start, nk), :]
    vb = v_ref[pl.ds(start, nk), :]
    qc = lax.broadcasted_iota(jnp.int32, (nq, nk), 0) // CHUNK
    kc = lax.broadcasted_iota(jnp.int32, (nq, nk), 1) // CHUNK
    valid = (kc >= qc) & (kc <= qc + PAST_CHUNKS) & (kc >= PAST_CHUNKS - n)
    gate = gate_ref[at, :].astype(F32)
    for h0 in range(0, HEADS, ATTN_HEAD_GROUP):
        heads = range(h0, h0 + ATTN_HEAD_GROUP)
        sls = {h: slice(h * HEAD_DIM, (h + 1) * HEAD_DIM) for h in heads}
        s = {h: _dg(q[:, sls[h]], kb[:, sls[h]], _NT) for h in heads}
        s = {h: jnp.where(valid, s[h] + bias_ref[h], -jnp.inf) for h in heads}
        p = {h: jnp.exp(s[h] - jnp.max(s[h], axis=-1, keepdims=True)) for h in heads}
        o = {h: _dg(p[h].astype(BF16), vb[:, sls[h]], _NN) for h in heads}
        for h in heads:
            oh = o[h] * (1.0 / jnp.sum(p[h], axis=-1, keepdims=True))
            y_o[at, sls[h]] = (oh * gate[:, sls[h]]).astype(y_o.dtype)


def _hgrn_prep_kernel(x_ref, g_ref, w_ref, lb_ref, q_o, k_o, lf_o, v_o, gate_o, *, layer):
    lbp = lb_ref[...]
    e = jnp.exp(lbp - jnp.max(lbp, axis=0, keepdims=True))
    sm = e / jnp.sum(e, axis=0, keepdims=True)
    lb = jnp.sum(sm[0:layer + 1], axis=0, keepdims=True) - sm[0:1]
    zc = _norm_proj(x_ref[...], g_ref, w_ref)
    q = zc[:, 0:WIDTH]
    gate = zc[:, 3 * WIDTH:4 * WIDTH]
    fg = lb + (1.0 - lb) * _sigmoid(zc[:, WIDTH:2 * WIDTH])
    q_o[...] = (q * _sigmoid(q)).astype(q_o.dtype)
    k_o[...] = (1.0 - fg).astype(k_o.dtype)
    lf_o[...] = jnp.log(fg)
    v_o[...] = zc[:, 2 * WIDTH:3 * WIDTH].astype(v_o.dtype)
    gate_o[...] = (gate * _sigmoid(gate)).astype(gate_o.dtype)


def _hgrn_scan_kernel(q_ref, *refs):
    s_ref = refs[-1]

    @pl.when(pl.program_id(1) == 0)
    def _():
        s_ref[...] = jnp.zeros_like(s_ref)

    _chunk_loop(q_ref.shape[1] // CHUNK, 1, functools.partial(_hgrn_chunk, q_ref, *refs))


def _hgrn_chunk(q_ref, k_ref, lf_ref, v_ref, gate_ref, sel_ref, ones_ref, ng_ref, y_o, s_ref, off):
    rows = q_ref.shape[0]
    at = pl.ds(off, CHUNK)
    n_lv = len(LEVELS)
    row = lax.broadcasted_iota(jnp.int32, (CHUNK, 1), 0)
    ri = lax.broadcasted_iota(jnp.int32, (CHUNK, CHUNK), 0)
    ci = lax.broadcasted_iota(jnp.int32, (CHUNK, CHUNK), 1)
    second = [((row // w) % 2) == 1 for w in LEVELS]
    same = [(ri // (2 * w)) == (ci // (2 * w)) for w in LEVELS]

    q_lv, k_lv, qk_diag, q_in, k_out, dec, v_f, v_b = ([] for _ in range(8))
    for j in range(rows):
        q, k, v_b16 = q_ref[j, at, :].astype(F32), k_ref[j, at, :].astype(F32), v_ref[j, at, :]
        b = _cumsum_rows(lf_ref[j, at, :])
        bend = b[CHUNK - 1:CHUNK, :]
        b1 = b.astype(BF16)
        rem = b - b1.astype(F32)
        b2 = rem.astype(BF16)
        b3 = (rem - b2.astype(F32)).astype(BF16)
        edges = _dg(sel_ref[...], jnp.concatenate([b1, b2, b3], axis=0), _NN)
        ql, kl = [], []
        for i in range(n_lv):
            before = edges[i * CHUNK:(i + 1) * CHUNK, :]
            last = edges[(n_lv + i) * CHUNK:(n_lv + i + 1) * CHUNK, :]
            e = jnp.exp(jnp.where(second[i], b - before, last - b))
            ql.append(jnp.where(second[i], q * e, 0.0).astype(BF16))
            kl.append(jnp.where(second[i], 0.0, k * e).astype(BF16))
        q_lv.append(ql)
        k_lv.append(kl)
        qk_diag.append(_head_sum(q * k, ones_ref[...]))
        q_in.append((q * jnp.exp(b)).astype(BF16))
        k_out.append((k * jnp.exp(bend - b)).astype(BF16))
        dec.append(jnp.exp(bend))
        v_f.append(v_b16.astype(F32))
        v_b.append(v_b16)

    units = [(j, slice(h * HEAD_DIM, (h + 1) * HEAD_DIM)) for j in range(rows) for h in range(HEADS)]
    un = range(len(units))
    att = [jnp.zeros((CHUNK, CHUNK), F32) for _ in un]
    for i in range(n_lv):
        pr = [_dg(q_lv[j][i][:, sl], k_lv[j][i][:, sl], _NT) for j, sl in units]
        att = [att[u] + jnp.where(same[i], pr[u], 0.0) for u in un]
    s = [s_ref[u] for u in un]
    o_in = [_dg(q_in[j][:, sl], s[u].astype(BF16), _NT) for u, (j, sl) in enumerate(units)]
    o_at = [_dg(att[u].astype(BF16), v_b[j][:, sl], _NN) for u, (j, sl) in enumerate(units)]
    vk = [_dg(v_b[j][:, sl], k_out[j][:, sl], _TN) for j, sl in units]
    for u, (j, sl) in enumerate(units):
        s_ref[u] = s[u] * dec[j][:, sl] + vk[u]
    gate = [gate_ref[j, at, :].astype(F32) for j in range(rows)]
    for u, (j, sl) in enumerate(units):
        o = o_at[u] + o_in[u] + qk_diag[j][:, sl] * v_f[j][:, sl]
        ms = jnp.mean(o * o, axis=-1, keepdims=True)
        o = o * lax.rsqrt(ms + RMS_EPS) * ng_ref[:, sl]
        y_o[j, at, sl] = (o * gate[j][:, sl]).astype(y_o.dtype)


def _merge_kernel(x_ref, g_ref, wg_ref, ya_ref, yb_ref, yc_ref, pa_ref, pb_ref, pc_ref, wo_ref, o_ref):
    gates = _sigmoid(_norm_proj(x_ref[...], g_ref, wg_ref))
    m = (gates[:, 0:D_MODEL] * jnp.dot(ya_ref[...], pa_ref[...], preferred_element_type=F32)
         + gates[:, D_MODEL:2 * D_MODEL] * jnp.dot(yb_ref[...], pb_ref[...], preferred_element_type=F32)
         + gates[:, 2 * D_MODEL:3 * D_MODEL] * jnp.dot(yc_ref[...], pc_ref[...], preferred_element_type=F32))
    o_ref[...] = x_ref[...] + jnp.dot(m.astype(BF16), wo_ref[...], preferred_element_type=F32)


def _params(*sem):
    return pltpu.CompilerParams(dimension_semantics=sem, vmem_limit_bytes=VMEM_LIMIT)


def _full(shape):
    return pl.BlockSpec(shape, lambda *_: (0,) * len(shape))


def _edge_selectors():
    n_lv = len(LEVELS)
    sel = np.zeros((2 * n_lv * CHUNK, CHUNK), np.float32)
    for i, w in enumerate(LEVELS):
        for t in range(CHUNK):
            first = (t // w) * w
            if first > 0:
                sel[i * CHUNK + t, first - 1] = 1.0
            sel[(n_lv + i) * CHUNK + t, first + w - 1] = 1.0
    return jnp.asarray(np.concatenate([sel, sel, sel], axis=1), BF16)


def _band_bias(rel_bias):
    nq = ATTN_Q_CHUNKS * CHUNK
    nk = (ATTN_Q_CHUNKS + PAST_CHUNKS) * CHUNK
    period = nk + nq
    j = np.arange(period)
    far = REL_CLIP + CHUNK - 1
    w = rel_bias.astype(F32)[:, np.where(j < nk, np.clip(PAD_ROWS - j, -(CHUNK - 1), REL_CLIP) + CHUNK - 1, far)]
    heads = w.shape[0]
    return jnp.tile(w, (1, nq))[:, :nq * (period - 1)].reshape(heads, nq, period - 1)[:, :, :nk]


def _layer(x, l, p, consts):
    bsz, seq, _ = x.shape
    tm = min(TOKEN_TILE, seq)
    n_t = seq // tm
    n_c = seq // CHUNK
    ones_bd, sel = consts
    g = p['norm_g'][l][None, :]

    tok = lambda w: pl.BlockSpec((None, tm, w), lambda b, t: (b, t, 0))
    rows = min(SCAN_ROWS, bsz)
    blk = min(SCAN_CHUNKS, n_c) * CHUNK
    chk = lambda w: pl.BlockSpec((rows, blk, w), lambda b, c: (b, c, 0))
    qck = lambda w: pl.BlockSpec((None, blk, w), lambda b, c: (b, c, 0))
    act = lambda dt: jax.ShapeDtypeStruct((bsz, seq, WIDTH), dt)

    w_in = p['w_in'][l]
    wa = w_in[:, :A_COLS]
    cols = np.r_[0:WIDTH, WIDTH + LORA:2 * WIDTH + LORA, 2 * WIDTH + LORA:3 * WIDTH + LORA,
                 3 * WIDTH + 2 * LORA:4 * WIDTH + 2 * LORA, WIDTH:WIDTH + LORA,
                 3 * WIDTH + LORA:3 * WIDTH + 2 * LORA]
    pa = jnp.stack([p['rwkv_w0'][l], p['rwkv_a0'][l], p['rwkv_k_k'][l], p['rwkv_k_a'][l],
                    p['rwkv_r_k'][l].reshape(WIDTH)] + [jnp.zeros((WIDTH,), F32)] * 3)
    r, lw, k2, v, kk, bv, bonus, gate_a = pl.pallas_call(
        _rwkv_prep_kernel,
        grid=(bsz, n_t),
        in_specs=[tok(D_MODEL), _full((1, D_MODEL)), _full((D_MODEL, A_COLS)), _full((1, A_COLS)),
                  _full((8, WIDTH)), _full((LORA, WIDTH)), _full((LORA, WIDTH)), _full((WIDTH, WIDTH))],
        out_specs=[tok(WIDTH)] * 8,
        out_shape=[act(BF16), act(F32)] + [act(BF16)] * 6,
        scratch_shapes=[pltpu.VMEM((8, A_COLS), F32)],
        compiler_params=_params("parallel", "arbitrary"),
        name="rwkv_prep",
    )(x, g, wa[:, cols].astype(BF16), p['rwkv_mu'][l][cols][None, :], pa,
      p['rwkv_w2'][l], p['rwkv_a2'][l], ones_bd)

    ln = jnp.stack([p['rwkv_ln_w'][l], p['rwkv_ln_b'][l]] + [jnp.zeros((WIDTH,), F32)] * 6)
    y_a = pl.pallas_call(
        _rwkv_scan_kernel,
        grid=(bsz // rows, seq // blk),
        in_specs=[chk(WIDTH)] * 8 + [_full((8, WIDTH))],
        out_specs=chk(WIDTH),
        out_shape=act(BF16),
        scratch_shapes=[pltpu.VMEM((rows * HEADS, HEAD_DIM, HEAD_DIM), F32)],
        compiler_params=_params("parallel", "arbitrary"),
        name="rwkv_scan",
    )(r, lw, k2, v, kk, bv, bonus, gate_a, ln)

    pad_tiles = PAD_ROWS // tm
    wb = w_in[:, A_COLS:A_COLS + 4 * WIDTH].astype(BF16)
    qk = jnp.stack([jnp.tile(p['attn_q_norm'][l], HEADS), jnp.tile(p['attn_k_norm'][l], HEADS)]
                   + [jnp.zeros((WIDTH,), F32)] * 6)
    shift = lambda b, i: (b, jnp.maximum(i - pad_tiles, 0), 0)
    q, k_pad, v_pad, gate_b = pl.pallas_call(
        functools.partial(_attn_prep_kernel, pad_tiles=pad_tiles),
        grid=(bsz, n_t + pad_tiles),
        in_specs=[pl.BlockSpec((None, tm, D_MODEL), shift), _full((1, D_MODEL)),
                  _full((D_MODEL, 4 * WIDTH)), _full((8, WIDTH)), _full((WIDTH, WIDTH))],
        out_specs=[pl.BlockSpec((None, tm, WIDTH), shift), tok(WIDTH), tok(WIDTH),
                   pl.BlockSpec((None, tm, WIDTH), shift)],
        out_shape=[act(BF16), jax.ShapeDtypeStruct((bsz, seq + PAD_ROWS, WIDTH), BF16),
                   jax.ShapeDtypeStruct((bsz, seq + PAD_ROWS, WIDTH), BF16), act(BF16)],
        compiler_params=_params("parallel", "arbitrary"),
        name="attn_prep",
    )(x, g, wb, qk, ones_bd)

    band = pl.BlockSpec((None, seq + PAD_ROWS, WIDTH), lambda b, c: (b, 0, 0))
    y_b = pl.pallas_call(
        _attn_band_kernel,
        grid=(bsz, seq // blk),
        in_specs=[qck(WIDTH), band, band,
                  _full((HEADS, ATTN_Q_CHUNKS * CHUNK, (ATTN_Q_CHUNKS + PAST_CHUNKS) * CHUNK)), qck(WIDTH)],
        out_specs=qck(WIDTH),
        out_shape=act(BF16),
        compiler_params=_params("parallel", "arbitrary"),
        name="attn_band",
    )(q, k_pad, v_pad, _band_bias(p['attn_rel_bias'][l]), gate_b)

    wc = w_in[:, A_COLS + 4 * WIDTH:A_COLS + 8 * WIDTH].astype(BF16)
    n_layers = p['hgrn_lb'].shape[0]
    qc, kc, lf, vc, gate_c = pl.pallas_call(
        functools.partial(_hgrn_prep_kernel, layer=l),
        grid=(bsz, n_t),
        in_specs=[tok(D_MODEL), _full((1, D_MODEL)), _full((D_MODEL, 4 * WIDTH)), _full((n_layers, WIDTH))],
        out_specs=[tok(WIDTH)] * 5,
        out_shape=[act(BF16), act(BF16), act(F32), act(BF16), act(BF16)],
        compiler_params=_params("parallel", "parallel"),
        name="hgrn_prep",
    )(x, g, wc, p['hgrn_lb'])

    y_c = pl.pallas_call(
        _hgrn_scan_kernel,
        grid=(bsz // rows, seq // blk),
        in_specs=[chk(WIDTH)] * 5 + [_full(sel.shape), _full((WIDTH, WIDTH)), _full((1, WIDTH))],
        out_specs=chk(WIDTH),
        out_shape=act(BF16),
        scratch_shapes=[pltpu.VMEM((rows * HEADS, HEAD_DIM, HEAD_DIM), F32)],
        compiler_params=_params("parallel", "arbitrary"),
        name="hgrn_scan",
    )(qc, kc, lf, vc, gate_c, sel, ones_bd, jnp.tile(p['hgrn_norm'][l], HEADS)[None, :])

    wg = w_in[:, A_COLS + 8 * WIDTH:].astype(BF16)
    return pl.pallas_call(
        _merge_kernel,
        grid=(bsz, n_t),
        in_specs=[tok(D_MODEL), _full((1, D_MODEL)), _full((D_MODEL, 3 * D_MODEL)),
                  tok(WIDTH), tok(WIDTH), tok(WIDTH),
                  _full((WIDTH, D_MODEL)), _full((WIDTH, D_MODEL)), _full((WIDTH, D_MODEL)),
                  _full((D_MODEL, D_MODEL))],
        out_specs=tok(D_MODEL),
        out_shape=jax.ShapeDtypeStruct(x.shape, x.dtype),
        compiler_params=_params("parallel", "parallel"),
        name="merge",
    )(x, g, wg, y_a, y_b, y_c, p['proj_a'][l].astype(BF16), p['proj_b'][l].astype(BF16),
      p['proj_c'][l].astype(BF16), p['w_out'][l].astype(BF16))


def kernel(x, norm_g, w_in, rwkv_mu, rwkv_w0, rwkv_w2, rwkv_a0, rwkv_a2, rwkv_k_k, rwkv_k_a, rwkv_r_k,
           rwkv_ln_w, rwkv_ln_b, attn_q_norm, attn_k_norm, attn_rel_bias, hgrn_lb, hgrn_norm,
           proj_a, proj_b, proj_c, w_out):
    p = dict(norm_g=norm_g, w_in=w_in, rwkv_mu=rwkv_mu, rwkv_w0=rwkv_w0, rwkv_w2=rwkv_w2,
             rwkv_a0=rwkv_a0, rwkv_a2=rwkv_a2, rwkv_k_k=rwkv_k_k, rwkv_k_a=rwkv_k_a, rwkv_r_k=rwkv_r_k,
             rwkv_ln_w=rwkv_ln_w, rwkv_ln_b=rwkv_ln_b, attn_q_norm=attn_q_norm, attn_k_norm=attn_k_norm,
             attn_rel_bias=attn_rel_bias, hgrn_lb=hgrn_lb, hgrn_norm=hgrn_norm,
             proj_a=proj_a, proj_b=proj_b, proj_c=proj_c, w_out=w_out)
    head = np.arange(WIDTH) // HEAD_DIM
    ones_bd = jnp.asarray(head[:, None] == head[None, :], BF16)
    consts = (ones_bd, _edge_selectors())
    for l in range(norm_g.shape[0]):
        x = _layer(x, l, p, consts)
    return x
```

```python
import functools

import numpy as np
import jax
import jax.numpy as jnp
from jax import lax
from jax.experimental import pallas as pl
from jax.experimental.pallas import tpu as pltpu

F32 = jnp.float32
BF16 = jnp.bfloat16

D_MODEL = 1024
CHUNK = 64
HEAD_DIM = 64
HEADS = 8
WIDTH = HEADS * HEAD_DIM
LORA = 64
A_COLS = 4 * WIDTH + 2 * LORA
PAST_CHUNKS = 8
PAD_ROWS = PAST_CHUNKS * CHUNK
REL_CLIP = 256
RMS_EPS = 1e-6
GN_EPS = 64e-5
LEVELS = (32, 16, 8, 4, 2, 1)

TOKEN_TILE = 512
PREP_SUB_ROWS = 256
SCAN_ROWS = 2
SCAN_CHUNKS = 8
ATTN_Q_CHUNKS = 2
VMEM_LIMIT = 56 * 1024 * 1024

_NN = (((1,), (0,)), ((), ()))
_NT = (((1,), (1,)), ((), ()))
_TN = (((0,), (0,)), ((), ()))


def _dg(a, b, dims):
    return lax.dot_general(a, b, dims, preferred_element_type=F32)


def _hi_lo(a):
    hi = a.astype(BF16)
    lo = (a - hi.astype(F32)).astype(BF16)
    return hi, lo


def _mm3(a, b, dims=_NN):
    a1, a2 = _hi_lo(a)
    b1, b2 = _hi_lo(b)
    return _dg(a1, b1, dims) + (_dg(a1, b2, dims) + _dg(a2, b1, dims))


def _head_sum(a, ones_bd):
    return _dg(a.astype(BF16), ones_bd, _NN)


def _sigmoid(x):
    return 0.5 * jnp.tanh(0.5 * x) + 0.5


def _cumsum_rows(x):
    n = x.shape[0]
    rows = lax.broadcasted_iota(jnp.int32, x.shape, 0)
    s = 1
    while s < n:
        x = x + jnp.where(rows >= s, pltpu.roll(x, s, 0), 0.0)
        s *= 2
    return x


def _chunk_loop(n_chunks, step, body):
    def it(i, carry):
        body(pl.multiple_of(i * (step * CHUNK), step * CHUNK))
        return carry
    lax.fori_loop(0, n_chunks // step, it, 0)


def _sub_blocks(n_rows):
    sub = min(PREP_SUB_ROWS, n_rows)
    return [slice(i * sub, (i + 1) * sub) for i in range(n_rows // sub)]


def _norm_proj(x, g_ref, w_ref):
    ms = jnp.mean(x * x, axis=-1, keepdims=True)
    h = x * lax.rsqrt(ms + RMS_EPS) * g_ref[...]
    return jnp.dot(h.astype(BF16), w_ref[...], preferred_element_type=F32)


def _rwkv_prep_kernel(x_ref, g_ref, w_ref, mu_ref, pa_ref, w2_ref, a2_ref, ones_ref,
                      r_o, lw_o, k_o, v_o, kk_o, b_o, bonus_o, gate_o, carry_ref):
    @pl.when(pl.program_id(1) == 0)
    def _():
        carry_ref[...] = jnp.zeros_like(carry_ref)

    w0, a0, k_k, k_a, r_k = (pa_ref[i:i + 1, :] for i in range(5))
    ones_bd = ones_ref[...]
    last = carry_ref[7:8, :]
    for rs in _sub_blocks(x_ref.shape[0]):
        za = _norm_proj(x_ref[rs, :], g_ref, w_ref)
        sub = za.shape[0]
        row = lax.broadcasted_iota(jnp.int32, (sub, 1), 0)
        prev = jnp.where(row == 0, last, pltpu.roll(za, 1, 0))
        last = za[sub - 1:sub, :]
        tail = za[sub - 8:, :]
        xs = za + (prev - za) * mu_ref[...]

        r = xs[:, 0:WIDTH]
        k = xs[:, WIDTH:2 * WIDTH]
        v = xs[:, 2 * WIDTH:3 * WIDTH]
        gate = xs[:, 3 * WIDTH:4 * WIDTH]
        wl = xs[:, 4 * WIDTH:4 * WIDTH + LORA]
        al = xs[:, 4 * WIDTH + LORA:4 * WIDTH + 2 * LORA]

        u = -(w0 + _mm3(jnp.tanh(wl), w2_ref[...]))
        softplus = jnp.maximum(u, 0.0) + jnp.log(1.0 + jnp.exp(-jnp.abs(u)))
        lw = -jnp.exp(-softplus - 0.5)
        a = _sigmoid(a0 + _mm3(al, a2_ref[...]))

        kkp = k * k_k
        kk = kkp * lax.rsqrt(jnp.maximum(_head_sum(kkp * kkp, ones_bd), 1e-24))
        k2 = k * (1.0 + (a - 1.0) * k_a)

        r_o[rs, :] = r.astype(r_o.dtype)
        lw_o[rs, :] = lw
        k_o[rs, :] = k2.astype(k_o.dtype)
        v_o[rs, :] = v.astype(v_o.dtype)
        kk_o[rs, :] = kk.astype(kk_o.dtype)
        b_o[rs, :] = (kk * a).astype(b_o.dtype)
        bonus_o[rs, :] = (_head_sum(r * k2 * r_k, ones_bd) * v).astype(bonus_o.dtype)
        gate_o[rs, :] = (gate * _sigmoid(gate)).astype(gate_o.dtype)
    carry_ref[...] = tail


def _rwkv_scan_kernel(r_ref, *refs):
    s_ref = refs[-1]

    @pl.when(pl.program_id(1) == 0)
    def _():
        s_ref[...] = jnp.zeros_like(s_ref)

    _chunk_loop(r_ref.shape[1] // CHUNK, 1, functools.partial(_rwkv_chunk, r_ref, *refs))


def _rwkv_chunk(r_ref, lw_ref, k_ref, v_ref, kk_ref, b_ref, bonus_ref, gate_ref, ln_ref,
                y_o, s_ref, off):
    rows = r_ref.shape[0]
    at = pl.ds(off, CHUNK)
    a_t, b_t, k_t, r_f, r_t, b_h, k_h, v_b, pend = ([] for _ in range(9))
    for j in range(rows):
        lw = lw_ref[j, at, :]
        cum = _cumsum_rows(lw)
        cend = cum[CHUNK - 1:CHUNK, :]
        pinv = jnp.exp(-cum)
        dend = jnp.exp(cend - cum)
        kk, bv, k2 = kk_ref[j, at, :].astype(F32), b_ref[j, at, :].astype(F32), k_ref[j, at, :].astype(F32)
        a_t.append((-kk * jnp.exp(cum - lw)).astype(BF16))
        b_t.append((bv * pinv).astype(BF16))
        k_t.append((k2 * pinv).astype(BF16))
        r_f.append(r_ref[j, at, :].astype(F32) * jnp.exp(cum))
        r_t.append(r_f[j].astype(BF16))
        b_h.append((bv * dend).astype(BF16))
        k_h.append((k2 * dend).astype(BF16))
        v_b.append(v_ref[j, at, :])
        pend.append(jnp.exp(cend))

    ri = lax.broadcasted_iota(jnp.int32, (CHUNK, CHUNK), 0)
    ci = lax.broadcasted_iota(jnp.int32, (CHUNK, CHUNK), 1)
    strict = ri > ci
    incl = ri >= ci
    units = [(j, slice(h * HEAD_DIM, (h + 1) * HEAD_DIM)) for j in range(rows) for h in range(HEADS)]
    un = range(len(units))
    lo, hi = slice(0, HEAD_DIM), slice(HEAD_DIM, 2 * HEAD_DIM)

    sc = [_dg(jnp.concatenate([a_t[j][:, sl], r_t[j][:, sl]], axis=0),
              jnp.concatenate([b_t[j][:, sl], k_t[j][:, sl]], axis=0), _NT) for j, sl in units]
    m = [jnp.where(strict, sc[u][lo, lo], 0.0) for u in un]
    m_ak = [jnp.where(strict, sc[u][lo, hi], 0.0).astype(BF16) for u in un]
    l_rb = [jnp.where(incl, sc[u][hi, lo], 0.0).astype(BF16) for u in un]
    l_rk = [jnp.where(incl, sc[u][hi, hi], 0.0).astype(BF16) for u in un]
    vh = [v_b[j][:, sl] for j, sl in units]

    u0 = [_dg(m_ak[u], vh[u], _NN) for u in un]
    t = [jnp.where(ri == ci, 1.0, m[u]) for u in un]
    mb = [m[u].astype(BF16) for u in un]
    mb = [_dg(mb[u], mb[u], _NN).astype(BF16) for u in un]
    for i in range(1, 6):
        if i < 5:
            tm = [_dg(jnp.concatenate([t[u].astype(BF16), mb[u]], axis=0), mb[u], _NN) for u in un]
            t = [t[u] + tm[u][lo, :] for u in un]
            mb = [tm[u][hi, :].astype(BF16) for u in un]
        else:
            t = [t[u] + _dg(t[u].astype(BF16), mb[u], _NN) for u in un]
    xb = [_dg(t[u].astype(BF16),
              jnp.concatenate([a_t[j][:, sl], u0[u].astype(BF16)], axis=1), _NN).astype(BF16)
          for u, (j, sl) in enumerate(units)]

    s = [s_ref[u] for u in un]
    sb = [s[u].astype(BF16) for u in un]
    lx = [_dg(l_rb[u], xb[u], _NN) for u in un]
    lv = [_dg(l_rk[u], vh[u], _NN) for u in un]
    r2 = [(r_f[j][:, sl] + lx[u][:, lo]).astype(BF16) for u, (j, sl) in enumerate(units)]
    y = [_dg(r2[u], sb[u], _NT) + lx[u][:, hi] + lv[u] for u in un]

    xq = [_dg(xb[u], b_h[j][:, sl], _TN) for u, (j, sl) in enumerate(units)]
    vk = [_dg(vh[u], k_h[j][:, sl], _TN) for u, (j, sl) in enumerate(units)]
    sq = [_dg(sb[u], xq[u][lo, :].astype(BF16), _NN) for u in un]
    for u, (j, sl) in enumerate(units):
        s_ref[u] = s[u] * pend[j][:, sl] + sq[u] + xq[u][hi, :] + vk[u]

    bonus = [bonus_ref[j, at, :].astype(F32) for j in range(rows)]
    gate = [gate_ref[j, at, :].astype(F32) for j in range(rows)]
    for u, (j, sl) in enumerate(units):
        mean = jnp.mean(y[u], axis=-1, keepdims=True)
        yc = y[u] - mean
        var = jnp.mean(yc * yc, axis=-1, keepdims=True)
        yn = yc * lax.rsqrt(var + GN_EPS) * ln_ref[0:1, sl] + ln_ref[1:2, sl] + bonus[j][:, sl]
        y_o[j, at, sl] = (yn * gate[j][:, sl]).astype(y_o.dtype)


def _attn_prep_kernel(x_ref, g_ref, w_ref, qk_ref, ones_ref, q_o, k_o, v_o, gate_o, *, pad_tiles):
    i = pl.program_id(1)

    @pl.when(i < pad_tiles)
    def _():
        k_o[...] = jnp.zeros_like(k_o)
        v_o[...] = jnp.zeros_like(v_o)

    @pl.when(i >= pad_tiles)
    def _():
        zb = _norm_proj(x_ref[...], g_ref, w_ref)
        ones_bd = ones_ref[...]
        q = zb[:, 0:WIDTH]
        k = zb[:, WIDTH:2 * WIDTH]
        gate = zb[:, 3 * WIDTH:4 * WIDTH]
        qn = q * lax.rsqrt(_head_sum(q * q, ones_bd) * (1.0 / HEAD_DIM) + RMS_EPS) * qk_ref[0:1, :]
        kn = k * lax.rsqrt(_head_sum(k * k, ones_bd) * (1.0 / HEAD_DIM) + RMS_EPS) * qk_ref[1:2, :]
        q_o[...] = (qn * HEAD_DIM ** -0.5).astype(q_o.dtype)
        k_o[...] = kn.astype(k_o.dtype)
        v_o[...] = zb[:, 2 * WIDTH:3 * WIDTH].astype(v_o.dtype)
        gate_o[...] = (gate * _sigmoid(gate)).astype(gate_o.dtype)


def _attn_band_kernel(q_ref, *refs):
    n_chunks = q_ref.shape[0] // CHUNK
    first = pl.program_id(1) * n_chunks
    _chunk_loop(n_chunks, ATTN_Q_CHUNKS, functools.partial(_attn_group, first, q_ref, *refs))


def _attn_group(first, q_ref, k_ref, v_ref, bias_ref, gate_ref, y_o, off):
    nq = ATTN_Q_CHUNKS * CHUNK
    nk = (ATTN_Q_CHUNKS + PAST_CHUNKS) * CHUNK
    at = pl.ds(off, nq)
    n = first + off // CHUNK
    start = pl.multiple_of(first * CHUNK + off, CHUNK)
    q = q_ref[at, :]
    kb = k_ref[pl.ds(start, nk), :]
    vb = v_ref[pl.ds(start, nk), :]
    qc = lax.broadcasted_iota(jnp.int32, (nq, nk), 0) // CHUNK
    kc = lax.broadcasted_iota(jnp.int32, (nq, nk), 1) // CHUNK
    valid = (kc >= qc) & (kc <= qc + PAST_CHUNKS) & (kc >= PAST_CHUNKS - n)
    gate = gate_ref[at, :].astype(F32)
    low_k = lax.broadcasted_iota(jnp.int32, (nk, 2 * HEAD_DIM), 1) < HEAD_DIM
    low_q = lax.broadcasted_iota(jnp.int32, (nq, 2 * HEAD_DIM), 1) < HEAD_DIM
    zero = jnp.zeros((nk, 2 * HEAD_DIM), BF16)
    for pair in range(HEADS // 2):
        lanes = slice(pair * 2 * HEAD_DIM, (pair + 1) * 2 * HEAD_DIM)
        k2, v2 = kb[:, lanes], vb[:, lanes]
        k_bd = jnp.concatenate([jnp.where(low_k, k2, zero), jnp.where(low_k, zero, k2)], axis=0)
        v_bd = jnp.concatenate([jnp.where(low_k, v2, zero), jnp.where(low_k, zero, v2)], axis=0)
        s2 = _dg(q[:, lanes], k_bd, _NT)
        p, inv = [], []
        for i in range(2):
            s = jnp.where(valid, s2[:, i * nk:(i + 1) * nk] + bias_ref[2 * pair + i], -jnp.inf)
            e = jnp.exp(s - jnp.max(s, axis=-1, keepdims=True))
            p.append(e.astype(BF16))
            inv.append(1.0 / jnp.sum(e, axis=-1, keepdims=True))
        o = _dg(jnp.concatenate(p, axis=1), v_bd, _NN)
        y_o[at, lanes] = (o * jnp.where(low_q, inv[0], inv[1]) * gate[:, lanes]).astype(y_o.dtype)


def _hgrn_prep_kernel(x_ref, g_ref, w_ref, lb_ref, q_o, k_o, lf_o, v_o, gate_o, *, layer):
    lbp = lb_ref[...]
    e = jnp.exp(lbp - jnp.max(lbp, axis=0, keepdims=True))
    sm = e / jnp.sum(e, axis=0, keepdims=True)
    lb = jnp.sum(sm[0:layer + 1], axis=0, keepdims=True) - sm[0:1]
    zc = _norm_proj(x_ref[...], g_ref, w_ref)
    q = zc[:, 0:WIDTH]
    gate = zc[:, 3 * WIDTH:4 * WIDTH]
    fg = lb + (1.0 - lb) * _sigmoid(zc[:, WIDTH:2 * WIDTH])
    q_o[...] = (q * _sigmoid(q)).astype(q_o.dtype)
    k_o[...] = (1.0 - fg).astype(k_o.dtype)
    lf_o[...] = jnp.log(fg)
    v_o[...] = zc[:, 2 * WIDTH:3 * WIDTH].astype(v_o.dtype)
    gate_o[...] = (gate * _sigmoid(gate)).astype(gate_o.dtype)


def _hgrn_scan_kernel(q_ref, *refs):
    s_ref = refs[-1]

    @pl.when(pl.program_id(1) == 0)
    def _():
        s_ref[...] = jnp.zeros_like(s_ref)

    _chunk_loop(q_ref.shape[1] // CHUNK, 1, functools.partial(_hgrn_chunk, q_ref, *refs))


def _hgrn_chunk(q_ref, k_ref, lf_ref, v_ref, gate_ref, sel_ref, ones_ref, ng_ref, y_o, s_ref, off):
    rows = q_ref.shape[0]
    at = pl.ds(off, CHUNK)
    n_lv = len(LEVELS)
    row = lax.broadcasted_iota(jnp.int32, (CHUNK, 1), 0)
    ri = lax.broadcasted_iota(jnp.int32, (CHUNK, CHUNK), 0)
    ci = lax.broadcasted_iota(jnp.int32, (CHUNK, CHUNK), 1)
    second = [((row // w) % 2) == 1 for w in LEVELS]
    same = [(ri // (2 * w)) == (ci // (2 * w)) for w in LEVELS]

    q_lv, k_lv, qk_diag, q_in, k_out, dec, v_f, v_b = ([] for _ in range(8))
    for j in range(rows):
        q, k, v_b16 = q_ref[j, at, :].astype(F32), k_ref[j, at, :].astype(F32), v_ref[j, at, :]
        b = _cumsum_rows(lf_ref[j, at, :])
        bend = b[CHUNK - 1:CHUNK, :]
        b1 = b.astype(BF16)
        rem = b - b1.astype(F32)
        b2 = rem.astype(BF16)
        b3 = (rem - b2.astype(F32)).astype(BF16)
        edges = _dg(sel_ref[...], jnp.concatenate([b1, b2, b3], axis=0), _NN)
        ql, kl = [], []
        for i in range(n_lv):
            edge = edges[i * CHUNK:(i + 1) * CHUNK, :]
            e = jnp.exp(jnp.where(second[i], b - edge, edge - b))
            ql.append(jnp.where(second[i], q * e, 0.0).astype(BF16))
            kl.append(jnp.where(second[i], 0.0, k * e).astype(BF16))
        q_lv.append(ql)
        k_lv.append(kl)
        qk_diag.append(_head_sum(q * k, ones_ref[...]))
        q_in.append((q * jnp.exp(b)).astype(BF16))
        k_out.append((k * jnp.exp(bend - b)).astype(BF16))
        dec.append(jnp.exp(bend))
        v_f.append(v_b16.astype(F32))
        v_b.append(v_b16)

    units = [(j, slice(h * HEAD_DIM, (h + 1) * HEAD_DIM)) for j in range(rows) for h in range(HEADS)]
    un = range(len(units))
    att = [jnp.zeros((CHUNK, CHUNK), F32) for _ in un]
    for i in range(n_lv):
        pr = [_dg(q_lv[j][i][:, sl], k_lv[j][i][:, sl], _NT) for j, sl in units]
        att = [att[u] + jnp.where(same[i], pr[u], 0.0) for u in un]
    s = [s_ref[u] for u in un]
    o_in = [_dg(q_in[j][:, sl], s[u].astype(BF16), _NT) for u, (j, sl) in enumerate(units)]
    o_at = [_dg(att[u].astype(BF16), v_b[j][:, sl], _NN) for u, (j, sl) in enumerate(units)]
    vk = [_dg(v_b[j][:, sl], k_out[j][:, sl], _TN) for j, sl in units]
    for u, (j, sl) in enumerate(units):
        s_ref[u] = s[u] * dec[j][:, sl] + vk[u]
    gate = [gate_ref[j, at, :].astype(F32) for j in range(rows)]
    for u, (j, sl) in enumerate(units):
        o = o_at[u] + o_in[u] + qk_diag[j][:, sl] * v_f[j][:, sl]
        ms = jnp.mean(o * o, axis=-1, keepdims=True)
        o = o * lax.rsqrt(ms + RMS_EPS) * ng_ref[:, sl]
        y_o[j, at, sl] = (o * gate[j][:, sl]).astype(y_o.dtype)


def _merge_kernel(x_ref, g_ref, wg_ref, ya_ref, yb_ref, yc_ref, pa_ref, pb_ref, pc_ref, wo_ref, o_ref):
    gates = _sigmoid(_norm_proj(x_ref[...], g_ref, wg_ref))
    m = (gates[:, 0:D_MODEL] * jnp.dot(ya_ref[...], pa_ref[...], preferred_element_type=F32)
         + gates[:, D_MODEL:2 * D_MODEL] * jnp.dot(yb_ref[...], pb_ref[...], preferred_element_type=F32)
         + gates[:, 2 * D_MODEL:3 * D_MODEL] * jnp.dot(yc_ref[...], pc_ref[...], preferred_element_type=F32))
    o_ref[...] = x_ref[...] + jnp.dot(m.astype(BF16), wo_ref[...], preferred_element_type=F32)


def _params(*sem):
    return pltpu.CompilerParams(dimension_semantics=sem, vmem_limit_bytes=VMEM_LIMIT)


def _full(shape):
    return pl.BlockSpec(shape, lambda *_: (0,) * len(shape))


def _edge_selectors():
    sel = np.zeros((len(LEVELS) * CHUNK, CHUNK), np.float32)
    for i, w in enumerate(LEVELS):
        for t in range(CHUNK):
            sel[i * CHUNK + t, (t // (2 * w)) * 2 * w + w - 1] = 1.0
    return jnp.asarray(np.concatenate([sel, sel, sel], axis=1), BF16)


def _band_bias(rel_bias):
    nq = ATTN_Q_CHUNKS * CHUNK
    nk = (ATTN_Q_CHUNKS + PAST_CHUNKS) * CHUNK
    period = nk + nq
    j = np.arange(period)
    far = REL_CLIP + CHUNK - 1
    w = rel_bias.astype(F32)[:, np.where(j < nk, np.clip(PAD_ROWS - j, -(CHUNK - 1), REL_CLIP) + CHUNK - 1, far)]
    heads = w.shape[0]
    return jnp.tile(w, (1, nq))[:, :nq * (period - 1)].reshape(heads, nq, period - 1)[:, :, :nk]


def _layer(x, l, p, consts):
    bsz, seq, _ = x.shape
    tm = min(TOKEN_TILE, seq)
    n_t = seq // tm
    n_c = seq // CHUNK
    ones_bd, sel = consts
    g = p['norm_g'][l][None, :]

    tok = lambda w: pl.BlockSpec((None, tm, w), lambda b, t: (b, t, 0))
    rows = min(SCAN_ROWS, bsz)
    blk = min(SCAN_CHUNKS, n_c) * CHUNK
    chk = lambda w: pl.BlockSpec((rows, blk, w), lambda b, c: (b, c, 0))
    qck = lambda w: pl.BlockSpec((None, blk, w), lambda b, c: (b, c, 0))
    act = lambda dt: jax.ShapeDtypeStruct((bsz, seq, WIDTH), dt)

    w_in = p['w_in'][l]
    wa = w_in[:, :A_COLS]
    cols = np.r_[0:WIDTH, WIDTH + LORA:2 * WIDTH + LORA, 2 * WIDTH + LORA:3 * WIDTH + LORA,
                 3 * WIDTH + 2 * LORA:4 * WIDTH + 2 * LORA, WIDTH:WIDTH + LORA,
                 3 * WIDTH + LORA:3 * WIDTH + 2 * LORA]
    pa = jnp.stack([p['rwkv_w0'][l], p['rwkv_a0'][l], p['rwkv_k_k'][l], p['rwkv_k_a'][l],
                    p['rwkv_r_k'][l].reshape(WIDTH)] + [jnp.zeros((WIDTH,), F32)] * 3)
    r, lw, k2, v, kk, bv, bonus, gate_a = pl.pallas_call(
        _rwkv_prep_kernel,
        grid=(bsz, n_t),
        in_specs=[tok(D_MODEL), _full((1, D_MODEL)), _full((D_MODEL, A_COLS)), _full((1, A_COLS)),
                  _full((8, WIDTH)), _full((LORA, WIDTH)), _full((LORA, WIDTH)), _full((WIDTH, WIDTH))],
        out_specs=[tok(WIDTH)] * 8,
        out_shape=[act(BF16), act(F32)] + [act(BF16)] * 6,
        scratch_shapes=[pltpu.VMEM((8, A_COLS), F32)],
        compiler_params=_params("parallel", "arbitrary"),
        name="rwkv_prep",
    )(x, g, wa[:, cols].astype(BF16), p['rwkv_mu'][l][cols][None, :], pa,
      p['rwkv_w2'][l], p['rwkv_a2'][l], ones_bd)

    ln = jnp.stack([p['rwkv_ln_w'][l], p['rwkv_ln_b'][l]] + [jnp.zeros((WIDTH,), F32)] * 6)
    y_a = pl.pallas_call(
        _rwkv_scan_kernel,
        grid=(bsz // rows, seq // blk),
        in_specs=[chk(WIDTH)] * 8 + [_full((8, WIDTH))],
        out_specs=chk(WIDTH),
        out_shape=act(BF16),
        scratch_shapes=[pltpu.VMEM((rows * HEADS, HEAD_DIM, HEAD_DIM), F32)],
        compiler_params=_params("parallel", "arbitrary"),
        name="rwkv_scan",
    )(r, lw, k2, v, kk, bv, bonus, gate_a, ln)

    pad_tiles = PAD_ROWS // tm
    wb = w_in[:, A_COLS:A_COLS + 4 * WIDTH].astype(BF16)
    qk = jnp.stack([jnp.tile(p['attn_q_norm'][l], HEADS), jnp.tile(p['attn_k_norm'][l], HEADS)]
                   + [jnp.zeros((WIDTH,), F32)] * 6)
    shift = lambda b, i: (b, jnp.maximum(i - pad_tiles, 0), 0)
    q, k_pad, v_pad, gate_b = pl.pallas_call(
        functools.partial(_attn_prep_kernel, pad_tiles=pad_tiles),
        grid=(bsz, n_t + pad_tiles),
        in_specs=[pl.BlockSpec((None, tm, D_MODEL), shift), _full((1, D_MODEL)),
                  _full((D_MODEL, 4 * WIDTH)), _full((8, WIDTH)), _full((WIDTH, WIDTH))],
        out_specs=[pl.BlockSpec((None, tm, WIDTH), shift), tok(WIDTH), tok(WIDTH),
                   pl.BlockSpec((None, tm, WIDTH), shift)],
        out_shape=[act(BF16), jax.ShapeDtypeStruct((bsz, seq + PAD_ROWS, WIDTH), BF16),
                   jax.ShapeDtypeStruct((bsz, seq + PAD_ROWS, WIDTH), BF16), act(BF16)],
        compiler_params=_params("parallel", "arbitrary"),
        name="attn_prep",
    )(x, g, wb, qk, ones_bd)

    band = pl.BlockSpec((None, seq + PAD_ROWS, WIDTH), lambda b, c: (b, 0, 0))
    bias = _band_bias(p['attn_rel_bias'][l])
    y_b = pl.pallas_call(
        _attn_band_kernel,
        grid=(bsz, seq // blk),
        in_specs=[qck(WIDTH), band, band,
                  _full(bias.shape), qck(WIDTH)],
        out_specs=qck(WIDTH),
        out_shape=act(BF16),
        compiler_params=_params("parallel", "arbitrary"),
        name="attn_band",
    )(q, k_pad, v_pad, bias, gate_b)

    wc = w_in[:, A_COLS + 4 * WIDTH:A_COLS + 8 * WIDTH].astype(BF16)
    n_layers = p['hgrn_lb'].shape[0]
    qc, kc, lf, vc, gate_c = pl.pallas_call(
        functools.partial(_hgrn_prep_kernel, layer=l),
        grid=(bsz, n_t),
        in_specs=[tok(D_MODEL), _full((1, D_MODEL)), _full((D_MODEL, 4 * WIDTH)), _full((n_layers, WIDTH))],
        out_specs=[tok(WIDTH)] * 5,
        out_shape=[act(BF16), act(BF16), act(F32), act(BF16), act(BF16)],
        compiler_params=_params("parallel", "parallel"),
        name="hgrn_prep",
    )(x, g, wc, p['hgrn_lb'])

    y_c = pl.pallas_call(
        _hgrn_scan_kernel,
        grid=(bsz // rows, seq // blk),
        in_specs=[chk(WIDTH)] * 5 + [_full(sel.shape), _full((WIDTH, WIDTH)), _full((1, WIDTH))],
        out_specs=chk(WIDTH),
        out_shape=act(BF16),
        scratch_shapes=[pltpu.VMEM((rows * HEADS, HEAD_DIM, HEAD_DIM), F32)],
        compiler_params=_params("parallel", "arbitrary"),
        name="hgrn_scan",
    )(qc, kc, lf, vc, gate_c, sel, ones_bd, jnp.tile(p['hgrn_norm'][l], HEADS)[None, :])

    wg = w_in[:, A_COLS + 8 * WIDTH:].astype(BF16)
    return pl.pallas_call(
        _merge_kernel,
        grid=(bsz, n_t),
        in_specs=[tok(D_MODEL), _full((1, D_MODEL)), _full((D_MODEL, 3 * D_MODEL)),
                  tok(WIDTH), tok(WIDTH), tok(WIDTH),
                  _full((WIDTH, D_MODEL)), _full((WIDTH, D_MODEL)), _full((WIDTH, D_MODEL)),
                  _full((D_MODEL, D_MODEL))],
        out_specs=tok(D_MODEL),
        out_shape=jax.ShapeDtypeStruct(x.shape, x.dtype),
        compiler_params=_params("parallel", "parallel"),
        name="merge",
    )(x, g, wg, y_a, y_b, y_c, p['proj_a'][l].astype(BF16), p['proj_b'][l].astype(BF16),
      p['proj_c'][l].astype(BF16), p['w_out'][l].astype(BF16))


def kernel(x, norm_g, w_in, rwkv_mu, rwkv_w0, rwkv_w2, rwkv_a0, rwkv_a2, rwkv_k_k, rwkv_k_a, rwkv_r_k,
           rwkv_ln_w, rwkv_ln_b, attn_q_norm, attn_k_norm, attn_rel_bias, hgrn_lb, hgrn_norm,
           proj_a, proj_b, proj_c, w_out):
    p = dict(norm_g=norm_g, w_in=w_in, rwkv_mu=rwkv_mu, rwkv_w0=rwkv_w0, rwkv_w2=rwkv_w2,
             rwkv_a0=rwkv_a0, rwkv_a2=rwkv_a2, rwkv_k_k=rwkv_k_k, rwkv_k_a=rwkv_k_a, rwkv_r_k=rwkv_r_k,
             rwkv_ln_w=rwkv_ln_w, rwkv_ln_b=rwkv_ln_b, attn_q_norm=attn_q_norm, attn_k_norm=attn_k_norm,
             attn_rel_bias=attn_rel_bias, hgrn_lb=hgrn_lb, hgrn_norm=hgrn_norm,
             proj_a=proj_a, proj_b=proj_b, proj_c=proj_c, w_out=w_out)
    head = np.arange(WIDTH) // HEAD_DIM
    ones_bd = jnp.asarray(head[:, None] == head[None, :], BF16)
    consts = (ones_bd, _edge_selectors())
    for l in range(norm_g.shape[0]):
        x = _layer(x, l, p, consts)
    return x
```

```python
import functools

import numpy as np
import jax
import jax.numpy as jnp
from jax import lax
from jax.experimental import pallas as pl
from jax.experimental.pallas import tpu as pltpu

F32 = jnp.float32
BF16 = jnp.bfloat16

D_MODEL = 1024
CHUNK = 64
HEAD_DIM = 64
HEADS = 8
WIDTH = HEADS * HEAD_DIM
LORA = 64
A_COLS = 4 * WIDTH + 2 * LORA
PAST_CHUNKS = 8
PAD_ROWS = PAST_CHUNKS * CHUNK
REL_CLIP = 256
RMS_EPS = 1e-6
GN_EPS = 64e-5
LEVELS = (32, 16, 8, 4, 2, 1)

TOKEN_TILE = 512
PREP_SUB_ROWS = 256
SCAN_ROWS = 8
SCAN_CHUNKS = 4
ATTN_CHUNKS = 8
ATTN_Q_CHUNKS = 2
VMEM_LIMIT = 56 * 1024 * 1024

_NN = (((1,), (0,)), ((), ()))
_NT = (((1,), (1,)), ((), ()))
_TN = (((0,), (0,)), ((), ()))


def _dg(a, b, dims):
    return lax.dot_general(a, b, dims, preferred_element_type=F32)


def _hi_lo(a):
    hi = a.astype(BF16)
    lo = (a - hi.astype(F32)).astype(BF16)
    return hi, lo


def _mm3(a, b, dims=_NN):
    a1, a2 = _hi_lo(a)
    b1, b2 = _hi_lo(b)
    return _dg(a1, b1, dims) + (_dg(a1, b2, dims) + _dg(a2, b1, dims))


def _head_sum(a, ones_bd):
    return _dg(a.astype(BF16), ones_bd, _NN)


def _block_diag(x):
    low = lax.broadcasted_iota(jnp.int32, x.shape, 1) < HEAD_DIM
    zero = jnp.zeros_like(x)
    return jnp.concatenate([jnp.where(low, x, zero), jnp.where(low, zero, x)], axis=0)


def _pair_mean(a, low):
    zero = jnp.zeros_like(a)
    lo_sum = jnp.sum(jnp.where(low, a, zero), axis=-1, keepdims=True)
    hi_sum = jnp.sum(jnp.where(low, zero, a), axis=-1, keepdims=True)
    return jnp.where(low, lo_sum, hi_sum) * (1.0 / HEAD_DIM)


def _sigmoid(x):
    return 0.5 * jnp.tanh(0.5 * x) + 0.5


def _cumsum_rows(x):
    n = x.shape[0]
    rows = lax.broadcasted_iota(jnp.int32, x.shape, 0)
    s = 1
    while s < n:
        x = x + jnp.where(rows >= s, pltpu.roll(x, s, 0), 0.0)
        s *= 2
    return x


def _chunk_loop(n_chunks, step, body):
    def it(i, carry):
        body(pl.multiple_of(i * (step * CHUNK), step * CHUNK))
        return carry
    lax.fori_loop(0, n_chunks // step, it, 0)


def _sub_blocks(n_rows):
    sub = min(PREP_SUB_ROWS, n_rows)
    return [slice(i * sub, (i + 1) * sub) for i in range(n_rows // sub)]


def _norm_proj(x, g_ref, w_ref):
    ms = jnp.mean(x * x, axis=-1, keepdims=True)
    h = x * lax.rsqrt(ms + RMS_EPS) * g_ref[...]
    return jnp.dot(h.astype(BF16), w_ref[...], preferred_element_type=F32)


def _rwkv_prep_kernel(x_ref, g_ref, w_ref, mu_ref, pa_ref, w2_ref, a2_ref, ones_ref,
                      r_o, lw_o, k_o, v_o, kk_o, b_o, bonus_o, gate_o, carry_ref):
    @pl.when(pl.program_id(1) == 0)
    def _():
        carry_ref[...] = jnp.zeros_like(carry_ref)

    w0, a0, k_k, k_a, r_k = (pa_ref[i:i + 1, :] for i in range(5))
    ones_bd = ones_ref[...]
    last = carry_ref[7:8, :]
    for rs in _sub_blocks(x_ref.shape[0]):
        za = _norm_proj(x_ref[rs, :], g_ref, w_ref)
        sub = za.shape[0]
        row = lax.broadcasted_iota(jnp.int32, (sub, 1), 0)
        prev = jnp.where(row == 0, last, pltpu.roll(za, 1, 0))
        last = za[sub - 1:sub, :]
        tail = za[sub - 8:, :]
        xs = za + (prev - za) * mu_ref[...]

        r = xs[:, 0:WIDTH]
        k = xs[:, WIDTH:2 * WIDTH]
        v = xs[:, 2 * WIDTH:3 * WIDTH]
        gate = xs[:, 3 * WIDTH:4 * WIDTH]
        wl = xs[:, 4 * WIDTH:4 * WIDTH + LORA]
        al = xs[:, 4 * WIDTH + LORA:4 * WIDTH + 2 * LORA]

        u = -(w0 + _mm3(jnp.tanh(wl), w2_ref[...]))
        softplus = jnp.maximum(u, 0.0) + jnp.log(1.0 + jnp.exp(-jnp.abs(u)))
        lw = -jnp.exp(-softplus - 0.5)
        a = _sigmoid(a0 + _mm3(al, a2_ref[...]))

        kkp = k * k_k
        kk = kkp * lax.rsqrt(jnp.maximum(_head_sum(kkp * kkp, ones_bd), 1e-24))
        k2 = k * (1.0 + (a - 1.0) * k_a)

        r_o[rs, :] = r.astype(r_o.dtype)
        lw_o[rs, :] = lw
        k_o[rs, :] = k2.astype(k_o.dtype)
        v_o[rs, :] = v.astype(v_o.dtype)
        kk_o[rs, :] = kk.astype(kk_o.dtype)
        b_o[rs, :] = (kk * a).astype(b_o.dtype)
        bonus_o[rs, :] = (_head_sum(r * k2 * r_k, ones_bd) * v).astype(bonus_o.dtype)
        gate_o[rs, :] = (gate * _sigmoid(gate)).astype(gate_o.dtype)
    carry_ref[...] = tail


def _rwkv_scan_kernel(r_ref, *refs):
    s_ref = refs[-1]

    @pl.when(pl.program_id(1) == 0)
    def _():
        s_ref[...] = jnp.zeros_like(s_ref)

    _chunk_loop(r_ref.shape[1] // CHUNK, 1, functools.partial(_rwkv_chunk, r_ref, *refs))


def _rwkv_chunk(r_ref, lw_ref, k_ref, v_ref, kk_ref, b_ref, bonus_ref, gate_ref, ln_ref,
                y_o, s_ref, off):
    rows = r_ref.shape[0]
    at = pl.ds(off, CHUNK)
    a_t, b_t, k_t, r_f, r_t, b_h, k_h, v_b, pend = ([] for _ in range(9))
    for j in range(rows):
        lw = lw_ref[j, at, :]
        cum = _cumsum_rows(lw)
        cend = cum[CHUNK - 1:CHUNK, :]
        pinv = jnp.exp(-cum)
        dend = jnp.exp(cend - cum)
        kk, bv, k2 = kk_ref[j, at, :].astype(F32), b_ref[j, at, :].astype(F32), k_ref[j, at, :].astype(F32)
        a_t.append((-kk * jnp.exp(cum - lw)).astype(BF16))
        b_t.append((bv * pinv).astype(BF16))
        k_t.append((k2 * pinv).astype(BF16))
        r_f.append(r_ref[j, at, :].astype(F32) * jnp.exp(cum))
        r_t.append(r_f[j].astype(BF16))
        b_h.append((bv * dend).astype(BF16))
        k_h.append((k2 * dend).astype(BF16))
        v_b.append(v_ref[j, at, :])
        pend.append(jnp.exp(cend))

    P2 = 2 * HEAD_DIM
    ri = lax.broadcasted_iota(jnp.int32, (CHUNK, P2), 0)
    ci = lax.broadcasted_iota(jnp.int32, (CHUNK, P2), 1) % CHUNK
    strict = ri > ci
    incl = ri >= ci
    low = lax.broadcasted_iota(jnp.int32, (CHUNK, P2), 1) < HEAD_DIM
    own = ((lax.broadcasted_iota(jnp.int32, (P2, P2), 0) < HEAD_DIM)
           == (lax.broadcasted_iota(jnp.int32, (P2, P2), 1) < HEAD_DIM))
    units = [(j, slice(p * P2, (p + 1) * P2)) for j in range(rows) for p in range(HEADS // 2)]
    un = range(len(units))
    top, bot = slice(0, CHUNK), slice(CHUNK, 2 * CHUNK)
    left, right = slice(0, P2), slice(P2, 2 * P2)

    sc = [_dg(jnp.concatenate([a_t[j][:, ln], r_t[j][:, ln]], axis=0),
              jnp.concatenate([_block_diag(b_t[j][:, ln]), _block_diag(k_t[j][:, ln])], axis=0), _NT)
          for j, ln in units]
    m = [jnp.where(strict, sc[u][top, left], 0.0) for u in un]
    m_ak = [jnp.where(strict, sc[u][top, right], 0.0).astype(BF16) for u in un]
    l_rb = [jnp.where(incl, sc[u][bot, left], 0.0).astype(BF16) for u in un]
    l_rk = [jnp.where(incl, sc[u][bot, right], 0.0).astype(BF16) for u in un]
    v_bd = [_block_diag(v_b[j][:, ln]) for j, ln in units]

    u0 = [_dg(m_ak[u], v_bd[u], _NN).astype(BF16) for u in un]
    t = [jnp.where(ri == ci, 1.0, m[u]) for u in un]
    mb = [m[u].astype(BF16) for u in un]
    mb = [_dg(mb[u], _block_diag(mb[u]), _NN).astype(BF16) for u in un]
    for i in range(1, 6):
        if i < 5:
            tm = [_dg(jnp.concatenate([t[u].astype(BF16), mb[u]], axis=0), _block_diag(mb[u]), _NN) for u in un]
            t = [t[u] + tm[u][top, :] for u in un]
            mb = [tm[u][bot, :].astype(BF16) for u in un]
        else:
            t = [t[u] + _dg(t[u].astype(BF16), _block_diag(mb[u]), _NN) for u in un]
    xb = [_dg(t[u].astype(BF16),
              jnp.concatenate([_block_diag(a_t[j][:, ln]), _block_diag(u0[u])], axis=1), _NN).astype(BF16)
          for u, (j, ln) in enumerate(units)]
    x_bd = [jnp.concatenate([_block_diag(xb[u][:, left]), _block_diag(xb[u][:, right])], axis=1) for u in un]

    s = [s_ref[u] for u in un]
    sb = [s[u].astype(BF16) for u in un]
    lx = [_dg(l_rb[u], x_bd[u], _NN) for u in un]
    lv = [_dg(l_rk[u], v_bd[u], _NN) for u in un]
    r2 = [(r_f[j][:, ln] + lx[u][:, left]).astype(BF16) for u, (j, ln) in enumerate(units)]
    y = [_dg(r2[u], sb[u], _NT) + lx[u][:, right] + lv[u] for u in un]

    q = [jnp.where(own, _dg(xb[u][:, left], b_h[j][:, ln], _TN), 0.0).astype(BF16)
         for u, (j, ln) in enumerate(units)]
    uv = [_dg(jnp.concatenate([xb[u][:, right], v_b[j][:, ln]], axis=0),
              jnp.concatenate([b_h[j][:, ln], k_h[j][:, ln]], axis=0), _TN) for u, (j, ln) in enumerate(units)]
    sq = [_dg(sb[u], q[u], _NN) for u in un]
    for u, (j, ln) in enumerate(units):
        s_ref[u] = s[u] * pend[j][:, ln] + sq[u] + jnp.where(own, uv[u], 0.0)

    bonus = [bonus_ref[j, at, :].astype(F32) for j in range(rows)]
    gate = [gate_ref[j, at, :].astype(F32) for j in range(rows)]
    for u, (j, ln) in enumerate(units):
        yc = y[u] - _pair_mean(y[u], low)
        yn = yc * lax.rsqrt(_pair_mean(yc * yc, low) + GN_EPS) * ln_ref[0:1, ln] + ln_ref[1:2, ln]
        y_o[j, at, ln] = ((yn + bonus[j][:, ln]) * gate[j][:, ln]).astype(y_o.dtype)


def _attn_prep_kernel(x_ref, g_ref, w_ref, qk_ref, ones_ref, q_o, k_o, v_o, gate_o, *, pad_tiles):
    i = pl.program_id(1)

    @pl.when(i < pad_tiles)
    def _():
        k_o[...] = jnp.zeros_like(k_o)
        v_o[...] = jnp.zeros_like(v_o)

    @pl.when(i >= pad_tiles)
    def _():
        zb = _norm_proj(x_ref[...], g_ref, w_ref)
        ones_bd = ones_ref[...]
        q = zb[:, 0:WIDTH]
        k = zb[:, WIDTH:2 * WIDTH]
        gate = zb[:, 3 * WIDTH:4 * WIDTH]
        qn = q * lax.rsqrt(_head_sum(q * q, ones_bd) * (1.0 / HEAD_DIM) + RMS_EPS) * qk_ref[0:1, :]
        kn = k * lax.rsqrt(_head_sum(k * k, ones_bd) * (1.0 / HEAD_DIM) + RMS_EPS) * qk_ref[1:2, :]
        q_o[...] = (qn * HEAD_DIM ** -0.5).astype(q_o.dtype)
        k_o[...] = kn.astype(k_o.dtype)
        v_o[...] = zb[:, 2 * WIDTH:3 * WIDTH].astype(v_o.dtype)
        gate_o[...] = (gate * _sigmoid(gate)).astype(gate_o.dtype)


def _attn_band_kernel(q_ref, *refs):
    n_chunks = q_ref.shape[0] // CHUNK
    first = pl.program_id(1) * n_chunks
    _chunk_loop(n_chunks, ATTN_Q_CHUNKS, functools.partial(_attn_group, first, q_ref, *refs))


def _attn_group(first, q_ref, k_ref, v_ref, bias_ref, gate_ref, y_o, off):
    nq = ATTN_Q_CHUNKS * CHUNK
    nk = (ATTN_Q_CHUNKS + PAST_CHUNKS) * CHUNK
    at = pl.ds(off, nq)
    n = first + off // CHUNK
    start = pl.multiple_of(first * CHUNK + off, CHUNK)
    q = q_ref[at, :]
    kb = k_ref[pl.ds(start, nk), :]
    vb = v_ref[pl.ds(start, nk), :]
    qc = lax.broadcasted_iota(jnp.int32, (nq, nk), 0) // CHUNK
    kc = lax.broadcasted_iota(jnp.int32, (nq, nk), 1) // CHUNK
    valid = (kc >= qc) & (kc <= qc + PAST_CHUNKS) & (kc >= PAST_CHUNKS - n)
    gate = gate_ref[at, :].astype(F32)
    low_q = lax.broadcasted_iota(jnp.int32, (nq, 2 * HEAD_DIM), 1) < HEAD_DIM
    for pair in range(HEADS // 2):
        lanes = slice(pair * 2 * HEAD_DIM, (pair + 1) * 2 * HEAD_DIM)
        k_bd = _block_diag(kb[:, lanes])
        v_bd = _block_diag(vb[:, lanes])
        s2 = _dg(q[:, lanes], k_bd, _NT)
        p, inv = [], []
        for i in range(2):
            s = jnp.where(valid, s2[:, i * nk:(i + 1) * nk] + bias_ref[2 * pair + i], -jnp.inf)
            e = jnp.exp(s - jnp.max(s, axis=-1, keepdims=True))
            p.append(e.astype(BF16))
            inv.append(1.0 / jnp.sum(e, axis=-1, keepdims=True))
        o = _dg(jnp.concatenate(p, axis=1), v_bd, _NN)
        y_o[at, lanes] = (o * jnp.where(low_q, inv[0], inv[1]) * gate[:, lanes]).astype(y_o.dtype)


def _hgrn_prep_kernel(x_ref, g_ref, w_ref, lb_ref, q_o, k_o, lf_o, v_o, gate_o, *, layer):
    lbp = lb_ref[...]
    e = jnp.exp(lbp - jnp.max(lbp, axis=0, keepdims=True))
    sm = e / jnp.sum(e, axis=0, keepdims=True)
    lb = jnp.sum(sm[0:layer + 1], axis=0, keepdims=True) - sm[0:1]
    zc = _norm_proj(x_ref[...], g_ref, w_ref)
    q = zc[:, 0:WIDTH]
    gate = zc[:, 3 * WIDTH:4 * WIDTH]
    fg = lb + (1.0 - lb) * _sigmoid(zc[:, WIDTH:2 * WIDTH])
    q_o[...] = (q * _sigmoid(q)).astype(q_o.dtype)
    k_o[...] = (1.0 - fg).astype(k_o.dtype)
    lf_o[...] = jnp.log(fg)
    v_o[...] = zc[:, 2 * WIDTH:3 * WIDTH].astype(v_o.dtype)
    gate_o[...] = (gate * _sigmoid(gate)).astype(gate_o.dtype)


def _hgrn_scan_kernel(q_ref, *refs):
    s_ref = refs[-1]

    @pl.when(pl.program_id(1) == 0)
    def _():
        s_ref[...] = jnp.zeros_like(s_ref)

    _chunk_loop(q_ref.shape[1] // CHUNK, 1, functools.partial(_hgrn_chunk, q_ref, *refs))


def _hgrn_chunk(q_ref, k_ref, lf_ref, v_ref, gate_ref, sel_ref, ones_ref, ng_ref, y_o, s_ref, off):
    rows = q_ref.shape[0]
    at = pl.ds(off, CHUNK)
    n_lv = len(LEVELS)
    row = lax.broadcasted_iota(jnp.int32, (CHUNK, 1), 0)
    second = [((row // w) % 2) == 1 for w in LEVELS]

    q_lv, k_lv, qk_diag, q_in, k_out, dec, v_f, v_b = ([] for _ in range(8))
    for j in range(rows):
        q, k, v_b16 = q_ref[j, at, :].astype(F32), k_ref[j, at, :].astype(F32), v_ref[j, at, :]
        b = _cumsum_rows(lf_ref[j, at, :])
        bend = b[CHUNK - 1:CHUNK, :]
        b1 = b.astype(BF16)
        rem = b - b1.astype(F32)
        b2 = rem.astype(BF16)
        b3 = (rem - b2.astype(F32)).astype(BF16)
        edges = _dg(sel_ref[...], jnp.concatenate([b1, b2, b3], axis=0), _NN)
        ql, kl = [], []
        for i in range(n_lv):
            edge = edges[i * CHUNK:(i + 1) * CHUNK, :]
            e = jnp.exp(jnp.where(second[i], b - edge, edge - b))
            ql.append(jnp.where(second[i], q * e, 0.0).astype(BF16))
            kl.append(jnp.where(second[i], 0.0, k * e).astype(BF16))
        q_lv.append(ql)
        k_lv.append(kl)
        qk_diag.append(_head_sum(q * k, ones_ref[...]))
        q_in.append((q * jnp.exp(b)).astype(BF16))
        k_out.append((k * jnp.exp(bend - b)).astype(BF16))
        dec.append(jnp.exp(bend))
        v_f.append(v_b16.astype(F32))
        v_b.append(v_b16)

    units = [(j, slice(p * 2 * HEAD_DIM, (p + 1) * 2 * HEAD_DIM)) for j in range(rows) for p in range(HEADS // 2)]
    un = range(len(units))
    low = lax.broadcasted_iota(jnp.int32, (CHUNK, 2 * HEAD_DIM), 1) < HEAD_DIM
    ri2 = lax.broadcasted_iota(jnp.int32, (CHUNK, 2 * HEAD_DIM), 0)
    ci2 = lax.broadcasted_iota(jnp.int32, (CHUNK, 2 * HEAD_DIM), 1) % CHUNK
    same2 = [(ri2 // (2 * w)) == (ci2 // (2 * w)) for w in LEVELS]
    v_bd = [_block_diag(v_b[j][:, ln]) for j, ln in units]
    att = [jnp.zeros((CHUNK, 2 * HEAD_DIM), F32) for _ in un]
    for i in range(n_lv):
        pr = [_dg(q_lv[j][i][:, ln], _block_diag(k_lv[j][i][:, ln]), _NT) for j, ln in units]
        att = [att[u] + jnp.where(same2[i], pr[u], 0.0) for u in un]
    s = [s_ref[u] for u in un]
    o_in = [_dg(q_in[j][:, ln], s[u].astype(BF16), _NT) for u, (j, ln) in enumerate(units)]
    o_at = [_dg(att[u].astype(BF16), v_bd[u], _NN) for u in un]
    vk = [_dg(v_b[j][:, ln], k_out[j][:, ln], _TN) for j, ln in units]
    own = ((lax.broadcasted_iota(jnp.int32, (2 * HEAD_DIM, 2 * HEAD_DIM), 0) < HEAD_DIM)
           == (lax.broadcasted_iota(jnp.int32, (2 * HEAD_DIM, 2 * HEAD_DIM), 1) < HEAD_DIM))
    for u, (j, ln) in enumerate(units):
        s_ref[u] = s[u] * dec[j][:, ln] + jnp.where(own, vk[u], 0.0)
    gate = [gate_ref[j, at, :].astype(F32) for j in range(rows)]
    for u, (j, ln) in enumerate(units):
        o = o_at[u] + o_in[u] + qk_diag[j][:, ln] * v_f[j][:, ln]
        o = o * lax.rsqrt(_pair_mean(o * o, low) + RMS_EPS) * ng_ref[:, ln]
        y_o[j, at, ln] = (o * gate[j][:, ln]).astype(y_o.dtype)


def _merge_kernel(x_ref, g_ref, wg_ref, ya_ref, yb_ref, yc_ref, pa_ref, pb_ref, pc_ref, wo_ref, o_ref):
    gates = _sigmoid(_norm_proj(x_ref[...], g_ref, wg_ref))
    m = (gates[:, 0:D_MODEL] * jnp.dot(ya_ref[...], pa_ref[...], preferred_element_type=F32)
         + gates[:, D_MODEL:2 * D_MODEL] * jnp.dot(yb_ref[...], pb_ref[...], preferred_element_type=F32)
         + gates[:, 2 * D_MODEL:3 * D_MODEL] * jnp.dot(yc_ref[...], pc_ref[...], preferred_element_type=F32))
    o_ref[...] = x_ref[...] + jnp.dot(m.astype(BF16), wo_ref[...], preferred_element_type=F32)


def _params(*sem):
    return pltpu.CompilerParams(dimension_semantics=sem, vmem_limit_bytes=VMEM_LIMIT)


def _full(shape):
    return pl.BlockSpec(shape, lambda *_: (0,) * len(shape))


def _edge_selectors():
    sel = np.zeros((len(LEVELS) * CHUNK, CHUNK), np.float32)
    for i, w in enumerate(LEVELS):
        for t in range(CHUNK):
            sel[i * CHUNK + t, (t // (2 * w)) * 2 * w + w - 1] = 1.0
    return jnp.asarray(np.concatenate([sel, sel, sel], axis=1), BF16)


def _band_bias(rel_bias):
    nq = ATTN_Q_CHUNKS * CHUNK
    nk = (ATTN_Q_CHUNKS + PAST_CHUNKS) * CHUNK
    period = nk + nq
    j = np.arange(period)
    far = REL_CLIP + CHUNK - 1
    w = rel_bias.astype(F32)[:, np.where(j < nk, np.clip(PAD_ROWS - j, -(CHUNK - 1), REL_CLIP) + CHUNK - 1, far)]
    heads = w.shape[0]
    return jnp.tile(w, (1, nq))[:, :nq * (period - 1)].reshape(heads, nq, period - 1)[:, :, :nk]


def _layer(x, l, p, consts):
    bsz, seq, _ = x.shape
    tm = min(TOKEN_TILE, seq)
    n_t = seq // tm
    n_c = seq // CHUNK
    ones_bd, sel = consts
    g = p['norm_g'][l][None, :]

    tok = lambda w: pl.BlockSpec((None, tm, w), lambda b, t: (b, t, 0))
    rows = min(SCAN_ROWS, bsz)
    blk = min(SCAN_CHUNKS, n_c) * CHUNK
    qblk = min(ATTN_CHUNKS, n_c) * CHUNK
    chk = lambda w: pl.BlockSpec((rows, blk, w), lambda b, c: (b, c, 0))
    qck = lambda w: pl.BlockSpec((None, qblk, w), lambda b, c: (b, c, 0))
    act = lambda dt: jax.ShapeDtypeStruct((bsz, seq, WIDTH), dt)

    w_in = p['w_in'][l]
    wa = w_in[:, :A_COLS]
    cols = np.r_[0:WIDTH, WIDTH + LORA:2 * WIDTH + LORA, 2 * WIDTH + LORA:3 * WIDTH + LORA,
                 3 * WIDTH + 2 * LORA:4 * WIDTH + 2 * LORA, WIDTH:WIDTH + LORA,
                 3 * WIDTH + LORA:3 * WIDTH + 2 * LORA]
    pa = jnp.stack([p['rwkv_w0'][l], p['rwkv_a0'][l], p['rwkv_k_k'][l], p['rwkv_k_a'][l],
                    p['rwkv_r_k'][l].reshape(WIDTH)] + [jnp.zeros((WIDTH,), F32)] * 3)
    r, lw, k2, v, kk, bv, bonus, gate_a = pl.pallas_call(
        _rwkv_prep_kernel,
        grid=(bsz, n_t),
        in_specs=[tok(D_MODEL), _full((1, D_MODEL)), _full((D_MODEL, A_COLS)), _full((1, A_COLS)),
                  _full((8, WIDTH)), _full((LORA, WIDTH)), _full((LORA, WIDTH)), _full((WIDTH, WIDTH))],
        out_specs=[tok(WIDTH)] * 8,
        out_shape=[act(BF16), act(F32)] + [act(BF16)] * 6,
        scratch_shapes=[pltpu.VMEM((8, A_COLS), F32)],
        compiler_params=_params("parallel", "arbitrary"),
        name="rwkv_prep",
    )(x, g, wa[:, cols].astype(BF16), p['rwkv_mu'][l][cols][None, :], pa,
      p['rwkv_w2'][l], p['rwkv_a2'][l], ones_bd)

    ln = jnp.stack([p['rwkv_ln_w'][l], p['rwkv_ln_b'][l]] + [jnp.zeros((WIDTH,), F32)] * 6)
    y_a = pl.pallas_call(
        _rwkv_scan_kernel,
        grid=(bsz // rows, seq // blk),
        in_specs=[chk(WIDTH)] * 8 + [_full((8, WIDTH))],
        out_specs=chk(WIDTH),
        out_shape=act(BF16),
        scratch_shapes=[pltpu.VMEM((rows * HEADS // 2, 2 * HEAD_DIM, 2 * HEAD_DIM), F32)],
        compiler_params=_params("parallel", "arbitrary"),
        name="rwkv_scan",
    )(r, lw, k2, v, kk, bv, bonus, gate_a, ln)

    pad_tiles = PAD_ROWS // tm
    wb = w_in[:, A_COLS:A_COLS + 4 * WIDTH].astype(BF16)
    qk = jnp.stack([jnp.tile(p['attn_q_norm'][l], HEADS), jnp.tile(p['attn_k_norm'][l], HEADS)]
                   + [jnp.zeros((WIDTH,), F32)] * 6)
    shift = lambda b, i: (b, jnp.maximum(i - pad_tiles, 0), 0)
    q, k_pad, v_pad, gate_b = pl.pallas_call(
        functools.partial(_attn_prep_kernel, pad_tiles=pad_tiles),
        grid=(bsz, n_t + pad_tiles),
        in_specs=[pl.BlockSpec((None, tm, D_MODEL), shift), _full((1, D_MODEL)),
                  _full((D_MODEL, 4 * WIDTH)), _full((8, WIDTH)), _full((WIDTH, WIDTH))],
        out_specs=[pl.BlockSpec((None, tm, WIDTH), shift), tok(WIDTH), tok(WIDTH),
                   pl.BlockSpec((None, tm, WIDTH), shift)],
        out_shape=[act(BF16), jax.ShapeDtypeStruct((bsz, seq + PAD_ROWS, WIDTH), BF16),
                   jax.ShapeDtypeStruct((bsz, seq + PAD_ROWS, WIDTH), BF16), act(BF16)],
        compiler_params=_params("parallel", "arbitrary"),
        name="attn_prep",
    )(x, g, wb, qk, ones_bd)

    band = pl.BlockSpec((None, seq + PAD_ROWS, WIDTH), lambda b, c: (b, 0, 0))
    bias = _band_bias(p['attn_rel_bias'][l])
    y_b = pl.pallas_call(
        _attn_band_kernel,
        grid=(bsz, seq // qblk),
        in_specs=[qck(WIDTH), band, band,
                  _full(bias.shape), qck(WIDTH)],
        out_specs=qck(WIDTH),
        out_shape=act(BF16),
        compiler_params=_params("parallel", "arbitrary"),
        name="attn_band",
    )(q, k_pad, v_pad, bias, gate_b)

    wc = w_in[:, A_COLS + 4 * WIDTH:A_COLS + 8 * WIDTH].astype(BF16)
    n_layers = p['hgrn_lb'].shape[0]
    qc, kc, lf, vc, gate_c = pl.pallas_call(
        functools.partial(_hgrn_prep_kernel, layer=l),
        grid=(bsz, n_t),
        in_specs=[tok(D_MODEL), _full((1, D_MODEL)), _full((D_MODEL, 4 * WIDTH)), _full((n_layers, WIDTH))],
        out_specs=[tok(WIDTH)] * 5,
        out_shape=[act(BF16), act(BF16), act(F32), act(BF16), act(BF16)],
        compiler_params=_params("parallel", "parallel"),
        name="hgrn_prep",
    )(x, g, wc, p['hgrn_lb'])

    y_c = pl.pallas_call(
        _hgrn_scan_kernel,
        grid=(bsz // rows, seq // blk),
        in_specs=[chk(WIDTH)] * 5 + [_full(sel.shape), _full((WIDTH, WIDTH)), _full((1, WIDTH))],
        out_specs=chk(WIDTH),
        out_shape=act(BF16),
        scratch_shapes=[pltpu.VMEM((rows * HEADS // 2, 2 * HEAD_DIM, 2 * HEAD_DIM), F32)],
        compiler_params=_params("parallel", "arbitrary"),
        name="hgrn_scan",
    )(qc, kc, lf, vc, gate_c, sel, ones_bd, jnp.tile(p['hgrn_norm'][l], HEADS)[None, :])

    wg = w_in[:, A_COLS + 8 * WIDTH:].astype(BF16)
    return pl.pallas_call(
        _merge_kernel,
        grid=(bsz, n_t),
        in_specs=[tok(D_MODEL), _full((1, D_MODEL)), _full((D_MODEL, 3 * D_MODEL)),
                  tok(WIDTH), tok(WIDTH), tok(WIDTH),
                  _full((WIDTH, D_MODEL)), _full((WIDTH, D_MODEL)), _full((WIDTH, D_MODEL)),
                  _full((D_MODEL, D_MODEL))],
        out_specs=tok(D_MODEL),
        out_shape=jax.ShapeDtypeStruct(x.shape, x.dtype),
        compiler_params=_params("parallel", "parallel"),
        name="merge",
    )(x, g, wg, y_a, y_b, y_c, p['proj_a'][l].astype(BF16), p['proj_b'][l].astype(BF16),
      p['proj_c'][l].astype(BF16), p['w_out'][l].astype(BF16))


def kernel(x, norm_g, w_in, rwkv_mu, rwkv_w0, rwkv_w2, rwkv_a0, rwkv_a2, rwkv_k_k, rwkv_k_a, rwkv_r_k,
           rwkv_ln_w, rwkv_ln_b, attn_q_norm, attn_k_norm, attn_rel_bias, hgrn_lb, hgrn_norm,
           proj_a, proj_b, proj_c, w_out):
    p = dict(norm_g=norm_g, w_in=w_in, rwkv_mu=rwkv_mu, rwkv_w0=rwkv_w0, rwkv_w2=rwkv_w2,
             rwkv_a0=rwkv_a0, rwkv_a2=rwkv_a2, rwkv_k_k=rwkv_k_k, rwkv_k_a=rwkv_k_a, rwkv_r_k=rwkv_r_k,
             rwkv_ln_w=rwkv_ln_w, rwkv_ln_b=rwkv_ln_b, attn_q_norm=attn_q_norm, attn_k_norm=attn_k_norm,
             attn_rel_bias=attn_rel_bias, hgrn_lb=hgrn_lb, hgrn_norm=hgrn_norm,
             proj_a=proj_a, proj_b=proj_b, proj_c=proj_c, w_out=w_out)
    head = np.arange(WIDTH) // HEAD_DIM
    ones_bd = jnp.asarray(head[:, None] == head[None, :], BF16)
    consts = (ones_bd, _edge_selectors())
    for l in range(norm_g.shape[0]):
        x = _layer(x, l, p, consts)
    return x
```

```python
import functools

import numpy as np
import jax
import jax.numpy as jnp
from jax import lax
from jax.experimental import pallas as pl
from jax.experimental.pallas import tpu as pltpu

F32 = jnp.float32
BF16 = jnp.bfloat16

D_MODEL = 1024
CHUNK = 64
HEAD_DIM = 64
HEADS = 8
WIDTH = HEADS * HEAD_DIM
LORA = 64
A_COLS = 4 * WIDTH + 2 * LORA
PAST_CHUNKS = 8
PAD_ROWS = PAST_CHUNKS * CHUNK
REL_CLIP = 256
RMS_EPS = 1e-6
GN_EPS = 64e-5
LEVELS = (32, 16, 8, 4, 2, 1)

TOKEN_TILE = 512
PREP_SUB_ROWS = 256
RWKV_ROWS = 8
HGRN_ROWS = 4
SCAN_CHUNKS = 4
ATTN_CHUNKS = 8
ATTN_Q_CHUNKS = 2
VMEM_LIMIT = 56 * 1024 * 1024

_NN = (((1,), (0,)), ((), ()))
_NT = (((1,), (1,)), ((), ()))
_TN = (((0,), (0,)), ((), ()))


def _dg(a, b, dims):
    return lax.dot_general(a, b, dims, preferred_element_type=F32)


def _hi_lo(a):
    hi = a.astype(BF16)
    lo = (a - hi.astype(F32)).astype(BF16)
    return hi, lo


def _mm3(a, b, dims=_NN):
    a1, a2 = _hi_lo(a)
    b1, b2 = _hi_lo(b)
    return _dg(a1, b1, dims) + (_dg(a1, b2, dims) + _dg(a2, b1, dims))


def _head_sum(a, ones_bd):
    return _dg(a.astype(BF16), ones_bd, _NN)


def _block_diag(x):
    low = lax.broadcasted_iota(jnp.int32, x.shape, 1) < HEAD_DIM
    zero = jnp.zeros_like(x)
    return jnp.concatenate([jnp.where(low, x, zero), jnp.where(low, zero, x)], axis=0)


def _pair_mean(a, low):
    zero = jnp.zeros_like(a)
    lo_sum = jnp.sum(jnp.where(low, a, zero), axis=-1, keepdims=True)
    hi_sum = jnp.sum(jnp.where(low, zero, a), axis=-1, keepdims=True)
    return jnp.where(low, lo_sum, hi_sum) * (1.0 / HEAD_DIM)


def _sigmoid(x):
    return 0.5 * jnp.tanh(0.5 * x) + 0.5


def _cumsum_rows(x):
    n = x.shape[0]
    rows = lax.broadcasted_iota(jnp.int32, x.shape, 0)
    s = 1
    while s < n:
        x = x + jnp.where(rows >= s, pltpu.roll(x, s, 0), 0.0)
        s *= 2
    return x


def _chunk_loop(n_chunks, step, body):
    def it(i, carry):
        body(pl.multiple_of(i * (step * CHUNK), step * CHUNK))
        return carry
    lax.fori_loop(0, n_chunks // step, it, 0)


def _sub_blocks(n_rows):
    sub = min(PREP_SUB_ROWS, n_rows)
    return [slice(i * sub, (i + 1) * sub) for i in range(n_rows // sub)]


def _norm_proj(x, g_ref, w_ref):
    ms = jnp.mean(x * x, axis=-1, keepdims=True)
    h = x * lax.rsqrt(ms + RMS_EPS) * g_ref[...]
    return jnp.dot(h.astype(BF16), w_ref[...], preferred_element_type=F32)


def _rwkv_prep_kernel(x_ref, g_ref, w_ref, mu_ref, pa_ref, w2_ref, a2_ref, ones_ref,
                      r_o, lw_o, k_o, v_o, kk_o, b_o, bonus_o, gate_o, carry_ref):
    @pl.when(pl.program_id(1) == 0)
    def _():
        carry_ref[...] = jnp.zeros_like(carry_ref)

    w0, a0, k_k, k_a, r_k = (pa_ref[i:i + 1, :] for i in range(5))
    ones_bd = ones_ref[...]
    last = carry_ref[7:8, :]
    for rs in _sub_blocks(x_ref.shape[0]):
        za = _norm_proj(x_ref[rs, :], g_ref, w_ref)
        sub = za.shape[0]
        row = lax.broadcasted_iota(jnp.int32, (sub, 1), 0)
        prev = jnp.where(row == 0, last, pltpu.roll(za, 1, 0))
        last = za[sub - 1:sub, :]
        tail = za[sub - 8:, :]
        xs = za + (prev - za) * mu_ref[...]

        r = xs[:, 0:WIDTH]
        k = xs[:, WIDTH:2 * WIDTH]
        v = xs[:, 2 * WIDTH:3 * WIDTH]
        gate = xs[:, 3 * WIDTH:4 * WIDTH]
        wl = xs[:, 4 * WIDTH:4 * WIDTH + LORA]
        al = xs[:, 4 * WIDTH + LORA:4 * WIDTH + 2 * LORA]

        u = -(w0 + _mm3(jnp.tanh(wl), w2_ref[...]))
        softplus = jnp.maximum(u, 0.0) + jnp.log(1.0 + jnp.exp(-jnp.abs(u)))
        lw = -jnp.exp(-softplus - 0.5)
        a = _sigmoid(a0 + _mm3(al, a2_ref[...]))

        kkp = k * k_k
        kk = kkp * lax.rsqrt(jnp.maximum(_head_sum(kkp * kkp, ones_bd), 1e-24))
        k2 = k * (1.0 + (a - 1.0) * k_a)

        r_o[rs, :] = r.astype(r_o.dtype)
        lw_o[rs, :] = lw
        k_o[rs, :] = k2.astype(k_o.dtype)
        v_o[rs, :] = v.astype(v_o.dtype)
        kk_o[rs, :] = kk.astype(kk_o.dtype)
        b_o[rs, :] = (kk * a).astype(b_o.dtype)
        bonus_o[rs, :] = (_head_sum(r * k2 * r_k, ones_bd) * v).astype(bonus_o.dtype)
        gate_o[rs, :] = (gate * _sigmoid(gate)).astype(gate_o.dtype)
    carry_ref[...] = tail


def _rwkv_scan_kernel(r_ref, *refs):
    s_ref = refs[-1]

    @pl.when(pl.program_id(1) == 0)
    def _():
        s_ref[...] = jnp.zeros_like(s_ref)

    _chunk_loop(r_ref.shape[1] // CHUNK, 1, functools.partial(_rwkv_chunk, r_ref, *refs))


def _rwkv_chunk(r_ref, lw_ref, k_ref, v_ref, kk_ref, b_ref, bonus_ref, gate_ref, ln_ref,
                y_o, s_ref, off):
    rows = r_ref.shape[0]
    at = pl.ds(off, CHUNK)
    a_t, b_t, k_t, r_f, r_t, b_h, k_h, v_b, pend = ([] for _ in range(9))
    for j in range(rows):
        lw = lw_ref[j, at, :]
        cum = _cumsum_rows(lw)
        cend = cum[CHUNK - 1:CHUNK, :]
        pinv = jnp.exp(-cum)
        dend = jnp.exp(cend - cum)
        kk, bv, k2 = kk_ref[j, at, :].astype(F32), b_ref[j, at, :].astype(F32), k_ref[j, at, :].astype(F32)
        a_t.append((-kk * jnp.exp(cum - lw)).astype(BF16))
        b_t.append((bv * pinv).astype(BF16))
        k_t.append((k2 * pinv).astype(BF16))
        r_f.append(r_ref[j, at, :].astype(F32) * jnp.exp(cum))
        r_t.append(r_f[j].astype(BF16))
        b_h.append((bv * dend).astype(BF16))
        k_h.append((k2 * dend).astype(BF16))
        v_b.append(v_ref[j, at, :])
        pend.append(jnp.exp(cend))

    P2 = 2 * HEAD_DIM
    ri = lax.broadcasted_iota(jnp.int32, (CHUNK, P2), 0)
    ci = lax.broadcasted_iota(jnp.int32, (CHUNK, P2), 1) % CHUNK
    strict = ri > ci
    incl = ri >= ci
    low = lax.broadcasted_iota(jnp.int32, (CHUNK, P2), 1) < HEAD_DIM
    own = ((lax.broadcasted_iota(jnp.int32, (P2, P2), 0) < HEAD_DIM)
           == (lax.broadcasted_iota(jnp.int32, (P2, P2), 1) < HEAD_DIM))
    units = [(j, slice(p * P2, (p + 1) * P2)) for j in range(rows) for p in range(HEADS // 2)]
    un = range(len(units))
    top, bot = slice(0, CHUNK), slice(CHUNK, 2 * CHUNK)
    left, right = slice(0, P2), slice(P2, 2 * P2)

    sc = [_dg(jnp.concatenate([a_t[j][:, ln], r_t[j][:, ln]], axis=0),
              jnp.concatenate([_block_diag(b_t[j][:, ln]), _block_diag(k_t[j][:, ln])], axis=0), _NT)
          for j, ln in units]
    m = [jnp.where(strict, sc[u][top, left], 0.0) for u in un]
    m_ak = [jnp.where(strict, sc[u][top, right], 0.0).astype(BF16) for u in un]
    l_rb = [jnp.where(incl, sc[u][bot, left], 0.0).astype(BF16) for u in un]
    l_rk = [jnp.where(incl, sc[u][bot, right], 0.0).astype(BF16) for u in un]
    v_bd = [_block_diag(v_b[j][:, ln]) for j, ln in units]

    u0 = [_dg(m_ak[u], v_bd[u], _NN).astype(BF16) for u in un]
    t = [jnp.where(ri == ci, 1.0, m[u]) for u in un]
    mb = [m[u].astype(BF16) for u in un]
    mb = [_dg(mb[u], _block_diag(mb[u]), _NN).astype(BF16) for u in un]
    for i in range(1, 6):
        if i < 5:
            tm = [_dg(jnp.concatenate([t[u].astype(BF16), mb[u]], axis=0), _block_diag(mb[u]), _NN) for u in un]
            t = [t[u] + tm[u][top, :] for u in un]
            mb = [tm[u][bot, :].astype(BF16) for u in un]
        else:
            t = [t[u] + _dg(t[u].astype(BF16), _block_diag(mb[u]), _NN) for u in un]
    xb = [_dg(t[u].astype(BF16),
              jnp.concatenate([_block_diag(a_t[j][:, ln]), _block_diag(u0[u])], axis=1), _NN).astype(BF16)
          for u, (j, ln) in enumerate(units)]
    x_bd = [jnp.concatenate([_block_diag(xb[u][:, left]), _block_diag(xb[u][:, right])], axis=1) for u in un]

    s = [s_ref[u] for u in un]
    sb = [s[u].astype(BF16) for u in un]
    lx = [_dg(l_rb[u], x_bd[u], _NN) for u in un]
    lv = [_dg(l_rk[u], v_bd[u], _NN) for u in un]
    r2 = [(r_f[j][:, ln] + lx[u][:, left]).astype(BF16) for u, (j, ln) in enumerate(units)]
    y = [_dg(r2[u], sb[u], _NT) + lx[u][:, right] + lv[u] for u in un]

    q = [jnp.where(own, _dg(xb[u][:, left], b_h[j][:, ln], _TN), 0.0).astype(BF16)
         for u, (j, ln) in enumerate(units)]
    uv = [_dg(jnp.concatenate([xb[u][:, right], v_b[j][:, ln]], axis=0),
              jnp.concatenate([b_h[j][:, ln], k_h[j][:, ln]], axis=0), _TN) for u, (j, ln) in enumerate(units)]
    sq = [_dg(sb[u], q[u], _NN) for u in un]
    for u, (j, ln) in enumerate(units):
        s_ref[u] = s[u] * pend[j][:, ln] + sq[u] + jnp.where(own, uv[u], 0.0)

    bonus = [bonus_ref[j, at, :].astype(F32) for j in range(rows)]
    gate = [gate_ref[j, at, :].astype(F32) for j in range(rows)]
    for u, (j, ln) in enumerate(units):
        yc = y[u] - _pair_mean(y[u], low)
        yn = yc * lax.rsqrt(_pair_mean(yc * yc, low) + GN_EPS) * ln_ref[0:1, ln] + ln_ref[1:2, ln]
        y_o[j, at, ln] = ((yn + bonus[j][:, ln]) * gate[j][:, ln]).astype(y_o.dtype)


def _attn_prep_kernel(x_ref, g_ref, w_ref, qk_ref, ones_ref, q_o, k_o, v_o, gate_o, *, pad_tiles):
    i = pl.program_id(1)

    @pl.when(i < pad_tiles)
    def _():
        k_o[...] = jnp.zeros_like(k_o)
        v_o[...] = jnp.zeros_like(v_o)

    @pl.when(i >= pad_tiles)
    def _():
        zb = _norm_proj(x_ref[...], g_ref, w_ref)
        ones_bd = ones_ref[...]
        q = zb[:, 0:WIDTH]
        k = zb[:, WIDTH:2 * WIDTH]
        gate = zb[:, 3 * WIDTH:4 * WIDTH]
        qn = q * lax.rsqrt(_head_sum(q * q, ones_bd) * (1.0 / HEAD_DIM) + RMS_EPS) * qk_ref[0:1, :]
        kn = k * lax.rsqrt(_head_sum(k * k, ones_bd) * (1.0 / HEAD_DIM) + RMS_EPS) * qk_ref[1:2, :]
        q_o[...] = (qn * HEAD_DIM ** -0.5).astype(q_o.dtype)
        k_o[...] = kn.astype(k_o.dtype)
        v_o[...] = zb[:, 2 * WIDTH:3 * WIDTH].astype(v_o.dtype)
        gate_o[...] = (gate * _sigmoid(gate)).astype(gate_o.dtype)


def _attn_band_kernel(q_ref, *refs):
    n_chunks = q_ref.shape[0] // CHUNK
    first = pl.program_id(1) * n_chunks
    _chunk_loop(n_chunks, ATTN_Q_CHUNKS, functools.partial(_attn_group, first, q_ref, *refs))


def _attn_group(first, q_ref, k_ref, v_ref, bias_ref, gate_ref, y_o, off):
    nq = ATTN_Q_CHUNKS * CHUNK
    nk = (ATTN_Q_CHUNKS + PAST_CHUNKS) * CHUNK
    at = pl.ds(off, nq)
    n = first + off // CHUNK
    start = pl.multiple_of(first * CHUNK + off, CHUNK)
    q = q_ref[at, :]
    kb = k_ref[pl.ds(start, nk), :]
    vb = v_ref[pl.ds(start, nk), :]
    qc = lax.broadcasted_iota(jnp.int32, (nq, nk), 0) // CHUNK
    kc = lax.broadcasted_iota(jnp.int32, (nq, nk), 1) // CHUNK
    valid = (kc >= qc) & (kc <= qc + PAST_CHUNKS) & (kc >= PAST_CHUNKS - n)
    gate = gate_ref[at, :].astype(F32)
    low_q = lax.broadcasted_iota(jnp.int32, (nq, 2 * HEAD_DIM), 1) < HEAD_DIM
    for pair in range(HEADS // 2):
        lanes = slice(pair * 2 * HEAD_DIM, (pair + 1) * 2 * HEAD_DIM)
        k_bd = _block_diag(kb[:, lanes])
        v_bd = _block_diag(vb[:, lanes])
        s2 = _dg(q[:, lanes], k_bd, _NT)
        p, inv = [], []
        for i in range(2):
            s = jnp.where(valid, s2[:, i * nk:(i + 1) * nk] + bias_ref[2 * pair + i], -jnp.inf)
            e = jnp.exp(s - jnp.max(s, axis=-1, keepdims=True))
            p.append(e.astype(BF16))
            inv.append(1.0 / jnp.sum(e, axis=-1, keepdims=True))
        o = _dg(jnp.concatenate(p, axis=1), v_bd, _NN)
        y_o[at, lanes] = (o * jnp.where(low_q, inv[0], inv[1]) * gate[:, lanes]).astype(y_o.dtype)


def _hgrn_prep_kernel(x_ref, g_ref, w_ref, lb_ref, q_o, k_o, lf_o, v_o, gate_o, *, layer):
    lbp = lb_ref[...]
    e = jnp.exp(lbp - jnp.max(lbp, axis=0, keepdims=True))
    sm = e / jnp.sum(e, axis=0, keepdims=True)
    lb = jnp.sum(sm[0:layer + 1], axis=0, keepdims=True) - sm[0:1]
    zc = _norm_proj(x_ref[...], g_ref, w_ref)
    q = zc[:, 0:WIDTH]
    gate = zc[:, 3 * WIDTH:4 * WIDTH]
    fg = lb + (1.0 - lb) * _sigmoid(zc[:, WIDTH:2 * WIDTH])
    q_o[...] = (q * _sigmoid(q)).astype(q_o.dtype)
    k_o[...] = (1.0 - fg).astype(k_o.dtype)
    lf_o[...] = jnp.log(fg)
    v_o[...] = zc[:, 2 * WIDTH:3 * WIDTH].astype(v_o.dtype)
    gate_o[...] = (gate * _sigmoid(gate)).astype(gate_o.dtype)


def _hgrn_scan_kernel(q_ref, *refs):
    s_ref = refs[-1]

    @pl.when(pl.program_id(1) == 0)
    def _():
        s_ref[...] = jnp.zeros_like(s_ref)

    _chunk_loop(q_ref.shape[1] // CHUNK, 1, functools.partial(_hgrn_chunk, q_ref, *refs))


def _hgrn_chunk(q_ref, k_ref, lf_ref, v_ref, gate_ref, sel_ref, ones_ref, ng_ref, y_o, s_ref, off):
    rows = q_ref.shape[0]
    at = pl.ds(off, CHUNK)
    n_lv = len(LEVELS)
    row = lax.broadcasted_iota(jnp.int32, (CHUNK, 1), 0)
    second = [((row // w) % 2) == 1 for w in LEVELS]
    sign = [jnp.where(sec, 1.0, -1.0) for sec in second]

    z_lv, qk_diag, q_in, k_out, dec, v_f, v_b = ([] for _ in range(7))
    for j in range(rows):
        q, k, v_b16 = q_ref[j, at, :].astype(F32), k_ref[j, at, :].astype(F32), v_ref[j, at, :]
        b = _cumsum_rows(lf_ref[j, at, :])
        bend = b[CHUNK - 1:CHUNK, :]
        b1 = b.astype(BF16)
        rem = b - b1.astype(F32)
        b2 = rem.astype(BF16)
        b3 = (rem - b2.astype(F32)).astype(BF16)
        edges = _dg(sel_ref[...], jnp.concatenate([b1, b2, b3], axis=0), _NN)
        zl = []
        for i in range(n_lv):
            edge = edges[i * CHUNK:(i + 1) * CHUNK, :]
            e = jnp.exp((b - edge) * sign[i])
            zl.append((jnp.where(second[i], q, k) * e).astype(BF16))
        z_lv.append(zl)
        qk_diag.append(_head_sum(q * k, ones_ref[...]))
        q_in.append((q * jnp.exp(b)).astype(BF16))
        k_out.append((k * jnp.exp(bend - b)).astype(BF16))
        dec.append(jnp.exp(bend))
        v_f.append(v_b16.astype(F32))
        v_b.append(v_b16)

    units = [(j, slice(p * 2 * HEAD_DIM, (p + 1) * 2 * HEAD_DIM)) for j in range(rows) for p in range(HEADS // 2)]
    un = range(len(units))
    low = lax.broadcasted_iota(jnp.int32, (CHUNK, 2 * HEAD_DIM), 1) < HEAD_DIM
    ri2 = lax.broadcasted_iota(jnp.int32, (CHUNK, 2 * HEAD_DIM), 0)
    ci2 = lax.broadcasted_iota(jnp.int32, (CHUNK, 2 * HEAD_DIM), 1) % CHUNK
    keep = [((ri2 // (2 * w)) == (ci2 // (2 * w))) & ((ri2 // w) % 2 == 1) & ((ci2 // w) % 2 == 0) for w in LEVELS]
    v_bd = [_block_diag(v_b[j][:, ln]) for j, ln in units]
    att = []
    for j, ln in units:
        pr = [_dg(z[:, ln], _block_diag(z[:, ln]), _NT) for z in z_lv[j]]
        att.append(sum(jnp.where(keep[i], pr[i], 0.0) for i in range(n_lv)))
    s = [s_ref[u] for u in un]
    o_in = [_dg(q_in[j][:, ln], s[u].astype(BF16), _NT) for u, (j, ln) in enumerate(units)]
    o_at = [_dg(att[u].astype(BF16), v_bd[u], _NN) for u in un]
    vk = [_dg(v_b[j][:, ln], k_out[j][:, ln], _TN) for j, ln in units]
    own = ((lax.broadcasted_iota(jnp.int32, (2 * HEAD_DIM, 2 * HEAD_DIM), 0) < HEAD_DIM)
           == (lax.broadcasted_iota(jnp.int32, (2 * HEAD_DIM, 2 * HEAD_DIM), 1) < HEAD_DIM))
    for u, (j, ln) in enumerate(units):
        s_ref[u] = s[u] * dec[j][:, ln] + jnp.where(own, vk[u], 0.0)
    gate = [gate_ref[j, at, :].astype(F32) for j in range(rows)]
    for u, (j, ln) in enumerate(units):
        o = o_at[u] + o_in[u] + qk_diag[j][:, ln] * v_f[j][:, ln]
        o = o * lax.rsqrt(_pair_mean(o * o, low) + RMS_EPS) * ng_ref[:, ln]
        y_o[j, at, ln] = (o * gate[j][:, ln]).astype(y_o.dtype)


def _merge_kernel(x_ref, g_ref, wg_ref, ya_ref, yb_ref, yc_ref, pa_ref, pb_ref, pc_ref, wo_ref, o_ref):
    gates = _sigmoid(_norm_proj(x_ref[...], g_ref, wg_ref))
    m = (gates[:, 0:D_MODEL] * jnp.dot(ya_ref[...], pa_ref[...], preferred_element_type=F32)
         + gates[:, D_MODEL:2 * D_MODEL] * jnp.dot(yb_ref[...], pb_ref[...], preferred_element_type=F32)
         + gates[:, 2 * D_MODEL:3 * D_MODEL] * jnp.dot(yc_ref[...], pc_ref[...], preferred_element_type=F32))
    o_ref[...] = x_ref[...] + jnp.dot(m.astype(BF16), wo_ref[...], preferred_element_type=F32)


def _params(*sem):
    return pltpu.CompilerParams(dimension_semantics=sem, vmem_limit_bytes=VMEM_LIMIT)


def _full(shape):
    return pl.BlockSpec(shape, lambda *_: (0,) * len(shape))


def _edge_selectors():
    sel = np.zeros((len(LEVELS) * CHUNK, CHUNK), np.float32)
    for i, w in enumerate(LEVELS):
        for t in range(CHUNK):
            sel[i * CHUNK + t, (t // (2 * w)) * 2 * w + w - 1] = 1.0
    return jnp.asarray(np.concatenate([sel, sel, sel], axis=1), BF16)


def _band_bias(rel_bias):
    nq = ATTN_Q_CHUNKS * CHUNK
    nk = (ATTN_Q_CHUNKS + PAST_CHUNKS) * CHUNK
    period = nk + nq
    j = np.arange(period)
    far = REL_CLIP + CHUNK - 1
    w = rel_bias.astype(F32)[:, np.where(j < nk, np.clip(PAD_ROWS - j, -(CHUNK - 1), REL_CLIP) + CHUNK - 1, far)]
    heads = w.shape[0]
    return jnp.tile(w, (1, nq))[:, :nq * (period - 1)].reshape(heads, nq, period - 1)[:, :, :nk]


def _layer(x, l, p, consts):
    bsz, seq, _ = x.shape
    tm = min(TOKEN_TILE, seq)
    n_t = seq // tm
    n_c = seq // CHUNK
    ones_bd, sel = consts
    g = p['norm_g'][l][None, :]

    tok = lambda w: pl.BlockSpec((None, tm, w), lambda b, t: (b, t, 0))
    rows_a, rows_c = min(RWKV_ROWS, bsz), min(HGRN_ROWS, bsz)
    blk = min(SCAN_CHUNKS, n_c) * CHUNK
    qblk = min(ATTN_CHUNKS, n_c) * CHUNK
    chk = lambda w, rows: pl.BlockSpec((rows, blk, w), lambda b, c: (b, c, 0))
    qck = lambda w: pl.BlockSpec((None, qblk, w), lambda b, c: (b, c, 0))
    act = lambda dt: jax.ShapeDtypeStruct((bsz, seq, WIDTH), dt)

    w_in = p['w_in'][l]
    wa = w_in[:, :A_COLS]
    cols = np.r_[0:WIDTH, WIDTH + LORA:2 * WIDTH + LORA, 2 * WIDTH + LORA:3 * WIDTH + LORA,
                 3 * WIDTH + 2 * LORA:4 * WIDTH + 2 * LORA, WIDTH:WIDTH + LORA,
                 3 * WIDTH + LORA:3 * WIDTH + 2 * LORA]
    pa = jnp.stack([p['rwkv_w0'][l], p['rwkv_a0'][l], p['rwkv_k_k'][l], p['rwkv_k_a'][l],
                    p['rwkv_r_k'][l].reshape(WIDTH)] + [jnp.zeros((WIDTH,), F32)] * 3)
    r, lw, k2, v, kk, bv, bonus, gate_a = pl.pallas_call(
        _rwkv_prep_kernel,
        grid=(bsz, n_t),
        in_specs=[tok(D_MODEL), _full((1, D_MODEL)), _full((D_MODEL, A_COLS)), _full((1, A_COLS)),
                  _full((8, WIDTH)), _full((LORA, WIDTH)), _full((LORA, WIDTH)), _full((WIDTH, WIDTH))],
        out_specs=[tok(WIDTH)] * 8,
        out_shape=[act(BF16), act(F32)] + [act(BF16)] * 6,
        scratch_shapes=[pltpu.VMEM((8, A_COLS), F32)],
        compiler_params=_params("parallel", "arbitrary"),
        name="rwkv_prep",
    )(x, g, wa[:, cols].astype(BF16), p['rwkv_mu'][l][cols][None, :], pa,
      p['rwkv_w2'][l], p['rwkv_a2'][l], ones_bd)

    ln = jnp.stack([p['rwkv_ln_w'][l], p['rwkv_ln_b'][l]] + [jnp.zeros((WIDTH,), F32)] * 6)
    y_a = pl.pallas_call(
        _rwkv_scan_kernel,
        grid=(bsz // rows_a, seq // blk),
        in_specs=[chk(WIDTH, rows_a)] * 8 + [_full((8, WIDTH))],
        out_specs=chk(WIDTH, rows_a),
        out_shape=act(BF16),
        scratch_shapes=[pltpu.VMEM((rows_a * HEADS // 2, 2 * HEAD_DIM, 2 * HEAD_DIM), F32)],
        compiler_params=_params("parallel", "arbitrary"),
        name="rwkv_scan",
    )(r, lw, k2, v, kk, bv, bonus, gate_a, ln)

    pad_tiles = PAD_ROWS // tm
    wb = w_in[:, A_COLS:A_COLS + 4 * WIDTH].astype(BF16)
    qk = jnp.stack([jnp.tile(p['attn_q_norm'][l], HEADS), jnp.tile(p['attn_k_norm'][l], HEADS)]
                   + [jnp.zeros((WIDTH,), F32)] * 6)
    shift = lambda b, i: (b, jnp.maximum(i - pad_tiles, 0), 0)
    q, k_pad, v_pad, gate_b = pl.pallas_call(
        functools.partial(_attn_prep_kernel, pad_tiles=pad_tiles),
        grid=(bsz, n_t + pad_tiles),
        in_specs=[pl.BlockSpec((None, tm, D_MODEL), shift), _full((1, D_MODEL)),
                  _full((D_MODEL, 4 * WIDTH)), _full((8, WIDTH)), _full((WIDTH, WIDTH))],
        out_specs=[pl.BlockSpec((None, tm, WIDTH), shift), tok(WIDTH), tok(WIDTH),
                   pl.BlockSpec((None, tm, WIDTH), shift)],
        out_shape=[act(BF16), jax.ShapeDtypeStruct((bsz, seq + PAD_ROWS, WIDTH), BF16),
                   jax.ShapeDtypeStruct((bsz, seq + PAD_ROWS, WIDTH), BF16), act(BF16)],
        compiler_params=_params("parallel", "arbitrary"),
        name="attn_prep",
    )(x, g, wb, qk, ones_bd)

    band = pl.BlockSpec((None, seq + PAD_ROWS, WIDTH), lambda b, c: (b, 0, 0))
    bias = _band_bias(p['attn_rel_bias'][l])
    y_b = pl.pallas_call(
        _attn_band_kernel,
        grid=(bsz, seq // qblk),
        in_specs=[qck(WIDTH), band, band,
                  _full(bias.shape), qck(WIDTH)],
        out_specs=qck(WIDTH),
        out_shape=act(BF16),
        compiler_params=_params("parallel", "arbitrary"),
        name="attn_band",
    )(q, k_pad, v_pad, bias, gate_b)

    wc = w_in[:, A_COLS + 4 * WIDTH:A_COLS + 8 * WIDTH].astype(BF16)
    n_layers = p['hgrn_lb'].shape[0]
    qc, kc, lf, vc, gate_c = pl.pallas_call(
        functools.partial(_hgrn_prep_kernel, layer=l),
        grid=(bsz, n_t),
        in_specs=[tok(D_MODEL), _full((1, D_MODEL)), _full((D_MODEL, 4 * WIDTH)), _full((n_layers, WIDTH))],
        out_specs=[tok(WIDTH)] * 5,
        out_shape=[act(BF16), act(BF16), act(F32), act(BF16), act(BF16)],
        compiler_params=_params("parallel", "parallel"),
        name="hgrn_prep",
    )(x, g, wc, p['hgrn_lb'])

    y_c = pl.pallas_call(
        _hgrn_scan_kernel,
        grid=(bsz // rows_c, seq // blk),
        in_specs=[chk(WIDTH, rows_c)] * 5 + [_full(sel.shape), _full((WIDTH, WIDTH)), _full((1, WIDTH))],
        out_specs=chk(WIDTH, rows_c),
        out_shape=act(BF16),
        scratch_shapes=[pltpu.VMEM((rows_c * HEADS // 2, 2 * HEAD_DIM, 2 * HEAD_DIM), F32)],
        compiler_params=_params("parallel", "arbitrary"),
        name="hgrn_scan",
    )(qc, kc, lf, vc, gate_c, sel, ones_bd, jnp.tile(p['hgrn_norm'][l], HEADS)[None, :])

    wg = w_in[:, A_COLS + 8 * WIDTH:].astype(BF16)
    return pl.pallas_call(
        _merge_kernel,
        grid=(bsz, n_t),
        in_specs=[tok(D_MODEL), _full((1, D_MODEL)), _full((D_MODEL, 3 * D_MODEL)),
                  tok(WIDTH), tok(WIDTH), tok(WIDTH),
                  _full((WIDTH, D_MODEL)), _full((WIDTH, D_MODEL)), _full((WIDTH, D_MODEL)),
                  _full((D_MODEL, D_MODEL))],
        out_specs=tok(D_MODEL),
        out_shape=jax.ShapeDtypeStruct(x.shape, x.dtype),
        compiler_params=_params("parallel", "parallel"),
        name="merge",
    )(x, g, wg, y_a, y_b, y_c, p['proj_a'][l].astype(BF16), p['proj_b'][l].astype(BF16),
      p['proj_c'][l].astype(BF16), p['w_out'][l].astype(BF16))


def kernel(x, norm_g, w_in, rwkv_mu, rwkv_w0, rwkv_w2, rwkv_a0, rwkv_a2, rwkv_k_k, rwkv_k_a, rwkv_r_k,
           rwkv_ln_w, rwkv_ln_b, attn_q_norm, attn_k_norm, attn_rel_bias, hgrn_lb, hgrn_norm,
           proj_a, proj_b, proj_c, w_out):
    p = dict(norm_g=norm_g, w_in=w_in, rwkv_mu=rwkv_mu, rwkv_w0=rwkv_w0, rwkv_w2=rwkv_w2,
             rwkv_a0=rwkv_a0, rwkv_a2=rwkv_a2, rwkv_k_k=rwkv_k_k, rwkv_k_a=rwkv_k_a, rwkv_r_k=rwkv_r_k,
             rwkv_ln_w=rwkv_ln_w, rwkv_ln_b=rwkv_ln_b, attn_q_norm=attn_q_norm, attn_k_norm=attn_k_norm,
             attn_rel_bias=attn_rel_bias, hgrn_lb=hgrn_lb, hgrn_norm=hgrn_norm,
             proj_a=proj_a, proj_b=proj_b, proj_c=proj_c, w_out=w_out)
    head = np.arange(WIDTH) // HEAD_DIM
    ones_bd = jnp.asarray(head[:, None] == head[None, :], BF16)
    consts = (ones_bd, _edge_selectors())
    for l in range(norm_g.shape[0]):
        x = _layer(x, l, p, consts)
    return x
```

```python
import functools

import numpy as np
import jax
import jax.numpy as jnp
from jax import lax
from jax.experimental import pallas as pl
from jax.experimental.pallas import tpu as pltpu

F32 = jnp.float32
BF16 = jnp.bfloat16

D_MODEL = 1024
CHUNK = 64
HEAD_DIM = 64
HEADS = 8
WIDTH = HEADS * HEAD_DIM
LORA = 64
A_COLS = 4 * WIDTH + 2 * LORA
PAST_CHUNKS = 8
PAD_ROWS = PAST_CHUNKS * CHUNK
REL_CLIP = 256
RMS_EPS = 1e-6
GN_EPS = 64e-5
LEVELS = (32, 16, 8, 4, 2, 1)

TOKEN_TILE = 512
PREP_SUB_ROWS = 256
RWKV_ROWS = 8
HGRN_ROWS = 8
SCAN_CHUNKS = 4
ATTN_CHUNKS = 8
ATTN_Q_CHUNKS = 2
VMEM_LIMIT = 56 * 1024 * 1024

_NN = (((1,), (0,)), ((), ()))
_NT = (((1,), (1,)), ((), ()))
_TN = (((0,), (0,)), ((), ()))


def _dg(a, b, dims):
    return lax.dot_general(a, b, dims, preferred_element_type=F32)


def _hi_lo(a):
    hi = a.astype(BF16)
    lo = (a - hi.astype(F32)).astype(BF16)
    return hi, lo


def _mm3(a, b, dims=_NN):
    a1, a2 = _hi_lo(a)
    b1, b2 = _hi_lo(b)
    return _dg(a1, b1, dims) + (_dg(a1, b2, dims) + _dg(a2, b1, dims))


def _head_sum(a, ones_bd):
    return _dg(a.astype(BF16), ones_bd, _NN)


def _block_diag(x):
    low = lax.broadcasted_iota(jnp.int32, x.shape, 1) < HEAD_DIM
    zero = jnp.zeros_like(x)
    return jnp.concatenate([jnp.where(low, x, zero), jnp.where(low, zero, x)], axis=0)


def _block_diag_cols(xt):
    top = lax.broadcasted_iota(jnp.int32, xt.shape, 0) < HEAD_DIM
    zero = jnp.zeros_like(xt)
    return jnp.concatenate([jnp.where(top, xt, zero), jnp.where(top, zero, xt)], axis=1)


def _pair_mean(a, low):
    zero = jnp.zeros_like(a)
    lo_sum = jnp.sum(jnp.where(low, a, zero), axis=-1, keepdims=True)
    hi_sum = jnp.sum(jnp.where(low, zero, a), axis=-1, keepdims=True)
    return jnp.where(low, lo_sum, hi_sum) * (1.0 / HEAD_DIM)


def _sigmoid(x):
    return 0.5 * jnp.tanh(0.5 * x) + 0.5


def _cumsum_rows(x):
    n = x.shape[0]
    rows = lax.broadcasted_iota(jnp.int32, x.shape, 0)
    s = 1
    while s < n:
        x = x + jnp.where(rows >= s, pltpu.roll(x, s, 0), 0.0)
        s *= 2
    return x


def _chunk_loop(n_chunks, step, body):
    def it(i, carry):
        body(pl.multiple_of(i * (step * CHUNK), step * CHUNK))
        return carry
    lax.fori_loop(0, n_chunks // step, it, 0)


def _sub_blocks(n_rows):
    sub = min(PREP_SUB_ROWS, n_rows)
    return [slice(i * sub, (i + 1) * sub) for i in range(n_rows // sub)]


def _norm_proj(x, g_ref, w_ref):
    ms = jnp.mean(x * x, axis=-1, keepdims=True)
    h = x * lax.rsqrt(ms + RMS_EPS) * g_ref[...]
    return jnp.dot(h.astype(BF16), w_ref[...], preferred_element_type=F32)


def _rwkv_prep_kernel(x_ref, g_ref, w_ref, mu_ref, pa_ref, w2_ref, a2_ref, ones_ref,
                      r_o, lw_o, k_o, v_o, kk_o, b_o, bonus_o, gate_o, carry_ref):
    @pl.when(pl.program_id(1) == 0)
    def _():
        carry_ref[...] = jnp.zeros_like(carry_ref)

    w0, a0, k_k, k_a, r_k = (pa_ref[i:i + 1, :] for i in range(5))
    ones_bd = ones_ref[...]
    last = carry_ref[7:8, :]
    for rs in _sub_blocks(x_ref.shape[0]):
        za = _norm_proj(x_ref[rs, :], g_ref, w_ref)
        sub = za.shape[0]
        row = lax.broadcasted_iota(jnp.int32, (sub, 1), 0)
        prev = jnp.where(row == 0, last, pltpu.roll(za, 1, 0))
        last = za[sub - 1:sub, :]
        tail = za[sub - 8:, :]
        xs = za + (prev - za) * mu_ref[...]

        r = xs[:, 0:WIDTH]
        k = xs[:, WIDTH:2 * WIDTH]
        v = xs[:, 2 * WIDTH:3 * WIDTH]
        gate = xs[:, 3 * WIDTH:4 * WIDTH]
        wl = xs[:, 4 * WIDTH:4 * WIDTH + LORA]
        al = xs[:, 4 * WIDTH + LORA:4 * WIDTH + 2 * LORA]

        u = -(w0 + _mm3(jnp.tanh(wl), w2_ref[...]))
        softplus = jnp.maximum(u, 0.0) + jnp.log(1.0 + jnp.exp(-jnp.abs(u)))
        lw = -jnp.exp(-softplus - 0.5)
        a = _sigmoid(a0 + _mm3(al, a2_ref[...]))

        kkp = k * k_k
        kk = kkp * lax.rsqrt(jnp.maximum(_head_sum(kkp * kkp, ones_bd), 1e-24))
        k2 = k * (1.0 + (a - 1.0) * k_a)

        r_o[rs, :] = r.astype(r_o.dtype)
        lw_o[rs, :] = lw
        k_o[rs, :] = k2.astype(k_o.dtype)
        v_o[rs, :] = v.astype(v_o.dtype)
        kk_o[rs, :] = kk.astype(kk_o.dtype)
        b_o[rs, :] = (kk * a).astype(b_o.dtype)
        bonus_o[rs, :] = (_head_sum(r * k2 * r_k, ones_bd) * v).astype(bonus_o.dtype)
        gate_o[rs, :] = (gate * _sigmoid(gate)).astype(gate_o.dtype)
    carry_ref[...] = tail


def _rwkv_scan_kernel(r_ref, *refs):
    s_ref = refs[-1]

    @pl.when(pl.program_id(1) == 0)
    def _():
        s_ref[...] = jnp.zeros_like(s_ref)

    _chunk_loop(r_ref.shape[1] // CHUNK, 1, functools.partial(_rwkv_chunk, r_ref, *refs))


def _rwkv_chunk(r_ref, lw_ref, k_ref, v_ref, kk_ref, b_ref, bonus_ref, gate_ref, ln_ref,
                y_o, s_ref, off):
    rows = r_ref.shape[0]
    at = pl.ds(off, CHUNK)
    a_t, b_t, k_t, r_f, r_t, b_h, k_h, v_b, pend = ([] for _ in range(9))
    for j in range(rows):
        lw = lw_ref[j, at, :]
        cum = _cumsum_rows(lw)
        cend = cum[CHUNK - 1:CHUNK, :]
        pinv = jnp.exp(-cum)
        dend = jnp.exp(cend - cum)
        kk, bv, k2 = kk_ref[j, at, :].astype(F32), b_ref[j, at, :].astype(F32), k_ref[j, at, :].astype(F32)
        a_t.append((-kk * jnp.exp(cum - lw)).astype(BF16))
        b_t.append((bv * pinv).astype(BF16))
        k_t.append((k2 * pinv).astype(BF16))
        r_f.append(r_ref[j, at, :].astype(F32) * jnp.exp(cum))
        r_t.append(r_f[j].astype(BF16))
        b_h.append((bv * dend).astype(BF16))
        k_h.append((k2 * dend).astype(BF16))
        v_b.append(v_ref[j, at, :])
        pend.append(jnp.exp(cend))

    P2 = 2 * HEAD_DIM
    ri = lax.broadcasted_iota(jnp.int32, (CHUNK, P2), 0)
    ci = lax.broadcasted_iota(jnp.int32, (CHUNK, P2), 1) % CHUNK
    strict = ri > ci
    incl = ri >= ci
    low = lax.broadcasted_iota(jnp.int32, (CHUNK, P2), 1) < HEAD_DIM
    own = ((lax.broadcasted_iota(jnp.int32, (P2, P2), 0) < HEAD_DIM)
           == (lax.broadcasted_iota(jnp.int32, (P2, P2), 1) < HEAD_DIM))
    units = [(j, slice(p * P2, (p + 1) * P2)) for j in range(rows) for p in range(HEADS // 2)]
    un = range(len(units))
    top, bot = slice(0, CHUNK), slice(CHUNK, 2 * CHUNK)
    left, right = slice(0, P2), slice(P2, 2 * P2)

    sc = [_dg(jnp.concatenate([a_t[j][:, ln], r_t[j][:, ln]], axis=0),
              jnp.concatenate([_block_diag(b_t[j][:, ln]), _block_diag(k_t[j][:, ln])], axis=0), _NT)
          for j, ln in units]
    m = [jnp.where(strict, sc[u][top, left], 0.0) for u in un]
    m_ak = [jnp.where(strict, sc[u][top, right], 0.0).astype(BF16) for u in un]
    l_rb = [jnp.where(incl, sc[u][bot, left], 0.0).astype(BF16) for u in un]
    l_rk = [jnp.where(incl, sc[u][bot, right], 0.0).astype(BF16) for u in un]
    v_bd = [_block_diag(v_b[j][:, ln]) for j, ln in units]

    u0 = [_dg(m_ak[u], v_bd[u], _NN).astype(BF16) for u in un]
    t = [jnp.where(ri == ci, 1.0, m[u]) for u in un]
    mb = [m[u].astype(BF16) for u in un]
    mb = [_dg(mb[u], _block_diag(mb[u]), _NN).astype(BF16) for u in un]
    for i in range(1, 6):
        if i < 5:
            tm = [_dg(jnp.concatenate([t[u].astype(BF16), mb[u]], axis=0), _block_diag(mb[u]), _NN) for u in un]
            t = [t[u] + tm[u][top, :] for u in un]
            mb = [tm[u][bot, :].astype(BF16) for u in un]
        else:
            t = [t[u] + _dg(t[u].astype(BF16), _block_diag(mb[u]), _NN) for u in un]
    xb = [_dg(t[u].astype(BF16),
              jnp.concatenate([_block_diag(a_t[j][:, ln]), _block_diag(u0[u])], axis=1), _NN).astype(BF16)
          for u, (j, ln) in enumerate(units)]
    x_bd = [jnp.concatenate([_block_diag(xb[u][:, left]), _block_diag(xb[u][:, right])], axis=1) for u in un]

    s = [s_ref[u] for u in un]
    sb = [s[u].astype(BF16) for u in un]
    lx = [_dg(l_rb[u], x_bd[u], _NN) for u in un]
    lv = [_dg(l_rk[u], v_bd[u], _NN) for u in un]
    r2 = [(r_f[j][:, ln] + lx[u][:, left]).astype(BF16) for u, (j, ln) in enumerate(units)]
    y = [_dg(r2[u], sb[u], _NT) + lx[u][:, right] + lv[u] for u in un]

    q = [jnp.where(own, _dg(xb[u][:, left], b_h[j][:, ln], _TN), 0.0).astype(BF16)
         for u, (j, ln) in enumerate(units)]
    uv = [_dg(jnp.concatenate([xb[u][:, right], v_b[j][:, ln]], axis=0),
              jnp.concatenate([b_h[j][:, ln], k_h[j][:, ln]], axis=0), _TN) for u, (j, ln) in enumerate(units)]
    sq = [_dg(sb[u], q[u], _NN) for u in un]
    for u, (j, ln) in enumerate(units):
        s_ref[u] = s[u] * pend[j][:, ln] + sq[u] + jnp.where(own, uv[u], 0.0)

    bonus = [bonus_ref[j, at, :].astype(F32) for j in range(rows)]
    gate = [gate_ref[j, at, :].astype(F32) for j in range(rows)]
    for u, (j, ln) in enumerate(units):
        yc = y[u] - _pair_mean(y[u], low)
        yn = yc * lax.rsqrt(_pair_mean(yc * yc, low) + GN_EPS) * ln_ref[0:1, ln] + ln_ref[1:2, ln]
        y_o[j, at, ln] = ((yn + bonus[j][:, ln]) * gate[j][:, ln]).astype(y_o.dtype)


def _attn_prep_kernel(x_ref, g_ref, w_ref, qk_ref, ones_ref, q_o, k_o, v_o, gate_o, *, pad_tiles):
    i = pl.program_id(1)

    @pl.when(i < pad_tiles)
    def _():
        k_o[...] = jnp.zeros_like(k_o)
        v_o[...] = jnp.zeros_like(v_o)

    @pl.when(i >= pad_tiles)
    def _():
        zb = _norm_proj(x_ref[...], g_ref, w_ref)
        ones_bd = ones_ref[...]
        q = zb[:, 0:WIDTH]
        k = zb[:, WIDTH:2 * WIDTH]
        gate = zb[:, 3 * WIDTH:4 * WIDTH]
        qn = q * lax.rsqrt(_head_sum(q * q, ones_bd) * (1.0 / HEAD_DIM) + RMS_EPS) * qk_ref[0:1, :]
        kn = k * lax.rsqrt(_head_sum(k * k, ones_bd) * (1.0 / HEAD_DIM) + RMS_EPS) * qk_ref[1:2, :]
        q_o[...] = (qn * HEAD_DIM ** -0.5).astype(q_o.dtype)
        k_o[...] = kn.T.astype(k_o.dtype)
        v_o[...] = zb[:, 2 * WIDTH:3 * WIDTH].astype(v_o.dtype)
        gate_o[...] = (gate * _sigmoid(gate)).astype(gate_o.dtype)


def _attn_band_kernel(q_ref, *refs):
    n_chunks = q_ref.shape[0] // CHUNK
    first = pl.program_id(1) * n_chunks
    _chunk_loop(n_chunks, ATTN_Q_CHUNKS, functools.partial(_attn_group, first, q_ref, *refs))


def _attn_group(first, q_ref, k_ref, v_ref, bias_ref, gate_ref, y_o, off):
    nq = ATTN_Q_CHUNKS * CHUNK
    nk = (ATTN_Q_CHUNKS + PAST_CHUNKS) * CHUNK
    at = pl.ds(off, nq)
    n = first + off // CHUNK
    start = pl.multiple_of(first * CHUNK + off, CHUNK)
    q = q_ref[at, :]
    kt = k_ref[:, pl.ds(pl.multiple_of(start, 2 * HEAD_DIM), nk)]
    vb = v_ref[pl.ds(start, nk), :]
    qc = lax.broadcasted_iota(jnp.int32, (nq, nk), 0) // CHUNK
    kc = lax.broadcasted_iota(jnp.int32, (nq, nk), 1) // CHUNK
    valid = (kc >= qc) & (kc <= qc + PAST_CHUNKS) & (kc >= PAST_CHUNKS - n)
    gate = gate_ref[at, :].astype(F32)
    low_q = lax.broadcasted_iota(jnp.int32, (nq, 2 * HEAD_DIM), 1) < HEAD_DIM
    for pair in range(HEADS // 2):
        lanes = slice(pair * 2 * HEAD_DIM, (pair + 1) * 2 * HEAD_DIM)
        v_bd = _block_diag(vb[:, lanes])
        s2 = _dg(q[:, lanes], _block_diag_cols(kt[lanes, :]), _NN)
        p, inv = [], []
        for i in range(2):
            s = jnp.where(valid, s2[:, i * nk:(i + 1) * nk] + bias_ref[2 * pair + i], -jnp.inf)
            e = jnp.exp(s - jnp.max(s, axis=-1, keepdims=True))
            p.append(e.astype(BF16))
            inv.append(1.0 / jnp.sum(e, axis=-1, keepdims=True))
        o = _dg(jnp.concatenate(p, axis=1), v_bd, _NN)
        y_o[at, lanes] = (o * jnp.where(low_q, inv[0], inv[1]) * gate[:, lanes]).astype(y_o.dtype)


def _hgrn_prep_kernel(x_ref, g_ref, w_ref, lb_ref, q_o, k_o, lf_o, v_o, gate_o, *, layer):
    lbp = lb_ref[...]
    e = jnp.exp(lbp - jnp.max(lbp, axis=0, keepdims=True))
    sm = e / jnp.sum(e, axis=0, keepdims=True)
    lb = jnp.sum(sm[0:layer + 1], axis=0, keepdims=True) - sm[0:1]
    zc = _norm_proj(x_ref[...], g_ref, w_ref)
    q = zc[:, 0:WIDTH]
    gate = zc[:, 3 * WIDTH:4 * WIDTH]
    fg = lb + (1.0 - lb) * _sigmoid(zc[:, WIDTH:2 * WIDTH])
    q_o[...] = (q * _sigmoid(q)).astype(q_o.dtype)
    k_o[...] = (1.0 - fg).astype(k_o.dtype)
    lf_o[...] = jnp.log(fg)
    v_o[...] = zc[:, 2 * WIDTH:3 * WIDTH].astype(v_o.dtype)
    gate_o[...] = (gate * _sigmoid(gate)).astype(gate_o.dtype)


def _hgrn_scan_kernel(q_ref, *refs):
    s_ref = refs[-1]

    @pl.when(pl.program_id(1) == 0)
    def _():
        s_ref[...] = jnp.zeros_like(s_ref)

    _chunk_loop(q_ref.shape[1] // CHUNK, 1, functools.partial(_hgrn_chunk, q_ref, *refs))


def _hgrn_chunk(q_ref, k_ref, lf_ref, v_ref, gate_ref, sel_ref, ones_ref, ng_ref, y_o, s_ref, off):
    rows = q_ref.shape[0]
    at = pl.ds(off, CHUNK)
    n_lv = len(LEVELS)
    row = lax.broadcasted_iota(jnp.int32, (CHUNK, 1), 0)
    second = [((row // w) % 2) == 1 for w in LEVELS]
    sign = [jnp.where(sec, 1.0, -1.0) for sec in second]

    z_lv, qk_diag, q_in, k_out, dec, v_f, v_b = ([] for _ in range(7))
    for j in range(rows):
        q, k, v_b16 = q_ref[j, at, :].astype(F32), k_ref[j, at, :].astype(F32), v_ref[j, at, :]
        b = _cumsum_rows(lf_ref[j, at, :])
        bend = b[CHUNK - 1:CHUNK, :]
        b1 = b.astype(BF16)
        rem = b - b1.astype(F32)
        b2 = rem.astype(BF16)
        b3 = (rem - b2.astype(F32)).astype(BF16)
        edges = _dg(sel_ref[...], jnp.concatenate([b1, b2, b3], axis=0), _NN)
        zl = []
        for i in range(n_lv):
            edge = edges[i * CHUNK:(i + 1) * CHUNK, :]
            e = jnp.exp((b - edge) * sign[i])
            z = jnp.where(second[i], q, k) * e
            zl.append((z.astype(BF16), z.T.astype(BF16)))
        z_lv.append(zl)
        qk_diag.append(_head_sum(q * k, ones_ref[...]))
        q_in.append((q * jnp.exp(b)).astype(BF16))
        k_out.append((k * jnp.exp(bend - b)).astype(BF16))
        dec.append(jnp.exp(bend))
        v_f.append(v_b16.astype(F32))
        v_b.append(v_b16)

    units = [(j, slice(p * 2 * HEAD_DIM, (p + 1) * 2 * HEAD_DIM)) for j in range(rows) for p in range(HEADS // 2)]
    un = range(len(units))
    low = lax.broadcasted_iota(jnp.int32, (CHUNK, 2 * HEAD_DIM), 1) < HEAD_DIM
    ri2 = lax.broadcasted_iota(jnp.int32, (CHUNK, 2 * HEAD_DIM), 0)
    ci2 = lax.broadcasted_iota(jnp.int32, (CHUNK, 2 * HEAD_DIM), 1) % CHUNK
    keep = [((ri2 // (2 * w)) == (ci2 // (2 * w))) & ((ri2 // w) % 2 == 1) & ((ci2 // w) % 2 == 0) for w in LEVELS]
    v_bd = [_block_diag(v_b[j][:, ln]) for j, ln in units]
    att = []
    for j, ln in units:
        pr = [_dg(z[:, ln], _block_diag_cols(zt[ln, :]), _NN) for z, zt in z_lv[j]]
        att.append(sum(jnp.where(keep[i], pr[i], 0.0) for i in range(n_lv)))
    s = [s_ref[u] for u in un]
    o_in = [_dg(q_in[j][:, ln], s[u].astype(BF16), _NT) for u, (j, ln) in enumerate(units)]
    o_at = [_dg(att[u].astype(BF16), v_bd[u], _NN) for u in un]
    vk = [_dg(v_b[j][:, ln], k_out[j][:, ln], _TN) for j, ln in units]
    own = ((lax.broadcasted_iota(jnp.int32, (2 * HEAD_DIM, 2 * HEAD_DIM), 0) < HEAD_DIM)
           == (lax.broadcasted_iota(jnp.int32, (2 * HEAD_DIM, 2 * HEAD_DIM), 1) < HEAD_DIM))
    for u, (j, ln) in enumerate(units):
        s_ref[u] = s[u] * dec[j][:, ln] + jnp.where(own, vk[u], 0.0)
    gate = [gate_ref[j, at, :].astype(F32) for j in range(rows)]
    for u, (j, ln) in enumerate(units):
        o = o_at[u] + o_in[u] + qk_diag[j][:, ln] * v_f[j][:, ln]
        o = o * lax.rsqrt(_pair_mean(o * o, low) + RMS_EPS) * ng_ref[:, ln]
        y_o[j, at, ln] = (o * gate[j][:, ln]).astype(y_o.dtype)


def _merge_kernel(x_ref, g_ref, wg_ref, ya_ref, yb_ref, yc_ref, pa_ref, pb_ref, pc_ref, wo_ref, o_ref):
    gates = _sigmoid(_norm_proj(x_ref[...], g_ref, wg_ref))
    m = (gates[:, 0:D_MODEL] * jnp.dot(ya_ref[...], pa_ref[...], preferred_element_type=F32)
         + gates[:, D_MODEL:2 * D_MODEL] * jnp.dot(yb_ref[...], pb_ref[...], preferred_element_type=F32)
         + gates[:, 2 * D_MODEL:3 * D_MODEL] * jnp.dot(yc_ref[...], pc_ref[...], preferred_element_type=F32))
    o_ref[...] = x_ref[...] + jnp.dot(m.astype(BF16), wo_ref[...], preferred_element_type=F32)


def _params(*sem):
    return pltpu.CompilerParams(dimension_semantics=sem, vmem_limit_bytes=VMEM_LIMIT)


def _full(shape):
    return pl.BlockSpec(shape, lambda *_: (0,) * len(shape))


def _edge_selectors():
    sel = np.zeros((len(LEVELS) * CHUNK, CHUNK), np.float32)
    for i, w in enumerate(LEVELS):
        for t in range(CHUNK):
            sel[i * CHUNK + t, (t // (2 * w)) * 2 * w + w - 1] = 1.0
    return jnp.asarray(np.concatenate([sel, sel, sel], axis=1), BF16)


def _band_bias(rel_bias):
    nq = ATTN_Q_CHUNKS * CHUNK
    nk = (ATTN_Q_CHUNKS + PAST_CHUNKS) * CHUNK
    period = nk + nq
    j = np.arange(period)
    far = REL_CLIP + CHUNK - 1
    w = rel_bias.astype(F32)[:, np.where(j < nk, np.clip(PAD_ROWS - j, -(CHUNK - 1), REL_CLIP) + CHUNK - 1, far)]
    heads = w.shape[0]
    return jnp.tile(w, (1, nq))[:, :nq * (period - 1)].reshape(heads, nq, period - 1)[:, :, :nk]


def _layer(x, l, p, consts):
    bsz, seq, _ = x.shape
    tm = min(TOKEN_TILE, seq)
    n_t = seq // tm
    n_c = seq // CHUNK
    ones_bd, sel = consts
    g = p['norm_g'][l][None, :]

    tok = lambda w: pl.BlockSpec((None, tm, w), lambda b, t: (b, t, 0))
    rows_a, rows_c = min(RWKV_ROWS, bsz), min(HGRN_ROWS, bsz)
    blk = min(SCAN_CHUNKS, n_c) * CHUNK
    qblk = min(ATTN_CHUNKS, n_c) * CHUNK
    chk = lambda w, rows: pl.BlockSpec((rows, blk, w), lambda b, c: (b, c, 0))
    qck = lambda w: pl.BlockSpec((None, qblk, w), lambda b, c: (b, c, 0))
    act = lambda dt: jax.ShapeDtypeStruct((bsz, seq, WIDTH), dt)

    w_in = p['w_in'][l]
    wa = w_in[:, :A_COLS]
    cols = np.r_[0:WIDTH, WIDTH + LORA:2 * WIDTH + LORA, 2 * WIDTH + LORA:3 * WIDTH + LORA,
                 3 * WIDTH + 2 * LORA:4 * WIDTH + 2 * LORA, WIDTH:WIDTH + LORA,
                 3 * WIDTH + LORA:3 * WIDTH + 2 * LORA]
    pa = jnp.stack([p['rwkv_w0'][l], p['rwkv_a0'][l], p['rwkv_k_k'][l], p['rwkv_k_a'][l],
                    p['rwkv_r_k'][l].reshape(WIDTH)] + [jnp.zeros((WIDTH,), F32)] * 3)
    r, lw, k2, v, kk, bv, bonus, gate_a = pl.pallas_call(
        _rwkv_prep_kernel,
        grid=(bsz, n_t),
        in_specs=[tok(D_MODEL), _full((1, D_MODEL)), _full((D_MODEL, A_COLS)), _full((1, A_COLS)),
                  _full((8, WIDTH)), _full((LORA, WIDTH)), _full((LORA, WIDTH)), _full((WIDTH, WIDTH))],
        out_specs=[tok(WIDTH)] * 8,
        out_shape=[act(BF16), act(F32)] + [act(BF16)] * 6,
        scratch_shapes=[pltpu.VMEM((8, A_COLS), F32)],
        compiler_params=_params("parallel", "arbitrary"),
        name="rwkv_prep",
    )(x, g, wa[:, cols].astype(BF16), p['rwkv_mu'][l][cols][None, :], pa,
      p['rwkv_w2'][l], p['rwkv_a2'][l], ones_bd)

    ln = jnp.stack([p['rwkv_ln_w'][l], p['rwkv_ln_b'][l]] + [jnp.zeros((WIDTH,), F32)] * 6)
    y_a = pl.pallas_call(
        _rwkv_scan_kernel,
        grid=(bsz // rows_a, seq // blk),
        in_specs=[chk(WIDTH, rows_a)] * 8 + [_full((8, WIDTH))],
        out_specs=chk(WIDTH, rows_a),
        out_shape=act(BF16),
        scratch_shapes=[pltpu.VMEM((rows_a * HEADS // 2, 2 * HEAD_DIM, 2 * HEAD_DIM), F32)],
        compiler_params=_params("parallel", "arbitrary"),
        name="rwkv_scan",
    )(r, lw, k2, v, kk, bv, bonus, gate_a, ln)

    pad_tiles = PAD_ROWS // tm
    wb = w_in[:, A_COLS:A_COLS + 4 * WIDTH].astype(BF16)
    qk = jnp.stack([jnp.tile(p['attn_q_norm'][l], HEADS), jnp.tile(p['attn_k_norm'][l], HEADS)]
                   + [jnp.zeros((WIDTH,), F32)] * 6)
    shift = lambda b, i: (b, jnp.maximum(i - pad_tiles, 0), 0)
    q, k_pad, v_pad, gate_b = pl.pallas_call(
        functools.partial(_attn_prep_kernel, pad_tiles=pad_tiles),
        grid=(bsz, n_t + pad_tiles),
        in_specs=[pl.BlockSpec((None, tm, D_MODEL), shift), _full((1, D_MODEL)),
                  _full((D_MODEL, 4 * WIDTH)), _full((8, WIDTH)), _full((WIDTH, WIDTH))],
        out_specs=[pl.BlockSpec((None, tm, WIDTH), shift),
                   pl.BlockSpec((None, WIDTH, tm), lambda b, i: (b, 0, i)), tok(WIDTH),
                   pl.BlockSpec((None, tm, WIDTH), shift)],
        out_shape=[act(BF16), jax.ShapeDtypeStruct((bsz, WIDTH, seq + PAD_ROWS), BF16),
                   jax.ShapeDtypeStruct((bsz, seq + PAD_ROWS, WIDTH), BF16), act(BF16)],
        compiler_params=_params("parallel", "arbitrary"),
        name="attn_prep",
    )(x, g, wb, qk, ones_bd)

    band = pl.BlockSpec((None, seq + PAD_ROWS, WIDTH), lambda b, c: (b, 0, 0))
    band_t = pl.BlockSpec((None, WIDTH, seq + PAD_ROWS), lambda b, c: (b, 0, 0))
    bias = _band_bias(p['attn_rel_bias'][l])
    y_b = pl.pallas_call(
        _attn_band_kernel,
        grid=(bsz, seq // qblk),
        in_specs=[qck(WIDTH), band_t, band,
                  _full(bias.shape), qck(WIDTH)],
        out_specs=qck(WIDTH),
        out_shape=act(BF16),
        compiler_params=_params("parallel", "arbitrary"),
        name="attn_band",
    )(q, k_pad, v_pad, bias, gate_b)

    wc = w_in[:, A_COLS + 4 * WIDTH:A_COLS + 8 * WIDTH].astype(BF16)
    n_layers = p['hgrn_lb'].shape[0]
    qc, kc, lf, vc, gate_c = pl.pallas_call(
        functools.partial(_hgrn_prep_kernel, layer=l),
        grid=(bsz, n_t),
        in_specs=[tok(D_MODEL), _full((1, D_MODEL)), _full((D_MODEL, 4 * WIDTH)), _full((n_layers, WIDTH))],
        out_specs=[tok(WIDTH)] * 5,
        out_shape=[act(BF16), act(BF16), act(F32), act(BF16), act(BF16)],
        compiler_params=_params("parallel", "parallel"),
        name="hgrn_prep",
    )(x, g, wc, p['hgrn_lb'])

    y_c = pl.pallas_call(
        _hgrn_scan_kernel,
        grid=(bsz // rows_c, seq // blk),
        in_specs=[chk(WIDTH, rows_c)] * 5 + [_full(sel.shape), _full((WIDTH, WIDTH)), _full((1, WIDTH))],
        out_specs=chk(WIDTH, rows_c),
        out_shape=act(BF16),
        scratch_shapes=[pltpu.VMEM((rows_c * HEADS // 2, 2 * HEAD_DIM, 2 * HEAD_DIM), F32)],
        compiler_params=_params("parallel", "arbitrary"),
        name="hgrn_scan",
    )(qc, kc, lf, vc, gate_c, sel, ones_bd, jnp.tile(p['hgrn_norm'][l], HEADS)[None, :])

    wg = w_in[:, A_COLS + 8 * WIDTH:].astype(BF16)
    return pl.pallas_call(
        _merge_kernel,
        grid=(bsz, n_t),
        in_specs=[tok(D_MODEL), _full((1, D_MODEL)), _full((D_MODEL, 3 * D_MODEL)),
                  tok(WIDTH), tok(WIDTH), tok(WIDTH),
                  _full((WIDTH, D_MODEL)), _full((WIDTH, D_MODEL)), _full((WIDTH, D_MODEL)),
                  _full((D_MODEL, D_MODEL))],
        out_specs=tok(D_MODEL),
        out_shape=jax.ShapeDtypeStruct(x.shape, x.dtype),
        compiler_params=_params("parallel", "parallel"),
        name="merge",
    )(x, g, wg, y_a, y_b, y_c, p['proj_a'][l].astype(BF16), p['proj_b'][l].astype(BF16),
      p['proj_c'][l].astype(BF16), p['w_out'][l].astype(BF16))


def kernel(x, norm_g, w_in, rwkv_mu, rwkv_w0, rwkv_w2, rwkv_a0, rwkv_a2, rwkv_k_k, rwkv_k_a, rwkv_r_k,
           rwkv_ln_w, rwkv_ln_b, attn_q_norm, attn_k_norm, attn_rel_bias, hgrn_lb, hgrn_norm,
           proj_a, proj_b, proj_c, w_out):
    p = dict(norm_g=norm_g, w_in=w_in, rwkv_mu=rwkv_mu, rwkv_w0=rwkv_w0, rwkv_w2=rwkv_w2,
             rwkv_a0=rwkv_a0, rwkv_a2=rwkv_a2, rwkv_k_k=rwkv_k_k, rwkv_k_a=rwkv_k_a, rwkv_r_k=rwkv_r_k,
             rwkv_ln_w=rwkv_ln_w, rwkv_ln_b=rwkv_ln_b, attn_q_norm=attn_q_norm, attn_k_norm=attn_k_norm,
             attn_rel_bias=attn_rel_bias, hgrn_lb=hgrn_lb, hgrn_norm=hgrn_norm,
             proj_a=proj_a, proj_b=proj_b, proj_c=proj_c, w_out=w_out)
    head = np.arange(WIDTH) // HEAD_DIM
    ones_bd = jnp.asarray(head[:, None] == head[None, :], BF16)
    consts = (ones_bd, _edge_selectors())
    for l in range(norm_g.shape[0]):
        x = _layer(x, l, p, consts)
    return x
```

```python
import functools

import numpy as np
import jax
import jax.numpy as jnp
from jax import lax
from jax.experimental import pallas as pl
from jax.experimental.pallas import tpu as pltpu

F32 = jnp.float32
BF16 = jnp.bfloat16

D_MODEL = 1024
CHUNK = 64
HEAD_DIM = 64
HEADS = 8
WIDTH = HEADS * HEAD_DIM
LORA = 64
A_COLS = 4 * WIDTH + 2 * LORA
PAST_CHUNKS = 8
PAD_ROWS = PAST_CHUNKS * CHUNK
REL_CLIP = 256
RMS_EPS = 1e-6
GN_EPS = 64e-5
LEVELS = (32, 16, 8, 4, 2, 1)

TOKEN_TILE = 512
RWKV_TOKEN_TILE = 1024
PREP_SUB_ROWS = 256
RWKV_ROWS = 8
HGRN_ROWS = 8
SCAN_CHUNKS = 4
ATTN_CHUNKS = 8
ATTN_Q_CHUNKS = 4
VMEM_LIMIT = 56 * 1024 * 1024

_NN = (((1,), (0,)), ((), ()))
_NT = (((1,), (1,)), ((), ()))
_TN = (((0,), (0,)), ((), ()))


def _dg(a, b, dims):
    return lax.dot_general(a, b, dims, preferred_element_type=F32)


def _hi_lo(a):
    hi = a.astype(BF16)
    lo = (a - hi.astype(F32)).astype(BF16)
    return hi, lo


def _mm3(a, b, dims=_NN):
    a1, a2 = _hi_lo(a)
    b1, b2 = _hi_lo(b)
    return _dg(a1, b1, dims) + (_dg(a1, b2, dims) + _dg(a2, b1, dims))


def _head_sum(a, ones_bd):
    return _dg(a.astype(BF16), ones_bd, _NN)


def _block_diag(x):
    low = lax.broadcasted_iota(jnp.int32, x.shape, 1) < HEAD_DIM
    zero = jnp.zeros_like(x)
    return jnp.concatenate([jnp.where(low, x, zero), jnp.where(low, zero, x)], axis=0)


def _block_diag_cols(xt):
    top = lax.broadcasted_iota(jnp.int32, xt.shape, 0) < HEAD_DIM
    zero = jnp.zeros_like(xt)
    return jnp.concatenate([jnp.where(top, xt, zero), jnp.where(top, zero, xt)], axis=1)


def _pair_mean(a, low):
    zero = jnp.zeros_like(a)
    lo_sum = jnp.sum(jnp.where(low, a, zero), axis=-1, keepdims=True)
    hi_sum = jnp.sum(jnp.where(low, zero, a), axis=-1, keepdims=True)
    return jnp.where(low, lo_sum, hi_sum) * (1.0 / HEAD_DIM)


def _sigmoid(x):
    return 0.5 * jnp.tanh(0.5 * x) + 0.5


def _cumsum_rows(x):
    n = x.shape[0]
    rows = lax.broadcasted_iota(jnp.int32, x.shape, 0)
    s = 1
    while s < n:
        x = x + jnp.where(rows >= s, pltpu.roll(x, s, 0), 0.0)
        s *= 2
    return x


def _chunk_loop(n_chunks, step, body):
    def it(i, carry):
        body(pl.multiple_of(i * (step * CHUNK), step * CHUNK))
        return carry
    lax.fori_loop(0, n_chunks // step, it, 0)


def _sub_blocks(n_rows):
    sub = min(PREP_SUB_ROWS, n_rows)
    return [slice(i * sub, (i + 1) * sub) for i in range(n_rows // sub)]


def _norm_proj(x, g_ref, w_ref):
    ms = jnp.mean(x * x, axis=-1, keepdims=True)
    h = x * lax.rsqrt(ms + RMS_EPS) * g_ref[...]
    return jnp.dot(h.astype(BF16), w_ref[...], preferred_element_type=F32)


def _rwkv_prep_kernel(x_ref, g_ref, w_ref, mu_ref, pa_ref, w2_ref, a2_ref, ones_ref,
                      r_o, lw_o, k_o, v_o, kk_o, b_o, bonus_o, gate_o, carry_ref):
    @pl.when(pl.program_id(1) == 0)
    def _():
        carry_ref[...] = jnp.zeros_like(carry_ref)

    w0, a0, k_k, k_a, r_k = (pa_ref[i:i + 1, :] for i in range(5))
    ones_bd = ones_ref[...]
    last = carry_ref[7:8, :]
    blocks = _sub_blocks(x_ref.shape[0])
    proj = [_norm_proj(x_ref[rs, :], g_ref, w_ref) for rs in blocks]
    for rs, za in zip(blocks, proj):
        sub = za.shape[0]
        row = lax.broadcasted_iota(jnp.int32, (sub, 1), 0)
        prev = jnp.where(row == 0, last, pltpu.roll(za, 1, 0))
        last = za[sub - 1:sub, :]
        tail = za[sub - 8:, :]
        xs = za + (prev - za) * mu_ref[...]

        r = xs[:, 0:WIDTH]
        k = xs[:, WIDTH:2 * WIDTH]
        v = xs[:, 2 * WIDTH:3 * WIDTH]
        gate = xs[:, 3 * WIDTH:4 * WIDTH]
        wl = xs[:, 4 * WIDTH:4 * WIDTH + LORA]
        al = xs[:, 4 * WIDTH + LORA:4 * WIDTH + 2 * LORA]

        u = -(w0 + _mm3(jnp.tanh(wl), w2_ref[...]))
        softplus = jnp.maximum(u, 0.0) + jnp.log(1.0 + jnp.exp(-jnp.abs(u)))
        lw = -jnp.exp(-softplus - 0.5)
        a = _sigmoid(a0 + _mm3(al, a2_ref[...]))

        kkp = k * k_k
        kk = kkp * lax.rsqrt(jnp.maximum(_head_sum(kkp * kkp, ones_bd), 1e-24))
        k2 = k * (1.0 + (a - 1.0) * k_a)

        r_o[rs, :] = r.astype(r_o.dtype)
        lw_o[rs, :] = lw
        k_o[rs, :] = k2.astype(k_o.dtype)
        v_o[rs, :] = v.astype(v_o.dtype)
        kk_o[rs, :] = kk.astype(kk_o.dtype)
        b_o[rs, :] = (kk * a).astype(b_o.dtype)
        bonus_o[rs, :] = (_head_sum(r * k2 * r_k, ones_bd) * v).astype(bonus_o.dtype)
        gate_o[rs, :] = (gate * _sigmoid(gate)).astype(gate_o.dtype)
    carry_ref[...] = tail


def _rwkv_scan_kernel(r_ref, *refs):
    s_ref = refs[-1]

    @pl.when(pl.program_id(1) == 0)
    def _():
        s_ref[...] = jnp.zeros_like(s_ref)

    _chunk_loop(r_ref.shape[1] // CHUNK, 1, functools.partial(_rwkv_chunk, r_ref, *refs))


def _rwkv_chunk(r_ref, lw_ref, k_ref, v_ref, kk_ref, b_ref, bonus_ref, gate_ref, ln_ref,
                y_o, s_ref, off):
    rows = r_ref.shape[0]
    at = pl.ds(off, CHUNK)
    a_t, b_t, k_t, r_f, r_t, b_h, k_h, v_b, pend = ([] for _ in range(9))
    for j in range(rows):
        lw = lw_ref[j, at, :]
        cum = _cumsum_rows(lw)
        cend = cum[CHUNK - 1:CHUNK, :]
        pinv = jnp.exp(-cum)
        dend = jnp.exp(cend - cum)
        kk, bv, k2 = kk_ref[j, at, :].astype(F32), b_ref[j, at, :].astype(F32), k_ref[j, at, :].astype(F32)
        a_t.append((-kk * jnp.exp(cum - lw)).astype(BF16))
        b_t.append((bv * pinv).astype(BF16))
        k_t.append((k2 * pinv).astype(BF16))
        r_f.append(r_ref[j, at, :].astype(F32) * jnp.exp(cum))
        r_t.append(r_f[j].astype(BF16))
        b_h.append((bv * dend).astype(BF16))
        k_h.append((k2 * dend).astype(BF16))
        v_b.append(v_ref[j, at, :])
        pend.append(jnp.exp(cend))

    P2 = 2 * HEAD_DIM
    ri = lax.broadcasted_iota(jnp.int32, (CHUNK, P2), 0)
    ci = lax.broadcasted_iota(jnp.int32, (CHUNK, P2), 1) % CHUNK
    strict = ri > ci
    incl = ri >= ci
    low = lax.broadcasted_iota(jnp.int32, (CHUNK, P2), 1) < HEAD_DIM
    own = ((lax.broadcasted_iota(jnp.int32, (P2, P2), 0) < HEAD_DIM)
           == (lax.broadcasted_iota(jnp.int32, (P2, P2), 1) < HEAD_DIM))
    units = [(j, slice(p * P2, (p + 1) * P2)) for j in range(rows) for p in range(HEADS // 2)]
    un = range(len(units))
    top, bot = slice(0, CHUNK), slice(CHUNK, 2 * CHUNK)
    left, right = slice(0, P2), slice(P2, 2 * P2)

    sc = [_dg(jnp.concatenate([a_t[j][:, ln], r_t[j][:, ln]], axis=0),
              jnp.concatenate([_block_diag(b_t[j][:, ln]), _block_diag(k_t[j][:, ln])], axis=0), _NT)
          for j, ln in units]
    m = [jnp.where(strict, sc[u][top, left], 0.0) for u in un]
    m_ak = [jnp.where(strict, sc[u][top, right], 0.0).astype(BF16) for u in un]
    l_rb = [jnp.where(incl, sc[u][bot, left], 0.0).astype(BF16) for u in un]
    l_rk = [jnp.where(incl, sc[u][bot, right], 0.0).astype(BF16) for u in un]
    v_bd = [_block_diag(v_b[j][:, ln]) for j, ln in units]

    u0 = [_dg(m_ak[u], v_bd[u], _NN).astype(BF16) for u in un]
    t = [jnp.where(ri == ci, 1.0, m[u]) for u in un]
    mb = [m[u].astype(BF16) for u in un]
    mb = [_dg(mb[u], _block_diag(mb[u]), _NN).astype(BF16) for u in un]
    for i in range(1, 6):
        if i < 5:
            tm = [_dg(jnp.concatenate([t[u].astype(BF16), mb[u]], axis=0), _block_diag(mb[u]), _NN) for u in un]
            t = [t[u] + tm[u][top, :] for u in un]
            mb = [tm[u][bot, :].astype(BF16) for u in un]
        else:
            t = [t[u] + _dg(t[u].astype(BF16), _block_diag(mb[u]), _NN) for u in un]
    xb = [_dg(t[u].astype(BF16),
              jnp.concatenate([_block_diag(a_t[j][:, ln]), _block_diag(u0[u])], axis=1), _NN).astype(BF16)
          for u, (j, ln) in enumerate(units)]
    x_bd = [jnp.concatenate([_block_diag(xb[u][:, left]), _block_diag(xb[u][:, right])], axis=1) for u in un]

    s = [s_ref[u] for u in un]
    sb = [s[u].astype(BF16) for u in un]
    lx = [_dg(l_rb[u], x_bd[u], _NN) for u in un]
    lv = [_dg(l_rk[u], v_bd[u], _NN) for u in un]
    r2 = [(r_f[j][:, ln] + lx[u][:, left]).astype(BF16) for u, (j, ln) in enumerate(units)]
    y = [_dg(r2[u], sb[u], _NT) + lx[u][:, right] + lv[u] for u in un]

    q = [jnp.where(own, _dg(xb[u][:, left], b_h[j][:, ln], _TN), 0.0).astype(BF16)
         for u, (j, ln) in enumerate(units)]
    uv = [_dg(jnp.concatenate([xb[u][:, right], v_b[j][:, ln]], axis=0),
              jnp.concatenate([b_h[j][:, ln], k_h[j][:, ln]], axis=0), _TN) for u, (j, ln) in enumerate(units)]
    sq = [_dg(sb[u], q[u], _NN) for u in un]
    for u, (j, ln) in enumerate(units):
        s_ref[u] = s[u] * pend[j][:, ln] + sq[u] + jnp.where(own, uv[u], 0.0)

    bonus = [bonus_ref[j, at, :].astype(F32) for j in range(rows)]
    gate = [gate_ref[j, at, :].astype(F32) for j in range(rows)]
    for u, (j, ln) in enumerate(units):
        yc = y[u] - _pair_mean(y[u], low)
        yn = yc * lax.rsqrt(_pair_mean(yc * yc, low) + GN_EPS) * ln_ref[0:1, ln] + ln_ref[1:2, ln]
        y_o[j, at, ln] = ((yn + bonus[j][:, ln]) * gate[j][:, ln]).astype(y_o.dtype)


def _attn_prep_kernel(x_ref, g_ref, w_ref, qk_ref, ones_ref, q_o, k_o, v_o, gate_o, *, pad_tiles):
    i = pl.program_id(1)

    @pl.when(i < pad_tiles)
    def _():
        k_o[...] = jnp.zeros_like(k_o)
        v_o[...] = jnp.zeros_like(v_o)

    @pl.when(i >= pad_tiles)
    def _():
        zb = _norm_proj(x_ref[...], g_ref, w_ref)
        ones_bd = ones_ref[...]
        q = zb[:, 0:WIDTH]
        k = zb[:, WIDTH:2 * WIDTH]
        gate = zb[:, 3 * WIDTH:4 * WIDTH]
        qn = q * lax.rsqrt(_head_sum(q * q, ones_bd) * (1.0 / HEAD_DIM) + RMS_EPS) * qk_ref[0:1, :]
        kn = k * lax.rsqrt(_head_sum(k * k, ones_bd) * (1.0 / HEAD_DIM) + RMS_EPS) * qk_ref[1:2, :]
        q_o[...] = (qn * HEAD_DIM ** -0.5).astype(q_o.dtype)
        k_o[...] = kn.T.astype(k_o.dtype)
        v_o[...] = zb[:, 2 * WIDTH:3 * WIDTH].astype(v_o.dtype)
        gate_o[...] = (gate * _sigmoid(gate)).astype(gate_o.dtype)


def _attn_band_kernel(q_ref, *refs):
    n_chunks = q_ref.shape[0] // CHUNK
    first = pl.program_id(1) * n_chunks
    _chunk_loop(n_chunks, ATTN_Q_CHUNKS, functools.partial(_attn_group, first, q_ref, *refs))


def _attn_group(first, q_ref, k_ref, v_ref, bias_ref, gate_ref, y_o, off):
    nq = ATTN_Q_CHUNKS * CHUNK
    nk = (ATTN_Q_CHUNKS + PAST_CHUNKS) * CHUNK
    at = pl.ds(off, nq)
    n = first + off // CHUNK
    start = pl.multiple_of(first * CHUNK + off, CHUNK)
    q = q_ref[at, :]
    kt = k_ref[:, pl.ds(pl.multiple_of(start, 2 * HEAD_DIM), nk)]
    vb = v_ref[pl.ds(start, nk), :]
    qc = lax.broadcasted_iota(jnp.int32, (nq, nk), 0) // CHUNK
    kc = lax.broadcasted_iota(jnp.int32, (nq, nk), 1) // CHUNK
    valid = (kc >= qc) & (kc <= qc + PAST_CHUNKS) & (kc >= PAST_CHUNKS - n)
    gate = gate_ref[at, :].astype(F32)
    low_q = lax.broadcasted_iota(jnp.int32, (nq, 2 * HEAD_DIM), 1) < HEAD_DIM
    for pair in range(HEADS // 2):
        lanes = slice(pair * 2 * HEAD_DIM, (pair + 1) * 2 * HEAD_DIM)
        v_bd = _block_diag(vb[:, lanes])
        s2 = _dg(q[:, lanes], _block_diag_cols(kt[lanes, :]), _NN)
        p, inv = [], []
        for i in range(2):
            s = jnp.where(valid, s2[:, i * nk:(i + 1) * nk] + bias_ref[2 * pair + i], -jnp.inf)
            e = jnp.exp(s - jnp.max(s, axis=-1, keepdims=True))
            p.append(e.astype(BF16))
            inv.append(1.0 / jnp.sum(e, axis=-1, keepdims=True))
        o = _dg(jnp.concatenate(p, axis=1), v_bd, _NN)
        y_o[at, lanes] = (o * jnp.where(low_q, inv[0], inv[1]) * gate[:, lanes]).astype(y_o.dtype)


def _hgrn_prep_kernel(x_ref, g_ref, w_ref, lb_ref, q_o, k_o, lf_o, v_o, gate_o, *, layer):
    lbp = lb_ref[...]
    e = jnp.exp(lbp - jnp.max(lbp, axis=0, keepdims=True))
    sm = e / jnp.sum(e, axis=0, keepdims=True)
    lb = jnp.sum(sm[0:layer + 1], axis=0, keepdims=True) - sm[0:1]
    zc = _norm_proj(x_ref[...], g_ref, w_ref)
    q = zc[:, 0:WIDTH]
    gate = zc[:, 3 * WIDTH:4 * WIDTH]
    fg = lb + (1.0 - lb) * _sigmoid(zc[:, WIDTH:2 * WIDTH])
    q_o[...] = (q * _sigmoid(q)).astype(q_o.dtype)
    k_o[...] = (1.0 - fg).astype(k_o.dtype)
    lf_o[...] = jnp.log(fg)
    v_o[...] = zc[:, 2 * WIDTH:3 * WIDTH].astype(v_o.dtype)
    gate_o[...] = (gate * _sigmoid(gate)).astype(gate_o.dtype)


def _hgrn_scan_kernel(q_ref, *refs):
    s_ref = refs[-1]

    @pl.when(pl.program_id(1) == 0)
    def _():
        s_ref[...] = jnp.zeros_like(s_ref)

    _chunk_loop(q_ref.shape[1] // CHUNK, 1, functools.partial(_hgrn_chunk, q_ref, *refs))


def _hgrn_chunk(q_ref, k_ref, lf_ref, v_ref, gate_ref, sel_ref, ones_ref, ng_ref, y_o, s_ref, off):
    rows = q_ref.shape[0]
    at = pl.ds(off, CHUNK)
    n_lv = len(LEVELS)
    row = lax.broadcasted_iota(jnp.int32, (CHUNK, 1), 0)
    second = [((row // w) % 2) == 1 for w in LEVELS]
    sign = [jnp.where(sec, 1.0, -1.0) for sec in second]

    z_lv, qk_diag, q_in, k_out, dec, v_f, v_b = ([] for _ in range(7))
    for j in range(rows):
        q, k, v_b16 = q_ref[j, at, :].astype(F32), k_ref[j, at, :].astype(F32), v_ref[j, at, :]
        b = _cumsum_rows(lf_ref[j, at, :])
        bend = b[CHUNK - 1:CHUNK, :]
        b1 = b.astype(BF16)
        rem = b - b1.astype(F32)
        b2 = rem.astype(BF16)
        b3 = (rem - b2.astype(F32)).astype(BF16)
        edges = _dg(sel_ref[...], jnp.concatenate([b1, b2, b3], axis=0), _NN)
        zl = []
        for i in range(n_lv):
            edge = edges[i * CHUNK:(i + 1) * CHUNK, :]
            e = jnp.exp((b - edge) * sign[i])
            z = jnp.where(second[i], q, k) * e
            zl.append((z.astype(BF16), z.T.astype(BF16)))
        z_lv.append(zl)
        qk_diag.append(_head_sum(q * k, ones_ref[...]))
        q_in.append((q * jnp.exp(b)).astype(BF16))
        k_out.append((k * jnp.exp(bend - b)).astype(BF16))
        dec.append(jnp.exp(bend))
        v_f.append(v_b16.astype(F32))
        v_b.append(v_b16)

    units = [(j, slice(p * 2 * HEAD_DIM, (p + 1) * 2 * HEAD_DIM)) for j in range(rows) for p in range(HEADS // 2)]
    un = range(len(units))
    low = lax.broadcasted_iota(jnp.int32, (CHUNK, 2 * HEAD_DIM), 1) < HEAD_DIM
    ri2 = lax.broadcasted_iota(jnp.int32, (CHUNK, 2 * HEAD_DIM), 0)
    ci2 = lax.broadcasted_iota(jnp.int32, (CHUNK, 2 * HEAD_DIM), 1) % CHUNK
    keep = [((ri2 // (2 * w)) == (ci2 // (2 * w))) & ((ri2 // w) % 2 == 1) & ((ci2 // w) % 2 == 0) for w in LEVELS]
    v_bd = [_block_diag(v_b[j][:, ln]) for j, ln in units]
    att = []
    for j, ln in units:
        pr = [_dg(z[:, ln], _block_diag_cols(zt[ln, :]), _NN) for z, zt in z_lv[j]]
        att.append(sum(jnp.where(keep[i], pr[i], 0.0) for i in range(n_lv)))
    s = [s_ref[u] for u in un]
    o_in = [_dg(q_in[j][:, ln], s[u].astype(BF16), _NT) for u, (j, ln) in enumerate(units)]
    o_at = [_dg(att[u].astype(BF16), v_bd[u], _NN) for u in un]
    vk = [_dg(v_b[j][:, ln], k_out[j][:, ln], _TN) for j, ln in units]
    own = ((lax.broadcasted_iota(jnp.int32, (2 * HEAD_DIM, 2 * HEAD_DIM), 0) < HEAD_DIM)
           == (lax.broadcasted_iota(jnp.int32, (2 * HEAD_DIM, 2 * HEAD_DIM), 1) < HEAD_DIM))
    for u, (j, ln) in enumerate(units):
        s_ref[u] = s[u] * dec[j][:, ln] + jnp.where(own, vk[u], 0.0)
    gate = [gate_ref[j, at, :].astype(F32) for j in range(rows)]
    for u, (j, ln) in enumerate(units):
        o = o_at[u] + o_in[u] + qk_diag[j][:, ln] * v_f[j][:, ln]
        o = o * lax.rsqrt(_pair_mean(o * o, low) + RMS_EPS) * ng_ref[:, ln]
        y_o[j, at, ln] = (o * gate[j][:, ln]).astype(y_o.dtype)


def _merge_kernel(x_ref, g_ref, wg_ref, ya_ref, yb_ref, yc_ref, pa_ref, pb_ref, pc_ref, wo_ref, o_ref):
    gates = _sigmoid(_norm_proj(x_ref[...], g_ref, wg_ref))
    m = (gates[:, 0:D_MODEL] * jnp.dot(ya_ref[...], pa_ref[...], preferred_element_type=F32)
         + gates[:, D_MODEL:2 * D_MODEL] * jnp.dot(yb_ref[...], pb_ref[...], preferred_element_type=F32)
         + gates[:, 2 * D_MODEL:3 * D_MODEL] * jnp.dot(yc_ref[...], pc_ref[...], preferred_element_type=F32))
    o_ref[...] = x_ref[...] + jnp.dot(m.astype(BF16), wo_ref[...], preferred_element_type=F32)


def _params(*sem):
    return pltpu.CompilerParams(dimension_semantics=sem, vmem_limit_bytes=VMEM_LIMIT)


def _full(shape):
    return pl.BlockSpec(shape, lambda *_: (0,) * len(shape))


def _edge_selectors():
    sel = np.zeros((len(LEVELS) * CHUNK, CHUNK), np.float32)
    for i, w in enumerate(LEVELS):
        for t in range(CHUNK):
            sel[i * CHUNK + t, (t // (2 * w)) * 2 * w + w - 1] = 1.0
    return jnp.asarray(np.concatenate([sel, sel, sel], axis=1), BF16)


def _band_bias(rel_bias):
    nq = ATTN_Q_CHUNKS * CHUNK
    nk = (ATTN_Q_CHUNKS + PAST_CHUNKS) * CHUNK
    period = nk + nq
    j = np.arange(period)
    far = REL_CLIP + CHUNK - 1
    w = rel_bias.astype(F32)[:, np.where(j < nk, np.clip(PAD_ROWS - j, -(CHUNK - 1), REL_CLIP) + CHUNK - 1, far)]
    heads = w.shape[0]
    return jnp.tile(w, (1, nq))[:, :nq * (period - 1)].reshape(heads, nq, period - 1)[:, :, :nk]


def _layer(x, l, p, consts):
    bsz, seq, _ = x.shape
    tm = min(TOKEN_TILE, seq)
    n_t = seq // tm
    n_c = seq // CHUNK
    ones_bd, sel = consts
    g = p['norm_g'][l][None, :]

    tma = min(RWKV_TOKEN_TILE, seq)
    tok = lambda w: pl.BlockSpec((None, tm, w), lambda b, t: (b, t, 0))
    toka = lambda w: pl.BlockSpec((None, tma, w), lambda b, t: (b, t, 0))
    rows_a, rows_c = min(RWKV_ROWS, bsz), min(HGRN_ROWS, bsz)
    blk = min(SCAN_CHUNKS, n_c) * CHUNK
    qblk = min(ATTN_CHUNKS, n_c) * CHUNK
    chk = lambda w, rows: pl.BlockSpec((rows, blk, w), lambda b, c: (b, c, 0))
    qck = lambda w: pl.BlockSpec((None, qblk, w), lambda b, c: (b, c, 0))
    act = lambda dt: jax.ShapeDtypeStruct((bsz, seq, WIDTH), dt)

    w_in = p['w_in'][l]
    wa = w_in[:, :A_COLS]
    cols = np.r_[0:WIDTH, WIDTH + LORA:2 * WIDTH + LORA, 2 * WIDTH + LORA:3 * WIDTH + LORA,
                 3 * WIDTH + 2 * LORA:4 * WIDTH + 2 * LORA, WIDTH:WIDTH + LORA,
                 3 * WIDTH + LORA:3 * WIDTH + 2 * LORA]
    pa = jnp.stack([p['rwkv_w0'][l], p['rwkv_a0'][l], p['rwkv_k_k'][l], p['rwkv_k_a'][l],
                    p['rwkv_r_k'][l].reshape(WIDTH)] + [jnp.zeros((WIDTH,), F32)] * 3)
    r, lw, k2, v, kk, bv, bonus, gate_a = pl.pallas_call(
        _rwkv_prep_kernel,
        grid=(bsz, seq // tma),
        in_specs=[toka(D_MODEL), _full((1, D_MODEL)), _full((D_MODEL, A_COLS)), _full((1, A_COLS)),
                  _full((8, WIDTH)), _full((LORA, WIDTH)), _full((LORA, WIDTH)), _full((WIDTH, WIDTH))],
        out_specs=[toka(WIDTH)] * 8,
        out_shape=[act(BF16), act(F32)] + [act(BF16)] * 6,
        scratch_shapes=[pltpu.VMEM((8, A_COLS), F32)],
        compiler_params=_params("parallel", "arbitrary"),
        name="rwkv_prep",
    )(x, g, wa[:, cols].astype(BF16), p['rwkv_mu'][l][cols][None, :], pa,
      p['rwkv_w2'][l], p['rwkv_a2'][l], ones_bd)

    ln = jnp.stack([p['rwkv_ln_w'][l], p['rwkv_ln_b'][l]] + [jnp.zeros((WIDTH,), F32)] * 6)
    y_a = pl.pallas_call(
        _rwkv_scan_kernel,
        grid=(bsz // rows_a, seq // blk),
        in_specs=[chk(WIDTH, rows_a)] * 8 + [_full((8, WIDTH))],
        out_specs=chk(WIDTH, rows_a),
        out_shape=act(BF16),
        scratch_shapes=[pltpu.VMEM((rows_a * HEADS // 2, 2 * HEAD_DIM, 2 * HEAD_DIM), F32)],
        compiler_params=_params("parallel", "arbitrary"),
        name="rwkv_scan",
    )(r, lw, k2, v, kk, bv, bonus, gate_a, ln)

    pad_tiles = PAD_ROWS // tm
    wb = w_in[:, A_COLS:A_COLS + 4 * WIDTH].astype(BF16)
    qk = jnp.stack([jnp.tile(p['attn_q_norm'][l], HEADS), jnp.tile(p['attn_k_norm'][l], HEADS)]
                   + [jnp.zeros((WIDTH,), F32)] * 6)
    shift = lambda b, i: (b, jnp.maximum(i - pad_tiles, 0), 0)
    q, k_pad, v_pad, gate_b = pl.pallas_call(
        functools.partial(_attn_prep_kernel, pad_tiles=pad_tiles),
        grid=(bsz, n_t + pad_tiles),
        in_specs=[pl.BlockSpec((None, tm, D_MODEL), shift), _full((1, D_MODEL)),
                  _full((D_MODEL, 4 * WIDTH)), _full((8, WIDTH)), _full((WIDTH, WIDTH))],
        out_specs=[pl.BlockSpec((None, tm, WIDTH), shift),
                   pl.BlockSpec((None, WIDTH, tm), lambda b, i: (b, 0, i)), tok(WIDTH),
                   pl.BlockSpec((None, tm, WIDTH), shift)],
        out_shape=[act(BF16), jax.ShapeDtypeStruct((bsz, WIDTH, seq + PAD_ROWS), BF16),
                   jax.ShapeDtypeStruct((bsz, seq + PAD_ROWS, WIDTH), BF16), act(BF16)],
        compiler_params=_params("parallel", "arbitrary"),
        name="attn_prep",
    )(x, g, wb, qk, ones_bd)

    band = pl.BlockSpec((None, seq + PAD_ROWS, WIDTH), lambda b, c: (b, 0, 0))
    band_t = pl.BlockSpec((None, WIDTH, seq + PAD_ROWS), lambda b, c: (b, 0, 0))
    bias = _band_bias(p['attn_rel_bias'][l])
    y_b = pl.pallas_call(
        _attn_band_kernel,
        grid=(bsz, seq // qblk),
        in_specs=[qck(WIDTH), band_t, band,
                  _full(bias.shape), qck(WIDTH)],
        out_specs=qck(WIDTH),
        out_shape=act(BF16),
        compiler_params=_params("parallel", "arbitrary"),
        name="attn_band",
    )(q, k_pad, v_pad, bias, gate_b)

    wc = w_in[:, A_COLS + 4 * WIDTH:A_COLS + 8 * WIDTH].astype(BF16)
    n_layers = p['hgrn_lb'].shape[0]
    qc, kc, lf, vc, gate_c = pl.pallas_call(
        functools.partial(_hgrn_prep_kernel, layer=l),
        grid=(bsz, n_t),
        in_specs=[tok(D_MODEL), _full((1, D_MODEL)), _full((D_MODEL, 4 * WIDTH)), _full((n_layers, WIDTH))],
        out_specs=[tok(WIDTH)] * 5,
        out_shape=[act(BF16), act(BF16), act(F32), act(BF16), act(BF16)],
        compiler_params=_params("parallel", "parallel"),
        name="hgrn_prep",
    )(x, g, wc, p['hgrn_lb'])

    y_c = pl.pallas_call(
        _hgrn_scan_kernel,
        grid=(bsz // rows_c, seq // blk),
        in_specs=[chk(WIDTH, rows_c)] * 5 + [_full(sel.shape), _full((WIDTH, WIDTH)), _full((1, WIDTH))],
        out_specs=chk(WIDTH, rows_c),
        out_shape=act(BF16),
        scratch_shapes=[pltpu.VMEM((rows_c * HEADS // 2, 2 * HEAD_DIM, 2 * HEAD_DIM), F32)],
        compiler_params=_params("parallel", "arbitrary"),
        name="hgrn_scan",
    )(qc, kc, lf, vc, gate_c, sel, ones_bd, jnp.tile(p['hgrn_norm'][l], HEADS)[None, :])

    wg = w_in[:, A_COLS + 8 * WIDTH:].astype(BF16)
    return pl.pallas_call(
        _merge_kernel,
        grid=(bsz, n_t),
        in_specs=[tok(D_MODEL), _full((1, D_MODEL)), _full((D_MODEL, 3 * D_MODEL)),
                  tok(WIDTH), tok(WIDTH), tok(WIDTH),
                  _full((WIDTH, D_MODEL)), _full((WIDTH, D_MODEL)), _full((WIDTH, D_MODEL)),
                  _full((D_MODEL, D_MODEL))],
        out_specs=tok(D_MODEL),
        out_shape=jax.ShapeDtypeStruct(x.shape, x.dtype),
        compiler_params=_params("parallel", "parallel"),
        name="merge",
    )(x, g, wg, y_a, y_b, y_c, p['proj_a'][l].astype(BF16), p['proj_b'][l].astype(BF16),
      p['proj_c'][l].astype(BF16), p['w_out'][l].astype(BF16))


def kernel(x, norm_g, w_in, rwkv_mu, rwkv_w0, rwkv_w2, rwkv_a0, rwkv_a2, rwkv_k_k, rwkv_k_a, rwkv_r_k,
           rwkv_ln_w, rwkv_ln_b, attn_q_norm, attn_k_norm, attn_rel_bias, hgrn_lb, hgrn_norm,
           proj_a, proj_b, proj_c, w_out):
    p = dict(norm_g=norm_g, w_in=w_in, rwkv_mu=rwkv_mu, rwkv_w0=rwkv_w0, rwkv_w2=rwkv_w2,
             rwkv_a0=rwkv_a0, rwkv_a2=rwkv_a2, rwkv_k_k=rwkv_k_k, rwkv_k_a=rwkv_k_a, rwkv_r_k=rwkv_r_k,
             rwkv_ln_w=rwkv_ln_w, rwkv_ln_b=rwkv_ln_b, attn_q_norm=attn_q_norm, attn_k_norm=attn_k_norm,
             attn_rel_bias=attn_rel_bias, hgrn_lb=hgrn_lb, hgrn_norm=hgrn_norm,
             proj_a=proj_a, proj_b=proj_b, proj_c=proj_c, w_out=w_out)
    head = np.arange(WIDTH) // HEAD_DIM
    ones_bd = jnp.asarray(head[:, None] == head[None, :], BF16)
    consts = (ones_bd, _edge_selectors())
    for l in range(norm_g.shape[0]):
        x = _layer(x, l, p, consts)
    return x
```

```python
import functools

import numpy as np
import jax
import jax.numpy as jnp
from jax import lax
from jax.experimental import pallas as pl
from jax.experimental.pallas import tpu as pltpu

F32 = jnp.float32
BF16 = jnp.bfloat16

D_MODEL = 1024
CHUNK = 64
HEAD_DIM = 64
HEADS = 8
WIDTH = HEADS * HEAD_DIM
LORA = 64
A_COLS = 4 * WIDTH + 2 * LORA
PAST_CHUNKS = 8
PAD_ROWS = PAST_CHUNKS * CHUNK
REL_CLIP = 256
RMS_EPS = 1e-6
GN_EPS = 64e-5
LEVELS = (32, 16, 8, 4, 2, 1)

TOKEN_TILE = 512
RWKV_TOKEN_TILE = 1024
PREP_SUB_ROWS = 256
RWKV_ROWS = 8
HGRN_ROWS = 8
SCAN_CHUNKS = 4
ATTN_CHUNKS = 8
ATTN_Q_CHUNKS = 4
VMEM_LIMIT = 56 * 1024 * 1024

_NN = (((1,), (0,)), ((), ()))
_NT = (((1,), (1,)), ((), ()))
_TN = (((0,), (0,)), ((), ()))


def _dg(a, b, dims):
    return lax.dot_general(a, b, dims, preferred_element_type=F32)


def _hi_lo(a):
    hi = a.astype(BF16)
    lo = (a - hi.astype(F32)).astype(BF16)
    return hi, lo


def _mm3(a, b, dims=_NN):
    a1, a2 = _hi_lo(a)
    b1, b2 = _hi_lo(b)
    return _dg(a1, b1, dims) + (_dg(a1, b2, dims) + _dg(a2, b1, dims))


def _head_sum(a, ones_bd):
    return _dg(a.astype(BF16), ones_bd, _NN)


def _block_diag(x):
    low = lax.broadcasted_iota(jnp.int32, x.shape, 1) < HEAD_DIM
    zero = jnp.zeros_like(x)
    return jnp.concatenate([jnp.where(low, x, zero), jnp.where(low, zero, x)], axis=0)


def _block_diag_cols(xt):
    top = lax.broadcasted_iota(jnp.int32, xt.shape, 0) < HEAD_DIM
    zero = jnp.zeros_like(xt)
    return jnp.concatenate([jnp.where(top, xt, zero), jnp.where(top, zero, xt)], axis=1)


def _pair_mean(a, low):
    zero = jnp.zeros_like(a)
    lo_sum = jnp.sum(jnp.where(low, a, zero), axis=-1, keepdims=True)
    hi_sum = jnp.sum(jnp.where(low, zero, a), axis=-1, keepdims=True)
    return jnp.where(low, lo_sum, hi_sum) * (1.0 / HEAD_DIM)


def _sigmoid(x):
    return 0.5 * jnp.tanh(0.5 * x) + 0.5


def _split3(x):
    p1 = x.astype(BF16)
    rem = x - p1.astype(F32)
    p2 = rem.astype(BF16)
    p3 = (rem - p2.astype(F32)).astype(BF16)
    return jnp.concatenate([p1, p2, p3], axis=0)


def _chunk_loop(n_chunks, step, body):
    def it(i, carry):
        body(pl.multiple_of(i * (step * CHUNK), step * CHUNK))
        return carry
    lax.fori_loop(0, n_chunks // step, it, 0)


def _sub_blocks(n_rows):
    sub = min(PREP_SUB_ROWS, n_rows)
    return [slice(i * sub, (i + 1) * sub) for i in range(n_rows // sub)]


def _norm_proj(x, g_ref, w_ref):
    ms = jnp.mean(x * x, axis=-1, keepdims=True)
    h = x * lax.rsqrt(ms + RMS_EPS) * g_ref[...]
    return jnp.dot(h.astype(BF16), w_ref[...], preferred_element_type=F32)


def _rwkv_prep_kernel(x_ref, g_ref, w_ref, mu_ref, pa_ref, w2_ref, a2_ref, ones_ref,
                      r_o, lw_o, k_o, v_o, kk_o, b_o, bonus_o, gate_o, carry_ref):
    @pl.when(pl.program_id(1) == 0)
    def _():
        carry_ref[...] = jnp.zeros_like(carry_ref)

    w0, a0, k_k, k_a, r_k = (pa_ref[i:i + 1, :] for i in range(5))
    ones_bd = ones_ref[...]
    last = carry_ref[7:8, :]
    blocks = _sub_blocks(x_ref.shape[0])
    proj = [_norm_proj(x_ref[rs, :], g_ref, w_ref) for rs in blocks]
    for rs, za in zip(blocks, proj):
        sub = za.shape[0]
        row = lax.broadcasted_iota(jnp.int32, (sub, 1), 0)
        prev = jnp.where(row == 0, last, pltpu.roll(za, 1, 0))
        last = za[sub - 1:sub, :]
        tail = za[sub - 8:, :]
        xs = za + (prev - za) * mu_ref[...]

        r = xs[:, 0:WIDTH]
        k = xs[:, WIDTH:2 * WIDTH]
        v = xs[:, 2 * WIDTH:3 * WIDTH]
        gate = xs[:, 3 * WIDTH:4 * WIDTH]
        wl = xs[:, 4 * WIDTH:4 * WIDTH + LORA]
        al = xs[:, 4 * WIDTH + LORA:4 * WIDTH + 2 * LORA]

        u = -(w0 + _mm3(jnp.tanh(wl), w2_ref[...]))
        softplus = jnp.maximum(u, 0.0) + jnp.log(1.0 + jnp.exp(-jnp.abs(u)))
        lw = -jnp.exp(-softplus - 0.5)
        a = _sigmoid(a0 + _mm3(al, a2_ref[...]))

        kkp = k * k_k
        kk = kkp * lax.rsqrt(jnp.maximum(_head_sum(kkp * kkp, ones_bd), 1e-24))
        k2 = k * (1.0 + (a - 1.0) * k_a)

        r_o[rs, :] = r.astype(r_o.dtype)
        lw_o[rs, :] = lw
        k_o[rs, :] = k2.astype(k_o.dtype)
        v_o[rs, :] = v.astype(v_o.dtype)
        kk_o[rs, :] = kk.astype(kk_o.dtype)
        b_o[rs, :] = (kk * a).astype(b_o.dtype)
        bonus_o[rs, :] = (_head_sum(r * k2 * r_k, ones_bd) * v).astype(bonus_o.dtype)
        gate_o[rs, :] = (gate * _sigmoid(gate)).astype(gate_o.dtype)
    carry_ref[...] = tail


def _rwkv_scan_kernel(r_ref, *refs):
    s_ref = refs[-1]

    @pl.when(pl.program_id(1) == 0)
    def _():
        s_ref[...] = jnp.zeros_like(s_ref)

    _chunk_loop(r_ref.shape[1] // CHUNK, 1, functools.partial(_rwkv_chunk, r_ref, *refs))


def _rwkv_chunk(r_ref, lw_ref, k_ref, v_ref, kk_ref, b_ref, bonus_ref, gate_ref, ln_ref, tri_ref,
                y_o, s_ref, off):
    rows = r_ref.shape[0]
    at = pl.ds(off, CHUNK)
    a_t, b_t, k_t, r_f, r_t, b_h, k_h, v_b, pend = ([] for _ in range(9))
    for j in range(rows):
        lw = lw_ref[j, at, :]
        cum = _dg(tri_ref[...], _split3(lw), _NN)
        cend = cum[CHUNK - 1:CHUNK, :]
        pinv = jnp.exp(-cum)
        dend = jnp.exp(cend - cum)
        kk, bv, k2 = kk_ref[j, at, :].astype(F32), b_ref[j, at, :].astype(F32), k_ref[j, at, :].astype(F32)
        a_t.append((-kk * jnp.exp(cum - lw)).astype(BF16))
        b_t.append((bv * pinv).astype(BF16))
        k_t.append((k2 * pinv).astype(BF16))
        r_f.append(r_ref[j, at, :].astype(F32) * jnp.exp(cum))
        r_t.append(r_f[j].astype(BF16))
        b_h.append((bv * dend).astype(BF16))
        k_h.append((k2 * dend).astype(BF16))
        v_b.append(v_ref[j, at, :])
        pend.append(jnp.exp(cend))

    P2 = 2 * HEAD_DIM
    ri = lax.broadcasted_iota(jnp.int32, (CHUNK, P2), 0)
    ci = lax.broadcasted_iota(jnp.int32, (CHUNK, P2), 1) % CHUNK
    strict = ri > ci
    incl = ri >= ci
    low = lax.broadcasted_iota(jnp.int32, (CHUNK, P2), 1) < HEAD_DIM
    own = ((lax.broadcasted_iota(jnp.int32, (P2, P2), 0) < HEAD_DIM)
           == (lax.broadcasted_iota(jnp.int32, (P2, P2), 1) < HEAD_DIM))
    units = [(j, slice(p * P2, (p + 1) * P2)) for j in range(rows) for p in range(HEADS // 2)]
    un = range(len(units))
    top, bot = slice(0, CHUNK), slice(CHUNK, 2 * CHUNK)
    left, right = slice(0, P2), slice(P2, 2 * P2)

    sc = [_dg(jnp.concatenate([a_t[j][:, ln], r_t[j][:, ln]], axis=0),
              jnp.concatenate([_block_diag(b_t[j][:, ln]), _block_diag(k_t[j][:, ln])], axis=0), _NT)
          for j, ln in units]
    m = [jnp.where(strict, sc[u][top, left], 0.0) for u in un]
    m_ak = [jnp.where(strict, sc[u][top, right], 0.0).astype(BF16) for u in un]
    l_rb = [jnp.where(incl, sc[u][bot, left], 0.0).astype(BF16) for u in un]
    l_rk = [jnp.where(incl, sc[u][bot, right], 0.0).astype(BF16) for u in un]
    v_bd = [_block_diag(v_b[j][:, ln]) for j, ln in units]

    u0 = [_dg(m_ak[u], v_bd[u], _NN).astype(BF16) for u in un]
    t = [jnp.where(ri == ci, 1.0, m[u]) for u in un]
    mb = [m[u].astype(BF16) for u in un]
    mb = [_dg(mb[u], _block_diag(mb[u]), _NN).astype(BF16) for u in un]
    for i in range(1, 6):
        if i < 5:
            tm = [_dg(jnp.concatenate([t[u].astype(BF16), mb[u]], axis=0), _block_diag(mb[u]), _NN) for u in un]
            t = [t[u] + tm[u][top, :] for u in un]
            mb = [tm[u][bot, :].astype(BF16) for u in un]
        else:
            t = [t[u] + _dg(t[u].astype(BF16), _block_diag(mb[u]), _NN) for u in un]
    xb = [_dg(t[u].astype(BF16),
              jnp.concatenate([_block_diag(a_t[j][:, ln]), _block_diag(u0[u])], axis=1), _NN).astype(BF16)
          for u, (j, ln) in enumerate(units)]
    x_bd = [jnp.concatenate([_block_diag(xb[u][:, left]), _block_diag(xb[u][:, right])], axis=1) for u in un]

    s = [s_ref[u] for u in un]
    sb = [s[u].astype(BF16) for u in un]
    lx = [_dg(l_rb[u], x_bd[u], _NN) for u in un]
    lv = [_dg(l_rk[u], v_bd[u], _NN) for u in un]
    r2 = [(r_f[j][:, ln] + lx[u][:, left]).astype(BF16) for u, (j, ln) in enumerate(units)]
    y = [_dg(r2[u], sb[u], _NT) + lx[u][:, right] + lv[u] for u in un]

    q = [jnp.where(own, _dg(xb[u][:, left], b_h[j][:, ln], _TN), 0.0).astype(BF16)
         for u, (j, ln) in enumerate(units)]
    uv = [_dg(jnp.concatenate([xb[u][:, right], v_b[j][:, ln]], axis=0),
              jnp.concatenate([b_h[j][:, ln], k_h[j][:, ln]], axis=0), _TN) for u, (j, ln) in enumerate(units)]
    sq = [_dg(sb[u], q[u], _NN) for u in un]
    for u, (j, ln) in enumerate(units):
        s_ref[u] = s[u] * pend[j][:, ln] + sq[u] + jnp.where(own, uv[u], 0.0)

    bonus = [bonus_ref[j, at, :].astype(F32) for j in range(rows)]
    gate = [gate_ref[j, at, :].astype(F32) for j in range(rows)]
    for u, (j, ln) in enumerate(units):
        yc = y[u] - _pair_mean(y[u], low)
        yn = yc * lax.rsqrt(_pair_mean(yc * yc, low) + GN_EPS) * ln_ref[0:1, ln] + ln_ref[1:2, ln]
        y_o[j, at, ln] = ((yn + bonus[j][:, ln]) * gate[j][:, ln]).astype(y_o.dtype)


def _attn_prep_kernel(x_ref, g_ref, w_ref, qk_ref, ones_ref, q_o, k_o, v_o, gate_o, *, pad_tiles):
    i = pl.program_id(1)

    @pl.when(i < pad_tiles)
    def _():
        k_o[...] = jnp.zeros_like(k_o)
        v_o[...] = jnp.zeros_like(v_o)

    @pl.when(i >= pad_tiles)
    def _():
        zb = _norm_proj(x_ref[...], g_ref, w_ref)
        ones_bd = ones_ref[...]
        q = zb[:, 0:WIDTH]
        k = zb[:, WIDTH:2 * WIDTH]
        gate = zb[:, 3 * WIDTH:4 * WIDTH]
        qn = q * lax.rsqrt(_head_sum(q * q, ones_bd) * (1.0 / HEAD_DIM) + RMS_EPS) * qk_ref[0:1, :]
        kn = k * lax.rsqrt(_head_sum(k * k, ones_bd) * (1.0 / HEAD_DIM) + RMS_EPS) * qk_ref[1:2, :]
        q_o[...] = (qn * HEAD_DIM ** -0.5).astype(q_o.dtype)
        k_o[...] = kn.T.astype(k_o.dtype)
        v_o[...] = zb[:, 2 * WIDTH:3 * WIDTH].astype(v_o.dtype)
        gate_o[...] = (gate * _sigmoid(gate)).astype(gate_o.dtype)


def _attn_band_kernel(q_ref, *refs):
    n_chunks = q_ref.shape[0] // CHUNK
    first = pl.program_id(1) * n_chunks
    _chunk_loop(n_chunks, ATTN_Q_CHUNKS, functools.partial(_attn_group, first, q_ref, *refs))


def _attn_group(first, q_ref, k_ref, v_ref, bias_ref, gate_ref, y_o, off):
    nq = ATTN_Q_CHUNKS * CHUNK
    nk = (ATTN_Q_CHUNKS + PAST_CHUNKS) * CHUNK
    at = pl.ds(off, nq)
    n = first + off // CHUNK
    start = pl.multiple_of(first * CHUNK + off, CHUNK)
    q = q_ref[at, :]
    kt = k_ref[:, pl.ds(pl.multiple_of(start, 2 * HEAD_DIM), nk)]
    vb = v_ref[pl.ds(start, nk), :]
    qc = lax.broadcasted_iota(jnp.int32, (nq, nk), 0) // CHUNK
    kc = lax.broadcasted_iota(jnp.int32, (nq, nk), 1) // CHUNK
    valid = (kc >= qc) & (kc <= qc + PAST_CHUNKS) & (kc >= PAST_CHUNKS - n)
    gate = gate_ref[at, :].astype(F32)
    low_q = lax.broadcasted_iota(jnp.int32, (nq, 2 * HEAD_DIM), 1) < HEAD_DIM
    for pair in range(HEADS // 2):
        lanes = slice(pair * 2 * HEAD_DIM, (pair + 1) * 2 * HEAD_DIM)
        v_bd = _block_diag(vb[:, lanes])
        s2 = _dg(q[:, lanes], _block_diag_cols(kt[lanes, :]), _NN)
        p, inv = [], []
        for i in range(2):
            s = jnp.where(valid, s2[:, i * nk:(i + 1) * nk] + bias_ref[2 * pair + i], -jnp.inf)
            e = jnp.exp(s - jnp.max(s, axis=-1, keepdims=True))
            p.append(e.astype(BF16))
            inv.append(1.0 / jnp.sum(e, axis=-1, keepdims=True))
        o = _dg(jnp.concatenate(p, axis=1), v_bd, _NN)
        y_o[at, lanes] = (o * jnp.where(low_q, inv[0], inv[1]) * gate[:, lanes]).astype(y_o.dtype)


def _hgrn_prep_kernel(x_ref, g_ref, w_ref, lb_ref, q_o, k_o, lf_o, v_o, gate_o, *, layer):
    lbp = lb_ref[...]
    e = jnp.exp(lbp - jnp.max(lbp, axis=0, keepdims=True))
    sm = e / jnp.sum(e, axis=0, keepdims=True)
    lb = jnp.sum(sm[0:layer + 1], axis=0, keepdims=True) - sm[0:1]
    zc = _norm_proj(x_ref[...], g_ref, w_ref)
    q = zc[:, 0:WIDTH]
    gate = zc[:, 3 * WIDTH:4 * WIDTH]
    fg = lb + (1.0 - lb) * _sigmoid(zc[:, WIDTH:2 * WIDTH])
    q_o[...] = (q * _sigmoid(q)).astype(q_o.dtype)
    k_o[...] = (1.0 - fg).astype(k_o.dtype)
    lf_o[...] = jnp.log(fg)
    v_o[...] = zc[:, 2 * WIDTH:3 * WIDTH].astype(v_o.dtype)
    gate_o[...] = (gate * _sigmoid(gate)).astype(gate_o.dtype)


def _hgrn_scan_kernel(q_ref, *refs):
    s_ref = refs[-1]

    @pl.when(pl.program_id(1) == 0)
    def _():
        s_ref[...] = jnp.zeros_like(s_ref)

    _chunk_loop(q_ref.shape[1] // CHUNK, 1, functools.partial(_hgrn_chunk, q_ref, *refs))


def _hgrn_chunk(q_ref, k_ref, lf_ref, v_ref, gate_ref, sel_ref, ones_ref, ng_ref, y_o, s_ref, off):
    rows = q_ref.shape[0]
    at = pl.ds(off, CHUNK)
    n_lv = len(LEVELS)
    row = lax.broadcasted_iota(jnp.int32, (CHUNK, 1), 0)
    second = [((row // w) % 2) == 1 for w in LEVELS]
    sign = [jnp.where(sec, 1.0, -1.0) for sec in second]

    z_lv, qk_diag, q_in, k_out, dec, v_f, v_b = ([] for _ in range(7))
    for j in range(rows):
        q, k, v_b16 = q_ref[j, at, :].astype(F32), k_ref[j, at, :].astype(F32), v_ref[j, at, :]
        sums = _dg(sel_ref[...], _split3(lf_ref[j, at, :]), _NN)
        b = sums[0:CHUNK, :]
        bend = b[CHUNK - 1:CHUNK, :]
        zl = []
        for i in range(n_lv):
            edge = sums[(i + 1) * CHUNK:(i + 2) * CHUNK, :]
            e = jnp.exp((b - edge) * sign[i])
            z = jnp.where(second[i], q, k) * e
            zl.append((z.astype(BF16), z.T.astype(BF16)))
        z_lv.append(zl)
        qk_diag.append(_head_sum(q * k, ones_ref[...]))
        q_in.append((q * jnp.exp(b)).astype(BF16))
        k_out.append((k * jnp.exp(bend - b)).astype(BF16))
        dec.append(jnp.exp(bend))
        v_f.append(v_b16.astype(F32))
        v_b.append(v_b16)

    units = [(j, slice(p * 2 * HEAD_DIM, (p + 1) * 2 * HEAD_DIM)) for j in range(rows) for p in range(HEADS // 2)]
    un = range(len(units))
    low = lax.broadcasted_iota(jnp.int32, (CHUNK, 2 * HEAD_DIM), 1) < HEAD_DIM
    ri2 = lax.broadcasted_iota(jnp.int32, (CHUNK, 2 * HEAD_DIM), 0)
    ci2 = lax.broadcasted_iota(jnp.int32, (CHUNK, 2 * HEAD_DIM), 1) % CHUNK
    keep = [((ri2 // (2 * w)) == (ci2 // (2 * w))) & ((ri2 // w) % 2 == 1) & ((ci2 // w) % 2 == 0) for w in LEVELS]
    v_bd = [_block_diag(v_b[j][:, ln]) for j, ln in units]
    att = []
    for j, ln in units:
        pr = [_dg(z[:, ln], _block_diag_cols(zt[ln, :]), _NN) for z, zt in z_lv[j]]
        att.append(sum(jnp.where(keep[i], pr[i], 0.0) for i in range(n_lv)))
    s = [s_ref[u] for u in un]
    o_in = [_dg(q_in[j][:, ln], s[u].astype(BF16), _NT) for u, (j, ln) in enumerate(units)]
    o_at = [_dg(att[u].astype(BF16), v_bd[u], _NN) for u in un]
    vk = [_dg(v_b[j][:, ln], k_out[j][:, ln], _TN) for j, ln in units]
    own = ((lax.broadcasted_iota(jnp.int32, (2 * HEAD_DIM, 2 * HEAD_DIM), 0) < HEAD_DIM)
           == (lax.broadcasted_iota(jnp.int32, (2 * HEAD_DIM, 2 * HEAD_DIM), 1) < HEAD_DIM))
    for u, (j, ln) in enumerate(units):
        s_ref[u] = s[u] * dec[j][:, ln] + jnp.where(own, vk[u], 0.0)
    gate = [gate_ref[j, at, :].astype(F32) for j in range(rows)]
    for u, (j, ln) in enumerate(units):
        o = o_at[u] + o_in[u] + qk_diag[j][:, ln] * v_f[j][:, ln]
        o = o * lax.rsqrt(_pair_mean(o * o, low) + RMS_EPS) * ng_ref[:, ln]
        y_o[j, at, ln] = (o * gate[j][:, ln]).astype(y_o.dtype)


def _merge_kernel(x_ref, g_ref, wg_ref, ya_ref, yb_ref, yc_ref, pa_ref, pb_ref, pc_ref, wo_ref, o_ref):
    gates = _sigmoid(_norm_proj(x_ref[...], g_ref, wg_ref))
    m = (gates[:, 0:D_MODEL] * jnp.dot(ya_ref[...], pa_ref[...], preferred_element_type=F32)
         + gates[:, D_MODEL:2 * D_MODEL] * jnp.dot(yb_ref[...], pb_ref[...], preferred_element_type=F32)
         + gates[:, 2 * D_MODEL:3 * D_MODEL] * jnp.dot(yc_ref[...], pc_ref[...], preferred_element_type=F32))
    o_ref[...] = x_ref[...] + jnp.dot(m.astype(BF16), wo_ref[...], preferred_element_type=F32)


def _params(*sem):
    return pltpu.CompilerParams(dimension_semantics=sem, vmem_limit_bytes=VMEM_LIMIT)


def _full(shape):
    return pl.BlockSpec(shape, lambda *_: (0,) * len(shape))


def _edge_selectors():
    tri = np.tril(np.ones((CHUNK, CHUNK), np.float32))
    pick = np.zeros((len(LEVELS) * CHUNK, CHUNK), np.float32)
    for i, w in enumerate(LEVELS):
        for t in range(CHUNK):
            pick[i * CHUNK + t, (t // (2 * w)) * 2 * w + w - 1] = 1.0
    sel = np.concatenate([tri, pick @ tri], axis=0)
    return jnp.asarray(np.concatenate([sel, sel, sel], axis=1), BF16)


def _band_bias(rel_bias):
    nq = ATTN_Q_CHUNKS * CHUNK
    nk = (ATTN_Q_CHUNKS + PAST_CHUNKS) * CHUNK
    period = nk + nq
    j = np.arange(period)
    far = REL_CLIP + CHUNK - 1
    w = rel_bias.astype(F32)[:, np.where(j < nk, np.clip(PAD_ROWS - j, -(CHUNK - 1), REL_CLIP) + CHUNK - 1, far)]
    heads = w.shape[0]
    return jnp.tile(w, (1, nq))[:, :nq * (period - 1)].reshape(heads, nq, period - 1)[:, :, :nk]


def _layer(x, l, p, consts):
    bsz, seq, _ = x.shape
    tm = min(TOKEN_TILE, seq)
    n_t = seq // tm
    n_c = seq // CHUNK
    ones_bd, sel = consts
    g = p['norm_g'][l][None, :]

    tma = min(RWKV_TOKEN_TILE, seq)
    tok = lambda w: pl.BlockSpec((None, tm, w), lambda b, t: (b, t, 0))
    toka = lambda w: pl.BlockSpec((None, tma, w), lambda b, t: (b, t, 0))
    rows_a, rows_c = min(RWKV_ROWS, bsz), min(HGRN_ROWS, bsz)
    blk = min(SCAN_CHUNKS, n_c) * CHUNK
    qblk = min(ATTN_CHUNKS, n_c) * CHUNK
    chk = lambda w, rows: pl.BlockSpec((rows, blk, w), lambda b, c: (b, c, 0))
    qck = lambda w: pl.BlockSpec((None, qblk, w), lambda b, c: (b, c, 0))
    act = lambda dt: jax.ShapeDtypeStruct((bsz, seq, WIDTH), dt)

    w_in = p['w_in'][l]
    spans = ((0, WIDTH), (WIDTH + LORA, 2 * WIDTH + LORA), (2 * WIDTH + LORA, 3 * WIDTH + LORA),
             (3 * WIDTH + 2 * LORA, 4 * WIDTH + 2 * LORA), (WIDTH, WIDTH + LORA),
             (3 * WIDTH + LORA, 3 * WIDTH + 2 * LORA))
    reorder = lambda t: jnp.concatenate([t[..., a:b] for a, b in spans], axis=-1)
    pa = jnp.stack([p['rwkv_w0'][l], p['rwkv_a0'][l], p['rwkv_k_k'][l], p['rwkv_k_a'][l],
                    p['rwkv_r_k'][l].reshape(WIDTH)] + [jnp.zeros((WIDTH,), F32)] * 3)
    r, lw, k2, v, kk, bv, bonus, gate_a = pl.pallas_call(
        _rwkv_prep_kernel,
        grid=(bsz, seq // tma),
        in_specs=[toka(D_MODEL), _full((1, D_MODEL)), _full((D_MODEL, A_COLS)), _full((1, A_COLS)),
                  _full((8, WIDTH)), _full((LORA, WIDTH)), _full((LORA, WIDTH)), _full((WIDTH, WIDTH))],
        out_specs=[toka(WIDTH)] * 8,
        out_shape=[act(BF16), act(F32)] + [act(BF16)] * 6,
        scratch_shapes=[pltpu.VMEM((8, A_COLS), F32)],
        compiler_params=_params("parallel", "arbitrary"),
        name="rwkv_prep",
    )(x, g, reorder(w_in).astype(BF16), reorder(p['rwkv_mu'][l])[None, :], pa,
      p['rwkv_w2'][l], p['rwkv_a2'][l], ones_bd)

    ln = jnp.stack([p['rwkv_ln_w'][l], p['rwkv_ln_b'][l]] + [jnp.zeros((WIDTH,), F32)] * 6)
    y_a = pl.pallas_call(
        _rwkv_scan_kernel,
        grid=(bsz // rows_a, seq // blk),
        in_specs=[chk(WIDTH, rows_a)] * 8 + [_full((8, WIDTH)), _full((CHUNK, 3 * CHUNK))],
        out_specs=chk(WIDTH, rows_a),
        out_shape=act(BF16),
        scratch_shapes=[pltpu.VMEM((rows_a * HEADS // 2, 2 * HEAD_DIM, 2 * HEAD_DIM), F32)],
        compiler_params=_params("parallel", "arbitrary"),
        name="rwkv_scan",
    )(r, lw, k2, v, kk, bv, bonus, gate_a, ln, sel[:CHUNK])

    pad_tiles = PAD_ROWS // tm
    wb = w_in[:, A_COLS:A_COLS + 4 * WIDTH].astype(BF16)
    qk = jnp.stack([jnp.tile(p['attn_q_norm'][l], HEADS), jnp.tile(p['attn_k_norm'][l], HEADS)]
                   + [jnp.zeros((WIDTH,), F32)] * 6)
    shift = lambda b, i: (b, jnp.maximum(i - pad_tiles, 0), 0)
    q, k_pad, v_pad, gate_b = pl.pallas_call(
        functools.partial(_attn_prep_kernel, pad_tiles=pad_tiles),
        grid=(bsz, n_t + pad_tiles),
        in_specs=[pl.BlockSpec((None, tm, D_MODEL), shift), _full((1, D_MODEL)),
                  _full((D_MODEL, 4 * WIDTH)), _full((8, WIDTH)), _full((WIDTH, WIDTH))],
        out_specs=[pl.BlockSpec((None, tm, WIDTH), shift),
                   pl.BlockSpec((None, WIDTH, tm), lambda b, i: (b, 0, i)), tok(WIDTH),
                   pl.BlockSpec((None, tm, WIDTH), shift)],
        out_shape=[act(BF16), jax.ShapeDtypeStruct((bsz, WIDTH, seq + PAD_ROWS), BF16),
                   jax.ShapeDtypeStruct((bsz, seq + PAD_ROWS, WIDTH), BF16), act(BF16)],
        compiler_params=_params("parallel", "arbitrary"),
        name="attn_prep",
    )(x, g, wb, qk, ones_bd)

    band = pl.BlockSpec((None, seq + PAD_ROWS, WIDTH), lambda b, c: (b, 0, 0))
    band_t = pl.BlockSpec((None, WIDTH, seq + PAD_ROWS), lambda b, c: (b, 0, 0))
    bias = _band_bias(p['attn_rel_bias'][l])
    y_b = pl.pallas_call(
        _attn_band_kernel,
        grid=(bsz, seq // qblk),
        in_specs=[qck(WIDTH), band_t, band,
                  _full(bias.shape), qck(WIDTH)],
        out_specs=qck(WIDTH),
        out_shape=act(BF16),
        compiler_params=_params("parallel", "arbitrary"),
        name="attn_band",
    )(q, k_pad, v_pad, bias, gate_b)

    wc = w_in[:, A_COLS + 4 * WIDTH:A_COLS + 8 * WIDTH].astype(BF16)
    n_layers = p['hgrn_lb'].shape[0]
    qc, kc, lf, vc, gate_c = pl.pallas_call(
        functools.partial(_hgrn_prep_kernel, layer=l),
        grid=(bsz, n_t),
        in_specs=[tok(D_MODEL), _full((1, D_MODEL)), _full((D_MODEL, 4 * WIDTH)), _full((n_layers, WIDTH))],
        out_specs=[tok(WIDTH)] * 5,
        out_shape=[act(BF16), act(BF16), act(F32), act(BF16), act(BF16)],
        compiler_params=_params("parallel", "parallel"),
        name="hgrn_prep",
    )(x, g, wc, p['hgrn_lb'])

    y_c = pl.pallas_call(
        _hgrn_scan_kernel,
        grid=(bsz // rows_c, seq // blk),
        in_specs=[chk(WIDTH, rows_c)] * 5 + [_full(sel.shape), _full((WIDTH, WIDTH)), _full((1, WIDTH))],
        out_specs=chk(WIDTH, rows_c),
        out_shape=act(BF16),
        scratch_shapes=[pltpu.VMEM((rows_c * HEADS // 2, 2 * HEAD_DIM, 2 * HEAD_DIM), F32)],
        compiler_params=_params("parallel", "arbitrary"),
        name="hgrn_scan",
    )(qc, kc, lf, vc, gate_c, sel, ones_bd, jnp.tile(p['hgrn_norm'][l], HEADS)[None, :])

    wg = w_in[:, A_COLS + 8 * WIDTH:].astype(BF16)
    return pl.pallas_call(
        _merge_kernel,
        grid=(bsz, n_t),
        in_specs=[tok(D_MODEL), _full((1, D_MODEL)), _full((D_MODEL, 3 * D_MODEL)),
                  tok(WIDTH), tok(WIDTH), tok(WIDTH),
                  _full((WIDTH, D_MODEL)), _full((WIDTH, D_MODEL)), _full((WIDTH, D_MODEL)),
                  _full((D_MODEL, D_MODEL))],
        out_specs=tok(D_MODEL),
        out_shape=jax.ShapeDtypeStruct(x.shape, x.dtype),
        compiler_params=_params("parallel", "parallel"),
        name="merge",
    )(x, g, wg, y_a, y_b, y_c, p['proj_a'][l].astype(BF16), p['proj_b'][l].astype(BF16),
      p['proj_c'][l].astype(BF16), p['w_out'][l].astype(BF16))


def kernel(x, norm_g, w_in, rwkv_mu, rwkv_w0, rwkv_w2, rwkv_a0, rwkv_a2, rwkv_k_k, rwkv_k_a, rwkv_r_k,
           rwkv_ln_w, rwkv_ln_b, attn_q_norm, attn_k_norm, attn_rel_bias, hgrn_lb, hgrn_norm,
           proj_a, proj_b, proj_c, w_out):
    p = dict(norm_g=norm_g, w_in=w_in, rwkv_mu=rwkv_mu, rwkv_w0=rwkv_w0, rwkv_w2=rwkv_w2,
             rwkv_a0=rwkv_a0, rwkv_a2=rwkv_a2, rwkv_k_k=rwkv_k_k, rwkv_k_a=rwkv_k_a, rwkv_r_k=rwkv_r_k,
             rwkv_ln_w=rwkv_ln_w, rwkv_ln_b=rwkv_ln_b, attn_q_norm=attn_q_norm, attn_k_norm=attn_k_norm,
             attn_rel_bias=attn_rel_bias, hgrn_lb=hgrn_lb, hgrn_norm=hgrn_norm,
             proj_a=proj_a, proj_b=proj_b, proj_c=proj_c, w_out=w_out)
    head = np.arange(WIDTH) // HEAD_DIM
    ones_bd = jnp.asarray(head[:, None] == head[None, :], BF16)
    consts = (ones_bd, _edge_selectors())
    for l in range(norm_g.shape[0]):
        x = _layer(x, l, p, consts)
    return x
```

```python
import functools

import numpy as np
import jax
import jax.numpy as jnp
from jax import lax
from jax.experimental import pallas as pl
from jax.experimental.pallas import tpu as pltpu

F32 = jnp.float32
BF16 = jnp.bfloat16

D_MODEL = 1024
CHUNK = 64
HEAD_DIM = 64
HEADS = 8
WIDTH = HEADS * HEAD_DIM
LORA = 64
A_COLS = 4 * WIDTH + 2 * LORA
PAST_CHUNKS = 8
PAD_ROWS = PAST_CHUNKS * CHUNK
REL_CLIP = 256
RMS_EPS = 1e-6
GN_EPS = 64e-5
LOG2E = 1.4426950408889634
LEVELS = (32, 16, 8, 4, 2, 1)

TOKEN_TILE = 512
RWKV_TOKEN_TILE = 1024
PREP_SUB_ROWS = 256
RWKV_ROWS = 8
HGRN_ROWS = 8
SCAN_CHUNKS = 4
ATTN_CHUNKS = 8
ATTN_Q_CHUNKS = 4
VMEM_LIMIT = 56 * 1024 * 1024

_NN = (((1,), (0,)), ((), ()))
_NT = (((1,), (1,)), ((), ()))
_TN = (((0,), (0,)), ((), ()))


def _dg(a, b, dims):
    return lax.dot_general(a, b, dims, preferred_element_type=F32)


def _hi_lo(a):
    hi = a.astype(BF16)
    lo = (a - hi.astype(F32)).astype(BF16)
    return hi, lo


def _mm3(a, b, dims=_NN):
    a1, a2 = _hi_lo(a)
    b1, b2 = _hi_lo(b)
    return _dg(a1, b1, dims) + (_dg(a1, b2, dims) + _dg(a2, b1, dims))


def _head_sum(a, ones_bd):
    return _dg(a.astype(BF16), ones_bd, _NN)


def _block_diag(x):
    low = lax.broadcasted_iota(jnp.int32, x.shape, 1) < HEAD_DIM
    zero = jnp.zeros_like(x)
    return jnp.concatenate([jnp.where(low, x, zero), jnp.where(low, zero, x)], axis=0)


def _block_diag_cols(xt):
    top = lax.broadcasted_iota(jnp.int32, xt.shape, 0) < HEAD_DIM
    zero = jnp.zeros_like(xt)
    return jnp.concatenate([jnp.where(top, xt, zero), jnp.where(top, zero, xt)], axis=1)


def _pair_mean(a, low):
    zero = jnp.zeros_like(a)
    lo_sum = jnp.sum(jnp.where(low, a, zero), axis=-1, keepdims=True)
    hi_sum = jnp.sum(jnp.where(low, zero, a), axis=-1, keepdims=True)
    return jnp.where(low, lo_sum, hi_sum) * (1.0 / HEAD_DIM)


def _sigmoid(x):
    return 0.5 * jnp.tanh(0.5 * x) + 0.5


def _split3(x):
    p1 = x.astype(BF16)
    rem = x - p1.astype(F32)
    p2 = rem.astype(BF16)
    p3 = (rem - p2.astype(F32)).astype(BF16)
    return jnp.concatenate([p1, p2, p3], axis=0)


def _chunk_loop(n_chunks, step, body):
    def it(i, carry):
        body(pl.multiple_of(i * (step * CHUNK), step * CHUNK))
        return carry
    lax.fori_loop(0, n_chunks // step, it, 0)


def _sub_blocks(n_rows):
    sub = min(PREP_SUB_ROWS, n_rows)
    return [slice(i * sub, (i + 1) * sub) for i in range(n_rows // sub)]


def _norm_proj(x, g_ref, w_ref):
    ms = jnp.mean(x * x, axis=-1, keepdims=True)
    h = x * lax.rsqrt(ms + RMS_EPS) * g_ref[...]
    return jnp.dot(h.astype(BF16), w_ref[...], preferred_element_type=F32)


def _rwkv_prep_kernel(x_ref, g_ref, w_ref, mu_ref, pa_ref, w2_ref, a2_ref, ones_ref,
                      r_o, lw_o, k_o, v_o, kk_o, b_o, bonus_o, gate_o, carry_ref):
    @pl.when(pl.program_id(1) == 0)
    def _():
        carry_ref[...] = jnp.zeros_like(carry_ref)

    w0, a0, k_k, k_a, r_k = (pa_ref[i:i + 1, :] for i in range(5))
    ones_bd = ones_ref[...]
    last = carry_ref[7:8, :]
    blocks = _sub_blocks(x_ref.shape[0])
    proj = [_norm_proj(x_ref[rs, :], g_ref, w_ref) for rs in blocks]
    for rs, za in zip(blocks, proj):
        sub = za.shape[0]
        row = lax.broadcasted_iota(jnp.int32, (sub, 1), 0)
        prev = jnp.where(row == 0, last, pltpu.roll(za, 1, 0))
        last = za[sub - 1:sub, :]
        tail = za[sub - 8:, :]
        xs = za + (prev - za) * mu_ref[...]

        r = xs[:, 0:WIDTH]
        k = xs[:, WIDTH:2 * WIDTH]
        v = xs[:, 2 * WIDTH:3 * WIDTH]
        gate = xs[:, 3 * WIDTH:4 * WIDTH]
        wl = xs[:, 4 * WIDTH:4 * WIDTH + LORA]
        al = xs[:, 4 * WIDTH + LORA:4 * WIDTH + 2 * LORA]

        u = -(w0 + _mm3(jnp.tanh(wl), w2_ref[...]))
        softplus = jnp.maximum(u, 0.0) + jnp.log(1.0 + jnp.exp(-jnp.abs(u)))
        lw = -jnp.exp(-softplus - 0.5)
        a = _sigmoid(a0 + _mm3(al, a2_ref[...]))

        kkp = k * k_k
        kk = kkp * lax.rsqrt(jnp.maximum(_head_sum(kkp * kkp, ones_bd), 1e-24))
        k2 = k * (1.0 + (a - 1.0) * k_a)

        r_o[rs, :] = r.astype(r_o.dtype)
        lw_o[rs, :] = lw
        k_o[rs, :] = k2.astype(k_o.dtype)
        v_o[rs, :] = v.astype(v_o.dtype)
        kk_o[rs, :] = kk.astype(kk_o.dtype)
        b_o[rs, :] = (kk * a).astype(b_o.dtype)
        bonus_o[rs, :] = (_head_sum(r * k2 * r_k, ones_bd) * v).astype(bonus_o.dtype)
        gate_o[rs, :] = (gate * _sigmoid(gate)).astype(gate_o.dtype)
    carry_ref[...] = tail


def _rwkv_scan_kernel(r_ref, *refs):
    s_ref = refs[-1]

    @pl.when(pl.program_id(1) == 0)
    def _():
        s_ref[...] = jnp.zeros_like(s_ref)

    _chunk_loop(r_ref.shape[1] // CHUNK, 1, functools.partial(_rwkv_chunk, r_ref, *refs))


def _rwkv_chunk(r_ref, lw_ref, k_ref, v_ref, kk_ref, b_ref, bonus_ref, gate_ref, ln_ref, tri_ref,
                y_o, s_ref, off):
    rows = r_ref.shape[0]
    at = pl.ds(off, CHUNK)
    a_t, b_t, k_t, r_f, r_t, b_h, k_h, v_b, pend = ([] for _ in range(9))
    for j in range(rows):
        lw = lw_ref[j, at, :]
        cum = _dg(tri_ref[...], _split3(lw), _NN)
        cend = cum[CHUNK - 1:CHUNK, :]
        pinv = jnp.exp(-cum)
        dend = jnp.exp(cend - cum)
        kk, bv, k2 = kk_ref[j, at, :].astype(F32), b_ref[j, at, :].astype(F32), k_ref[j, at, :].astype(F32)
        a_t.append((-kk * jnp.exp(cum - lw)).astype(BF16))
        b_t.append((bv * pinv).astype(BF16))
        k_t.append((k2 * pinv).astype(BF16))
        r_f.append(r_ref[j, at, :].astype(F32) * jnp.exp(cum))
        r_t.append(r_f[j].astype(BF16))
        b_h.append((bv * dend).astype(BF16))
        k_h.append((k2 * dend).astype(BF16))
        v_b.append(v_ref[j, at, :])
        pend.append(jnp.exp(cend))

    P2 = 2 * HEAD_DIM
    ri = lax.broadcasted_iota(jnp.int32, (CHUNK, P2), 0)
    ci = lax.broadcasted_iota(jnp.int32, (CHUNK, P2), 1) % CHUNK
    strict = ri > ci
    incl = ri >= ci
    low = lax.broadcasted_iota(jnp.int32, (CHUNK, P2), 1) < HEAD_DIM
    own = ((lax.broadcasted_iota(jnp.int32, (P2, P2), 0) < HEAD_DIM)
           == (lax.broadcasted_iota(jnp.int32, (P2, P2), 1) < HEAD_DIM))
    units = [(j, slice(p * P2, (p + 1) * P2)) for j in range(rows) for p in range(HEADS // 2)]
    un = range(len(units))
    top, bot = slice(0, CHUNK), slice(CHUNK, 2 * CHUNK)
    left, right = slice(0, P2), slice(P2, 2 * P2)

    sc = [_dg(jnp.concatenate([a_t[j][:, ln], r_t[j][:, ln]], axis=0),
              jnp.concatenate([_block_diag(b_t[j][:, ln]), _block_diag(k_t[j][:, ln])], axis=0), _NT)
          for j, ln in units]
    m = [jnp.where(strict, sc[u][top, left], 0.0) for u in un]
    m_ak = [jnp.where(strict, sc[u][top, right], 0.0).astype(BF16) for u in un]
    l_rb = [jnp.where(incl, sc[u][bot, left], 0.0).astype(BF16) for u in un]
    l_rk = [jnp.where(incl, sc[u][bot, right], 0.0).astype(BF16) for u in un]
    v_bd = [_block_diag(v_b[j][:, ln]) for j, ln in units]

    u0 = [_dg(m_ak[u], v_bd[u], _NN).astype(BF16) for u in un]
    t = [jnp.where(ri == ci, 1.0, m[u]) for u in un]
    mb = [m[u].astype(BF16) for u in un]
    mb = [_dg(mb[u], _block_diag(mb[u]), _NN).astype(BF16) for u in un]
    for i in range(1, 6):
        if i < 5:
            tm = [_dg(jnp.concatenate([t[u].astype(BF16), mb[u]], axis=0), _block_diag(mb[u]), _NN) for u in un]
            t = [t[u] + tm[u][top, :] for u in un]
            mb = [tm[u][bot, :].astype(BF16) for u in un]
        else:
            t = [t[u] + _dg(t[u].astype(BF16), _block_diag(mb[u]), _NN) for u in un]
    xb = [_dg(t[u].astype(BF16),
              jnp.concatenate([_block_diag(a_t[j][:, ln]), _block_diag(u0[u])], axis=1), _NN).astype(BF16)
          for u, (j, ln) in enumerate(units)]
    x_bd = [jnp.concatenate([_block_diag(xb[u][:, left]), _block_diag(xb[u][:, right])], axis=1) for u in un]

    s = [s_ref[u] for u in un]
    sb = [s[u].astype(BF16) for u in un]
    lx = [_dg(l_rb[u], x_bd[u], _NN) for u in un]
    lv = [_dg(l_rk[u], v_bd[u], _NN) for u in un]
    r2 = [(r_f[j][:, ln] + lx[u][:, left]).astype(BF16) for u, (j, ln) in enumerate(units)]
    y = [_dg(r2[u], sb[u], _NT) + lx[u][:, right] + lv[u] for u in un]

    q = [jnp.where(own, _dg(xb[u][:, left], b_h[j][:, ln], _TN), 0.0).astype(BF16)
         for u, (j, ln) in enumerate(units)]
    uv = [_dg(jnp.concatenate([xb[u][:, right], v_b[j][:, ln]], axis=0),
              jnp.concatenate([b_h[j][:, ln], k_h[j][:, ln]], axis=0), _TN) for u, (j, ln) in enumerate(units)]
    sq = [_dg(sb[u], q[u], _NN) for u in un]
    for u, (j, ln) in enumerate(units):
        s_ref[u] = s[u] * pend[j][:, ln] + sq[u] + jnp.where(own, uv[u], 0.0)

    bonus = [bonus_ref[j, at, :].astype(F32) for j in range(rows)]
    gate = [gate_ref[j, at, :].astype(F32) for j in range(rows)]
    for u, (j, ln) in enumerate(units):
        yc = y[u] - _pair_mean(y[u], low)
        yn = yc * lax.rsqrt(_pair_mean(yc * yc, low) + GN_EPS) * ln_ref[0:1, ln] + ln_ref[1:2, ln]
        y_o[j, at, ln] = ((yn + bonus[j][:, ln]) * gate[j][:, ln]).astype(y_o.dtype)


def _attn_prep_kernel(x_ref, g_ref, w_ref, qk_ref, ones_ref, q_o, k_o, v_o, gate_o, *, pad_tiles):
    i = pl.program_id(1)

    @pl.when(i < pad_tiles)
    def _():
        k_o[...] = jnp.zeros_like(k_o)
        v_o[...] = jnp.zeros_like(v_o)

    @pl.when(i >= pad_tiles)
    def _():
        zb = _norm_proj(x_ref[...], g_ref, w_ref)
        ones_bd = ones_ref[...]
        q = zb[:, 0:WIDTH]
        k = zb[:, WIDTH:2 * WIDTH]
        gate = zb[:, 3 * WIDTH:4 * WIDTH]
        qn = q * lax.rsqrt(_head_sum(q * q, ones_bd) * (1.0 / HEAD_DIM) + RMS_EPS) * qk_ref[0:1, :]
        kn = k * lax.rsqrt(_head_sum(k * k, ones_bd) * (1.0 / HEAD_DIM) + RMS_EPS) * qk_ref[1:2, :]
        q_o[...] = (qn * (HEAD_DIM ** -0.5 * LOG2E)).astype(q_o.dtype)
        k_o[...] = kn.T.astype(k_o.dtype)
        v_o[...] = zb[:, 2 * WIDTH:3 * WIDTH].astype(v_o.dtype)
        gate_o[...] = (gate * _sigmoid(gate)).astype(gate_o.dtype)


def _attn_band_kernel(q_ref, *refs):
    n_chunks = q_ref.shape[0] // CHUNK
    first = pl.program_id(1) * n_chunks
    _chunk_loop(n_chunks, ATTN_Q_CHUNKS, functools.partial(_attn_group, first, q_ref, *refs))


def _attn_group(first, q_ref, k_ref, v_ref, bias_ref, gate_ref, y_o, off):
    nq = ATTN_Q_CHUNKS * CHUNK
    nk = (ATTN_Q_CHUNKS + PAST_CHUNKS) * CHUNK
    at = pl.ds(off, nq)
    n = first + off // CHUNK
    start = pl.multiple_of(first * CHUNK + off, CHUNK)
    q = q_ref[at, :]
    kt = k_ref[:, pl.ds(pl.multiple_of(start, 2 * HEAD_DIM), nk)]
    vb = v_ref[pl.ds(start, nk), :]
    var = jnp.minimum(n // ATTN_Q_CHUNKS, bias_ref.shape[0] - 1)
    gate = gate_ref[at, :].astype(F32)
    low_q = lax.broadcasted_iota(jnp.int32, (nq, 2 * HEAD_DIM), 1) < HEAD_DIM
    for pair in range(HEADS // 2):
        lanes = slice(pair * 2 * HEAD_DIM, (pair + 1) * 2 * HEAD_DIM)
        v_bd = _block_diag(vb[:, lanes])
        s2 = _dg(q[:, lanes], _block_diag_cols(kt[lanes, :]), _NN)
        p, inv = [], []
        for i in range(2):
            s = s2[:, i * nk:(i + 1) * nk] + bias_ref[var, 2 * pair + i]
            e = jnp.exp2(s - jnp.max(s, axis=-1, keepdims=True))
            p.append(e.astype(BF16))
            inv.append(1.0 / jnp.sum(e, axis=-1, keepdims=True))
        o = _dg(jnp.concatenate(p, axis=1), v_bd, _NN)
        y_o[at, lanes] = (o * jnp.where(low_q, inv[0], inv[1]) * gate[:, lanes]).astype(y_o.dtype)


def _hgrn_prep_kernel(x_ref, g_ref, w_ref, lb_ref, q_o, k_o, lf_o, v_o, gate_o, *, layer):
    lbp = lb_ref[...]
    e = jnp.exp(lbp - jnp.max(lbp, axis=0, keepdims=True))
    sm = e / jnp.sum(e, axis=0, keepdims=True)
    lb = jnp.sum(sm[0:layer + 1], axis=0, keepdims=True) - sm[0:1]
    zc = _norm_proj(x_ref[...], g_ref, w_ref)
    q = zc[:, 0:WIDTH]
    gate = zc[:, 3 * WIDTH:4 * WIDTH]
    fg = lb + (1.0 - lb) * _sigmoid(zc[:, WIDTH:2 * WIDTH])
    q_o[...] = (q * _sigmoid(q)).astype(q_o.dtype)
    k_o[...] = (1.0 - fg).astype(k_o.dtype)
    lf_o[...] = jnp.log(fg)
    v_o[...] = zc[:, 2 * WIDTH:3 * WIDTH].astype(v_o.dtype)
    gate_o[...] = (gate * _sigmoid(gate)).astype(gate_o.dtype)


def _hgrn_scan_kernel(q_ref, *refs):
    s_ref = refs[-1]

    @pl.when(pl.program_id(1) == 0)
    def _():
        s_ref[...] = jnp.zeros_like(s_ref)

    _chunk_loop(q_ref.shape[1] // CHUNK, 1, functools.partial(_hgrn_chunk, q_ref, *refs))


def _hgrn_chunk(q_ref, k_ref, lf_ref, v_ref, gate_ref, sel_ref, ones_ref, ng_ref, y_o, s_ref, off):
    rows = q_ref.shape[0]
    at = pl.ds(off, CHUNK)
    n_lv = len(LEVELS)
    row = lax.broadcasted_iota(jnp.int32, (CHUNK, 1), 0)
    second = [((row // w) % 2) == 1 for w in LEVELS]
    sign = [jnp.where(sec, 1.0, -1.0) for sec in second]

    z_lv, qk_diag, q_in, k_out, dec, v_f, v_b = ([] for _ in range(7))
    for j in range(rows):
        q, k, v_b16 = q_ref[j, at, :].astype(F32), k_ref[j, at, :].astype(F32), v_ref[j, at, :]
        sums = _dg(sel_ref[...], _split3(lf_ref[j, at, :]), _NN)
        b = sums[0:CHUNK, :]
        bend = b[CHUNK - 1:CHUNK, :]
        zl = []
        for i in range(n_lv):
            edge = sums[(i + 1) * CHUNK:(i + 2) * CHUNK, :]
            e = jnp.exp((b - edge) * sign[i])
            z = jnp.where(second[i], q, k) * e
            zl.append((z.astype(BF16), z.T.astype(BF16)))
        z_lv.append(zl)
        qk_diag.append(_head_sum(q * k, ones_ref[...]))
        q_in.append((q * jnp.exp(b)).astype(BF16))
        k_out.append((k * jnp.exp(bend - b)).astype(BF16))
        dec.append(jnp.exp(bend))
        v_f.append(v_b16.astype(F32))
        v_b.append(v_b16)

    units = [(j, slice(p * 2 * HEAD_DIM, (p + 1) * 2 * HEAD_DIM)) for j in range(rows) for p in range(HEADS // 2)]
    un = range(len(units))
    low = lax.broadcasted_iota(jnp.int32, (CHUNK, 2 * HEAD_DIM), 1) < HEAD_DIM
    ri2 = lax.broadcasted_iota(jnp.int32, (CHUNK, 2 * HEAD_DIM), 0)
    ci2 = lax.broadcasted_iota(jnp.int32, (CHUNK, 2 * HEAD_DIM), 1) % CHUNK
    keep = [((ri2 // (2 * w)) == (ci2 // (2 * w))) & ((ri2 // w) % 2 == 1) & ((ci2 // w) % 2 == 0) for w in LEVELS]
    v_bd = [_block_diag(v_b[j][:, ln]) for j, ln in units]
    att = []
    for j, ln in units:
        pr = [_dg(z[:, ln], _block_diag_cols(zt[ln, :]), _NN) for z, zt in z_lv[j]]
        att.append(sum(jnp.where(keep[i], pr[i], 0.0) for i in range(n_lv)))
    s = [s_ref[u] for u in un]
    o_in = [_dg(q_in[j][:, ln], s[u].astype(BF16), _NT) for u, (j, ln) in enumerate(units)]
    o_at = [_dg(att[u].astype(BF16), v_bd[u], _NN) for u in un]
    vk = [_dg(v_b[j][:, ln], k_out[j][:, ln], _TN) for j, ln in units]
    own = ((lax.broadcasted_iota(jnp.int32, (2 * HEAD_DIM, 2 * HEAD_DIM), 0) < HEAD_DIM)
           == (lax.broadcasted_iota(jnp.int32, (2 * HEAD_DIM, 2 * HEAD_DIM), 1) < HEAD_DIM))
    for u, (j, ln) in enumerate(units):
        s_ref[u] = s[u] * dec[j][:, ln] + jnp.where(own, vk[u], 0.0)
    gate = [gate_ref[j, at, :].astype(F32) for j in range(rows)]
    for u, (j, ln) in enumerate(units):
        o = o_at[u] + o_in[u] + qk_diag[j][:, ln] * v_f[j][:, ln]
        o = o * lax.rsqrt(_pair_mean(o * o, low) + RMS_EPS) * ng_ref[:, ln]
        y_o[j, at, ln] = (o * gate[j][:, ln]).astype(y_o.dtype)


def _merge_kernel(x_ref, g_ref, wg_ref, ya_ref, yb_ref, yc_ref, pa_ref, pb_ref, pc_ref, wo_ref, o_ref):
    gates = _sigmoid(_norm_proj(x_ref[...], g_ref, wg_ref))
    m = (gates[:, 0:D_MODEL] * jnp.dot(ya_ref[...], pa_ref[...], preferred_element_type=F32)
         + gates[:, D_MODEL:2 * D_MODEL] * jnp.dot(yb_ref[...], pb_ref[...], preferred_element_type=F32)
         + gates[:, 2 * D_MODEL:3 * D_MODEL] * jnp.dot(yc_ref[...], pc_ref[...], preferred_element_type=F32))
    o_ref[...] = x_ref[...] + jnp.dot(m.astype(BF16), wo_ref[...], preferred_element_type=F32)


def _params(*sem):
    return pltpu.CompilerParams(dimension_semantics=sem, vmem_limit_bytes=VMEM_LIMIT)


def _full(shape):
    return pl.BlockSpec(shape, lambda *_: (0,) * len(shape))


def _edge_selectors():
    tri = np.tril(np.ones((CHUNK, CHUNK), np.float32))
    pick = np.zeros((len(LEVELS) * CHUNK, CHUNK), np.float32)
    for i, w in enumerate(LEVELS):
        for t in range(CHUNK):
            pick[i * CHUNK + t, (t // (2 * w)) * 2 * w + w - 1] = 1.0
    sel = np.concatenate([tri, pick @ tri], axis=0)
    return jnp.asarray(np.concatenate([sel, sel, sel], axis=1), BF16)


def _band_bias(rel_bias):
    nq = ATTN_Q_CHUNKS * CHUNK
    nk = (ATTN_Q_CHUNKS + PAST_CHUNKS) * CHUNK
    period = nk + nq
    j = np.arange(period)
    far = REL_CLIP + CHUNK - 1
    w = rel_bias.astype(F32)[:, np.where(j < nk, np.clip(PAD_ROWS - j, -(CHUNK - 1), REL_CLIP) + CHUNK - 1, far)]
    heads = w.shape[0]
    bias = jnp.tile(w, (1, nq))[:, :nq * (period - 1)].reshape(heads, nq, period - 1)[:, :, :nk] * LOG2E
    qc = np.arange(nq)[:, None] // CHUNK
    kc = np.arange(nk)[None, :] // CHUNK
    band = (kc >= qc) & (kc <= qc + PAST_CHUNKS)
    first_chunk = np.arange(0, PAST_CHUNKS + ATTN_Q_CHUNKS, ATTN_Q_CHUNKS)
    keep = band[None] & (kc[None] >= PAST_CHUNKS - first_chunk[:, None, None])
    return jnp.where(keep[:, None], bias[None], -jnp.inf)


def _layer(x, l, p, consts):
    bsz, seq, _ = x.shape
    tm = min(TOKEN_TILE, seq)
    n_t = seq // tm
    n_c = seq // CHUNK
    ones_bd, sel = consts
    g = p['norm_g'][l][None, :]

    tma = min(RWKV_TOKEN_TILE, seq)
    tok = lambda w: pl.BlockSpec((None, tm, w), lambda b, t: (b, t, 0))
    toka = lambda w: pl.BlockSpec((None, tma, w), lambda b, t: (b, t, 0))
    rows_a, rows_c = min(RWKV_ROWS, bsz), min(HGRN_ROWS, bsz)
    blk = min(SCAN_CHUNKS, n_c) * CHUNK
    qblk = min(ATTN_CHUNKS, n_c) * CHUNK
    chk = lambda w, rows: pl.BlockSpec((rows, blk, w), lambda b, c: (b, c, 0))
    qck = lambda w: pl.BlockSpec((None, qblk, w), lambda b, c: (b, c, 0))
    act = lambda dt: jax.ShapeDtypeStruct((bsz, seq, WIDTH), dt)

    w_in = p['w_in'][l]
    spans = ((0, WIDTH), (WIDTH + LORA, 2 * WIDTH + LORA), (2 * WIDTH + LORA, 3 * WIDTH + LORA),
             (3 * WIDTH + 2 * LORA, 4 * WIDTH + 2 * LORA), (WIDTH, WIDTH + LORA),
             (3 * WIDTH + LORA, 3 * WIDTH + 2 * LORA))
    reorder = lambda t: jnp.concatenate([t[..., a:b] for a, b in spans], axis=-1)
    pa = jnp.stack([p['rwkv_w0'][l], p['rwkv_a0'][l], p['rwkv_k_k'][l], p['rwkv_k_a'][l],
                    p['rwkv_r_k'][l].reshape(WIDTH)] + [jnp.zeros((WIDTH,), F32)] * 3)
    r, lw, k2, v, kk, bv, bonus, gate_a = pl.pallas_call(
        _rwkv_prep_kernel,
        grid=(bsz, seq // tma),
        in_specs=[toka(D_MODEL), _full((1, D_MODEL)), _full((D_MODEL, A_COLS)), _full((1, A_COLS)),
                  _full((8, WIDTH)), _full((LORA, WIDTH)), _full((LORA, WIDTH)), _full((WIDTH, WIDTH))],
        out_specs=[toka(WIDTH)] * 8,
        out_shape=[act(BF16), act(F32)] + [act(BF16)] * 6,
        scratch_shapes=[pltpu.VMEM((8, A_COLS), F32)],
        compiler_params=_params("parallel", "arbitrary"),
        name="rwkv_prep",
    )(x, g, reorder(w_in).astype(BF16), reorder(p['rwkv_mu'][l])[None, :], pa,
      p['rwkv_w2'][l], p['rwkv_a2'][l], ones_bd)

    ln = jnp.stack([p['rwkv_ln_w'][l], p['rwkv_ln_b'][l]] + [jnp.zeros((WIDTH,), F32)] * 6)
    y_a = pl.pallas_call(
        _rwkv_scan_kernel,
        grid=(bsz // rows_a, seq // blk),
        in_specs=[chk(WIDTH, rows_a)] * 8 + [_full((8, WIDTH)), _full((CHUNK, 3 * CHUNK))],
        out_specs=chk(WIDTH, rows_a),
        out_shape=act(BF16),
        scratch_shapes=[pltpu.VMEM((rows_a * HEADS // 2, 2 * HEAD_DIM, 2 * HEAD_DIM), F32)],
        compiler_params=_params("parallel", "arbitrary"),
        name="rwkv_scan",
    )(r, lw, k2, v, kk, bv, bonus, gate_a, ln, sel[:CHUNK])

    pad_tiles = PAD_ROWS // tm
    wb = w_in[:, A_COLS:A_COLS + 4 * WIDTH].astype(BF16)
    qk = jnp.stack([jnp.tile(p['attn_q_norm'][l], HEADS), jnp.tile(p['attn_k_norm'][l], HEADS)]
                   + [jnp.zeros((WIDTH,), F32)] * 6)
    shift = lambda b, i: (b, jnp.maximum(i - pad_tiles, 0), 0)
    q, k_pad, v_pad, gate_b = pl.pallas_call(
        functools.partial(_attn_prep_kernel, pad_tiles=pad_tiles),
        grid=(bsz, n_t + pad_tiles),
        in_specs=[pl.BlockSpec((None, tm, D_MODEL), shift), _full((1, D_MODEL)),
                  _full((D_MODEL, 4 * WIDTH)), _full((8, WIDTH)), _full((WIDTH, WIDTH))],
        out_specs=[pl.BlockSpec((None, tm, WIDTH), shift),
                   pl.BlockSpec((None, WIDTH, tm), lambda b, i: (b, 0, i)), tok(WIDTH),
                   pl.BlockSpec((None, tm, WIDTH), shift)],
        out_shape=[act(BF16), jax.ShapeDtypeStruct((bsz, WIDTH, seq + PAD_ROWS), BF16),
                   jax.ShapeDtypeStruct((bsz, seq + PAD_ROWS, WIDTH), BF16), act(BF16)],
        compiler_params=_params("parallel", "arbitrary"),
        name="attn_prep",
    )(x, g, wb, qk, ones_bd)

    band = pl.BlockSpec((None, seq + PAD_ROWS, WIDTH), lambda b, c: (b, 0, 0))
    band_t = pl.BlockSpec((None, WIDTH, seq + PAD_ROWS), lambda b, c: (b, 0, 0))
    bias = _band_bias(p['attn_rel_bias'][l])
    y_b = pl.pallas_call(
        _attn_band_kernel,
        grid=(bsz, seq // qblk),
        in_specs=[qck(WIDTH), band_t, band,
                  pl.BlockSpec(bias.shape, lambda b, c: (0, 0, 0, 0), pipeline_mode=pl.Buffered(1)),
                  qck(WIDTH)],
        out_specs=qck(WIDTH),
        out_shape=act(BF16),
        compiler_params=_params("parallel", "arbitrary"),
        name="attn_band",
    )(q, k_pad, v_pad, bias, gate_b)

    wc = w_in[:, A_COLS + 4 * WIDTH:A_COLS + 8 * WIDTH].astype(BF16)
    n_layers = p['hgrn_lb'].shape[0]
    qc, kc, lf, vc, gate_c = pl.pallas_call(
        functools.partial(_hgrn_prep_kernel, layer=l),
        grid=(bsz, n_t),
        in_specs=[tok(D_MODEL), _full((1, D_MODEL)), _full((D_MODEL, 4 * WIDTH)), _full((n_layers, WIDTH))],
        out_specs=[tok(WIDTH)] * 5,
        out_shape=[act(BF16), act(BF16), act(F32), act(BF16), act(BF16)],
        compiler_params=_params("parallel", "parallel"),
        name="hgrn_prep",
    )(x, g, wc, p['hgrn_lb'])

    y_c = pl.pallas_call(
        _hgrn_scan_kernel,
        grid=(bsz // rows_c, seq // blk),
        in_specs=[chk(WIDTH, rows_c)] * 5 + [_full(sel.shape), _full((WIDTH, WIDTH)), _full((1, WIDTH))],
        out_specs=chk(WIDTH, rows_c),
        out_shape=act(BF16),
        scratch_shapes=[pltpu.VMEM((rows_c * HEADS // 2, 2 * HEAD_DIM, 2 * HEAD_DIM), F32)],
        compiler_params=_params("parallel", "arbitrary"),
        name="hgrn_scan",
    )(qc, kc, lf, vc, gate_c, sel, ones_bd, jnp.tile(p['hgrn_norm'][l], HEADS)[None, :])

    wg = w_in[:, A_COLS + 8 * WIDTH:].astype(BF16)
    return pl.pallas_call(
        _merge_kernel,
        grid=(bsz, n_t),
        in_specs=[tok(D_MODEL), _full((1, D_MODEL)), _full((D_MODEL, 3 * D_MODEL)),
                  tok(WIDTH), tok(WIDTH), tok(WIDTH),
                  _full((WIDTH, D_MODEL)), _full((WIDTH, D_MODEL)), _full((WIDTH, D_MODEL)),
                  _full((D_MODEL, D_MODEL))],
        out_specs=tok(D_MODEL),
        out_shape=jax.ShapeDtypeStruct(x.shape, x.dtype),
        compiler_params=_params("parallel", "parallel"),
        name="merge",
    )(x, g, wg, y_a, y_b, y_c, p['proj_a'][l].astype(BF16), p['proj_b'][l].astype(BF16),
      p['proj_c'][l].astype(BF16), p['w_out'][l].astype(BF16))


def kernel(x, norm_g, w_in, rwkv_mu, rwkv_w0, rwkv_w2, rwkv_a0, rwkv_a2, rwkv_k_k, rwkv_k_a, rwkv_r_k,
           rwkv_ln_w, rwkv_ln_b, attn_q_norm, attn_k_norm, attn_rel_bias, hgrn_lb, hgrn_norm,
           proj_a, proj_b, proj_c, w_out):
    p = dict(norm_g=norm_g, w_in=w_in, rwkv_mu=rwkv_mu, rwkv_w0=rwkv_w0, rwkv_w2=rwkv_w2,
             rwkv_a0=rwkv_a0, rwkv_a2=rwkv_a2, rwkv_k_k=rwkv_k_k, rwkv_k_a=rwkv_k_a, rwkv_r_k=rwkv_r_k,
             rwkv_ln_w=rwkv_ln_w, rwkv_ln_b=rwkv_ln_b, attn_q_norm=attn_q_norm, attn_k_norm=attn_k_norm,
             attn_rel_bias=attn_rel_bias, hgrn_lb=hgrn_lb, hgrn_norm=hgrn_norm,
             proj_a=proj_a, proj_b=proj_b, proj_c=proj_c, w_out=w_out)
    head = np.arange(WIDTH) // HEAD_DIM
    ones_bd = jnp.asarray(head[:, None] == head[None, :], BF16)
    consts = (ones_bd, _edge_selectors())
    for l in range(norm_g.shape[0]):
        x = _layer(x, l, p, consts)
    return x
```

```python
import functools

import numpy as np
import jax
import jax.numpy as jnp
from jax import lax
from jax.experimental import pallas as pl
from jax.experimental.pallas import tpu as pltpu

F32 = jnp.float32
BF16 = jnp.bfloat16

D_MODEL = 1024
CHUNK = 64
HEAD_DIM = 64
HEADS = 8
WIDTH = HEADS * HEAD_DIM
LORA = 64
A_COLS = 4 * WIDTH + 2 * LORA
PAST_CHUNKS = 8
PAD_ROWS = PAST_CHUNKS * CHUNK
REL_CLIP = 256
RMS_EPS = 1e-6
GN_EPS = 64e-5
LOG2E = 1.4426950408889634
LEVELS = (32, 16, 8, 4, 2, 1)

TOKEN_TILE = 512
RWKV_TOKEN_TILE = 1024
PREP_SUB_ROWS = 256
RWKV_ROWS = 8
HGRN_ROWS = 8
SCAN_CHUNKS = 4
ATTN_CHUNKS = 8
ATTN_Q_CHUNKS = 4
VMEM_LIMIT = 56 * 1024 * 1024

_NN = (((1,), (0,)), ((), ()))
_NT = (((1,), (1,)), ((), ()))
_TN = (((0,), (0,)), ((), ()))


def _dg(a, b, dims):
    return lax.dot_general(a, b, dims, preferred_element_type=F32)


def _hi_lo(a):
    hi = a.astype(BF16)
    lo = (a - hi.astype(F32)).astype(BF16)
    return hi, lo


def _mm3(a, b, dims=_NN):
    a1, a2 = _hi_lo(a)
    b1, b2 = _hi_lo(b)
    return _dg(a1, b1, dims) + (_dg(a1, b2, dims) + _dg(a2, b1, dims))


def _head_sum(a, ones_bd):
    return _dg(a.astype(BF16), ones_bd, _NN)


def _block_diag(x):
    low = lax.broadcasted_iota(jnp.int32, x.shape, 1) < HEAD_DIM
    zero = jnp.zeros_like(x)
    return jnp.concatenate([jnp.where(low, x, zero), jnp.where(low, zero, x)], axis=0)


def _block_diag_cols(xt):
    top = lax.broadcasted_iota(jnp.int32, xt.shape, 0) < HEAD_DIM
    zero = jnp.zeros_like(xt)
    return jnp.concatenate([jnp.where(top, xt, zero), jnp.where(top, zero, xt)], axis=1)


def _pair_mean(a, low):
    zero = jnp.zeros_like(a)
    lo_sum = jnp.sum(jnp.where(low, a, zero), axis=-1, keepdims=True)
    hi_sum = jnp.sum(jnp.where(low, zero, a), axis=-1, keepdims=True)
    return jnp.where(low, lo_sum, hi_sum) * (1.0 / HEAD_DIM)


def _sigmoid(x):
    return 0.5 * jnp.tanh(0.5 * x) + 0.5


def _split3(x):
    p1 = x.astype(BF16)
    rem = x - p1.astype(F32)
    p2 = rem.astype(BF16)
    p3 = (rem - p2.astype(F32)).astype(BF16)
    return jnp.concatenate([p1, p2, p3], axis=0)


def _chunk_loop(n_chunks, step, body):
    def it(i, carry):
        body(pl.multiple_of(i * (step * CHUNK), step * CHUNK))
        return carry
    lax.fori_loop(0, n_chunks // step, it, 0)


def _sub_blocks(n_rows):
    sub = min(PREP_SUB_ROWS, n_rows)
    return [slice(i * sub, (i + 1) * sub) for i in range(n_rows // sub)]


def _norm_proj(x, g_ref, w_ref):
    ms = jnp.mean(x * x, axis=-1, keepdims=True)
    h = x * lax.rsqrt(ms + RMS_EPS) * g_ref[...]
    return jnp.dot(h.astype(BF16), w_ref[...], preferred_element_type=F32)


def _rwkv_prep_kernel(x_ref, g_ref, w_ref, mu_ref, pa_ref, w2_ref, a2_ref, ones_ref,
                      r_o, lw_o, k_o, v_o, kk_o, b_o, bonus_o, gate_o, carry_ref):
    @pl.when(pl.program_id(1) == 0)
    def _():
        carry_ref[...] = jnp.zeros_like(carry_ref)

    w0, a0, k_k, k_a, r_k = (pa_ref[i:i + 1, :] for i in range(5))
    ones_bd = ones_ref[...]
    last = carry_ref[7:8, :]
    blocks = _sub_blocks(x_ref.shape[0])
    proj = [_norm_proj(x_ref[rs, :], g_ref, w_ref) for rs in blocks]
    for rs, za in zip(blocks, proj):
        sub = za.shape[0]
        row = lax.broadcasted_iota(jnp.int32, (sub, 1), 0)
        prev = jnp.where(row == 0, last, pltpu.roll(za, 1, 0))
        last = za[sub - 1:sub, :]
        tail = za[sub - 8:, :]
        xs = za + (prev - za) * mu_ref[...]

        r = xs[:, 0:WIDTH]
        k = xs[:, WIDTH:2 * WIDTH]
        v = xs[:, 2 * WIDTH:3 * WIDTH]
        gate = xs[:, 3 * WIDTH:4 * WIDTH]
        wl = xs[:, 4 * WIDTH:4 * WIDTH + LORA]
        al = xs[:, 4 * WIDTH + LORA:4 * WIDTH + 2 * LORA]

        u = -(w0 + _mm3(jnp.tanh(wl), w2_ref[...]))
        softplus = jnp.maximum(u, 0.0) + jnp.log(1.0 + jnp.exp(-jnp.abs(u)))
        lw = -jnp.exp(-softplus - 0.5)
        a = _sigmoid(a0 + _mm3(al, a2_ref[...]))

        kkp = k * k_k
        kk = kkp * lax.rsqrt(jnp.maximum(_head_sum(kkp * kkp, ones_bd), 1e-24))
        k2 = k * (1.0 + (a - 1.0) * k_a)

        r_o[rs, :] = r.astype(r_o.dtype)
        lw_o[rs, :] = lw
        k_o[rs, :] = k2.astype(k_o.dtype)
        v_o[rs, :] = v.astype(v_o.dtype)
        kk_o[rs, :] = kk.astype(kk_o.dtype)
        b_o[rs, :] = (kk * a).astype(b_o.dtype)
        bonus_o[rs, :] = (_head_sum(r * k2 * r_k, ones_bd) * v).astype(bonus_o.dtype)
        gate_o[rs, :] = (gate * _sigmoid(gate)).astype(gate_o.dtype)
    carry_ref[...] = tail


def _rwkv_scan_kernel(r_ref, *refs):
    s_ref = refs[-1]

    @pl.when(pl.program_id(1) == 0)
    def _():
        s_ref[...] = jnp.zeros_like(s_ref)

    _chunk_loop(r_ref.shape[1] // CHUNK, 1, functools.partial(_rwkv_chunk, r_ref, *refs))


def _rwkv_chunk(r_ref, lw_ref, k_ref, v_ref, kk_ref, b_ref, bonus_ref, gate_ref, ln_ref, tri_ref,
                y_o, s_ref, off):
    rows = r_ref.shape[0]
    at = pl.ds(off, CHUNK)
    a_t, b_t, k_t, r_f, r_t, b_h, k_h, v_b, pend = ([] for _ in range(9))
    for j in range(rows):
        lw = lw_ref[j, at, :]
        cum = _dg(tri_ref[...], _split3(lw), _NN)
        cend = cum[CHUNK - 1:CHUNK, :]
        pinv = jnp.exp(-cum)
        dend = jnp.exp(cend - cum)
        kk, bv, k2 = kk_ref[j, at, :].astype(F32), b_ref[j, at, :].astype(F32), k_ref[j, at, :].astype(F32)
        a_t.append((-kk * jnp.exp(cum - lw)).astype(BF16))
        b_t.append((bv * pinv).astype(BF16))
        k_t.append((k2 * pinv).astype(BF16))
        r_f.append(r_ref[j, at, :].astype(F32) * jnp.exp(cum))
        r_t.append(r_f[j].astype(BF16))
        b_h.append((bv * dend).astype(BF16))
        k_h.append((k2 * dend).astype(BF16))
        v_b.append(v_ref[j, at, :])
        pend.append(jnp.exp(cend))

    P2 = 2 * HEAD_DIM
    ri = lax.broadcasted_iota(jnp.int32, (CHUNK, P2), 0)
    ci = lax.broadcasted_iota(jnp.int32, (CHUNK, P2), 1) % CHUNK
    strict = ri > ci
    incl = ri >= ci
    low = lax.broadcasted_iota(jnp.int32, (CHUNK, P2), 1) < HEAD_DIM
    own = ((lax.broadcasted_iota(jnp.int32, (P2, P2), 0) < HEAD_DIM)
           == (lax.broadcasted_iota(jnp.int32, (P2, P2), 1) < HEAD_DIM))
    units = [(j, slice(p * P2, (p + 1) * P2)) for j in range(rows) for p in range(HEADS // 2)]
    un = range(len(units))
    top, bot = slice(0, CHUNK), slice(CHUNK, 2 * CHUNK)
    left, right = slice(0, P2), slice(P2, 2 * P2)

    sc = [_dg(jnp.concatenate([a_t[j][:, ln], r_t[j][:, ln]], axis=0),
              jnp.concatenate([_block_diag(b_t[j][:, ln]), _block_diag(k_t[j][:, ln])], axis=0), _NT)
          for j, ln in units]
    m = [jnp.where(strict, sc[u][top, left], 0.0) for u in un]
    m_ak = [jnp.where(strict, sc[u][top, right], 0.0).astype(BF16) for u in un]
    l_rb = [jnp.where(incl, sc[u][bot, left], 0.0).astype(BF16) for u in un]
    l_rk = [jnp.where(incl, sc[u][bot, right], 0.0).astype(BF16) for u in un]
    v_bd = [_block_diag(v_b[j][:, ln]) for j, ln in units]

    u0 = [_dg(m_ak[u], v_bd[u], _NN).astype(BF16) for u in un]
    t = [jnp.where(ri == ci, 1.0, m[u]) for u in un]
    mb = [m[u].astype(BF16) for u in un]
    mb = [_dg(mb[u], _block_diag(mb[u]), _NN).astype(BF16) for u in un]
    for i in range(1, 6):
        if i < 5:
            tm = [_dg(jnp.concatenate([t[u].astype(BF16), mb[u]], axis=0), _block_diag(mb[u]), _NN) for u in un]
            t = [t[u] + tm[u][top, :] for u in un]
            mb = [tm[u][bot, :].astype(BF16) for u in un]
        else:
            t = [t[u] + _dg(t[u].astype(BF16), _block_diag(mb[u]), _NN) for u in un]
    xb = [_dg(t[u].astype(BF16),
              jnp.concatenate([_block_diag(a_t[j][:, ln]), _block_diag(u0[u])], axis=1), _NN).astype(BF16)
          for u, (j, ln) in enumerate(units)]
    x_bd = [jnp.concatenate([_block_diag(xb[u][:, left]), _block_diag(xb[u][:, right])], axis=1) for u in un]

    s = [s_ref[u] for u in un]
    sb = [s[u].astype(BF16) for u in un]
    lx = [_dg(l_rb[u], x_bd[u], _NN) for u in un]
    lv = [_dg(l_rk[u], v_bd[u], _NN) for u in un]
    r2 = [(r_f[j][:, ln] + lx[u][:, left]).astype(BF16) for u, (j, ln) in enumerate(units)]
    y = [_dg(r2[u], sb[u], _NT) + lx[u][:, right] + lv[u] for u in un]

    q = [jnp.where(own, _dg(xb[u][:, left], b_h[j][:, ln], _TN), 0.0).astype(BF16)
         for u, (j, ln) in enumerate(units)]
    uv = [_dg(jnp.concatenate([xb[u][:, right], v_b[j][:, ln]], axis=0),
              jnp.concatenate([b_h[j][:, ln], k_h[j][:, ln]], axis=0), _TN) for u, (j, ln) in enumerate(units)]
    sq = [_dg(sb[u], q[u], _NN) for u in un]
    for u, (j, ln) in enumerate(units):
        s_ref[u] = s[u] * pend[j][:, ln] + sq[u] + jnp.where(own, uv[u], 0.0)

    bonus = [bonus_ref[j, at, :].astype(F32) for j in range(rows)]
    gate = [gate_ref[j, at, :].astype(F32) for j in range(rows)]
    for u, (j, ln) in enumerate(units):
        yc = y[u] - _pair_mean(y[u], low)
        yn = yc * lax.rsqrt(_pair_mean(yc * yc, low) + GN_EPS) * ln_ref[0:1, ln] + ln_ref[1:2, ln]
        y_o[j, at, ln] = ((yn + bonus[j][:, ln]) * gate[j][:, ln]).astype(y_o.dtype)


def _attn_hgrn_prep_kernel(x_ref, g_ref, w_ref, qk_ref, ones_ref, lb_ref,
                           q_o, k_o, v_o, gate_o, hq_o, hk_o, hlf_o, hv_o, hgate_o, *, pad_tiles, layer):
    i = pl.program_id(1)

    @pl.when(i < pad_tiles)
    def _():
        k_o[...] = jnp.zeros_like(k_o)
        v_o[...] = jnp.zeros_like(v_o)

    @pl.when(i >= pad_tiles)
    def _():
        z = _norm_proj(x_ref[...], g_ref, w_ref)
        ones_bd = ones_ref[...]
        q = z[:, 0:WIDTH]
        k = z[:, WIDTH:2 * WIDTH]
        gate = z[:, 3 * WIDTH:4 * WIDTH]
        qn = q * lax.rsqrt(_head_sum(q * q, ones_bd) * (1.0 / HEAD_DIM) + RMS_EPS) * qk_ref[0:1, :]
        kn = k * lax.rsqrt(_head_sum(k * k, ones_bd) * (1.0 / HEAD_DIM) + RMS_EPS) * qk_ref[1:2, :]
        q_o[...] = (qn * (HEAD_DIM ** -0.5 * LOG2E)).astype(q_o.dtype)
        k_o[...] = kn.T.astype(k_o.dtype)
        v_o[...] = z[:, 2 * WIDTH:3 * WIDTH].astype(v_o.dtype)
        gate_o[...] = (gate * _sigmoid(gate)).astype(gate_o.dtype)

        lbp = lb_ref[...]
        e = jnp.exp(lbp - jnp.max(lbp, axis=0, keepdims=True))
        sm = e / jnp.sum(e, axis=0, keepdims=True)
        lb = jnp.sum(sm[0:layer + 1], axis=0, keepdims=True) - sm[0:1]
        hq = z[:, 4 * WIDTH:5 * WIDTH]
        hgate = z[:, 7 * WIDTH:8 * WIDTH]
        fg = lb + (1.0 - lb) * _sigmoid(z[:, 5 * WIDTH:6 * WIDTH])
        hq_o[...] = (hq * _sigmoid(hq)).astype(hq_o.dtype)
        hk_o[...] = (1.0 - fg).astype(hk_o.dtype)
        hlf_o[...] = jnp.log(fg)
        hv_o[...] = z[:, 6 * WIDTH:7 * WIDTH].astype(hv_o.dtype)
        hgate_o[...] = (hgate * _sigmoid(hgate)).astype(hgate_o.dtype)


def _attn_band_kernel(q_ref, *refs):
    n_chunks = q_ref.shape[0] // CHUNK
    first = pl.program_id(1) * n_chunks
    _chunk_loop(n_chunks, ATTN_Q_CHUNKS, functools.partial(_attn_group, first, q_ref, *refs))


def _attn_group(first, q_ref, k_ref, v_ref, bias_ref, gate_ref, y_o, off):
    nq = ATTN_Q_CHUNKS * CHUNK
    nk = (ATTN_Q_CHUNKS + PAST_CHUNKS) * CHUNK
    at = pl.ds(off, nq)
    n = first + off // CHUNK
    start = pl.multiple_of(first * CHUNK + off, CHUNK)
    q = q_ref[at, :]
    kt = k_ref[:, pl.ds(pl.multiple_of(start, 2 * HEAD_DIM), nk)]
    vb = v_ref[pl.ds(start, nk), :]
    var = jnp.minimum(n // ATTN_Q_CHUNKS, bias_ref.shape[0] - 1)
    gate = gate_ref[at, :].astype(F32)
    low_q = lax.broadcasted_iota(jnp.int32, (nq, 2 * HEAD_DIM), 1) < HEAD_DIM
    for pair in range(HEADS // 2):
        lanes = slice(pair * 2 * HEAD_DIM, (pair + 1) * 2 * HEAD_DIM)
        v_bd = _block_diag(vb[:, lanes])
        s2 = _dg(q[:, lanes], _block_diag_cols(kt[lanes, :]), _NN)
        p, inv = [], []
        for i in range(2):
            s = s2[:, i * nk:(i + 1) * nk] + bias_ref[var, 2 * pair + i]
            e = jnp.exp2(s - jnp.max(s, axis=-1, keepdims=True))
            p.append(e.astype(BF16))
            inv.append(1.0 / jnp.sum(e, axis=-1, keepdims=True))
        o = _dg(jnp.concatenate(p, axis=1), v_bd, _NN)
        y_o[at, lanes] = (o * jnp.where(low_q, inv[0], inv[1]) * gate[:, lanes]).astype(y_o.dtype)


def _hgrn_scan_kernel(q_ref, *refs):
    s_ref = refs[-1]

    @pl.when(pl.program_id(1) == 0)
    def _():
        s_ref[...] = jnp.zeros_like(s_ref)

    _chunk_loop(q_ref.shape[1] // CHUNK, 1, functools.partial(_hgrn_chunk, q_ref, *refs))


def _hgrn_chunk(q_ref, k_ref, lf_ref, v_ref, gate_ref, sel_ref, ones_ref, ng_ref, y_o, s_ref, off):
    rows = q_ref.shape[0]
    at = pl.ds(off, CHUNK)
    n_lv = len(LEVELS)
    row = lax.broadcasted_iota(jnp.int32, (CHUNK, 1), 0)
    second = [((row // w) % 2) == 1 for w in LEVELS]
    sign = [jnp.where(sec, 1.0, -1.0) for sec in second]

    z_lv, qk_diag, q_in, k_out, dec, v_f, v_b = ([] for _ in range(7))
    for j in range(rows):
        q, k, v_b16 = q_ref[j, at, :].astype(F32), k_ref[j, at, :].astype(F32), v_ref[j, at, :]
        sums = _dg(sel_ref[...], _split3(lf_ref[j, at, :]), _NN)
        b = sums[0:CHUNK, :]
        bend = b[CHUNK - 1:CHUNK, :]
        zl = []
        for i in range(n_lv):
            edge = sums[(i + 1) * CHUNK:(i + 2) * CHUNK, :]
            e = jnp.exp((b - edge) * sign[i])
            z = jnp.where(second[i], q, k) * e
            zl.append((z.astype(BF16), z.T.astype(BF16)))
        z_lv.append(zl)
        qk_diag.append(_head_sum(q * k, ones_ref[...]))
        q_in.append((q * jnp.exp(b)).astype(BF16))
        k_out.append((k * jnp.exp(bend - b)).astype(BF16))
        dec.append(jnp.exp(bend))
        v_f.append(v_b16.astype(F32))
        v_b.append(v_b16)

    units = [(j, slice(p * 2 * HEAD_DIM, (p + 1) * 2 * HEAD_DIM)) for j in range(rows) for p in range(HEADS // 2)]
    un = range(len(units))
    low = lax.broadcasted_iota(jnp.int32, (CHUNK, 2 * HEAD_DIM), 1) < HEAD_DIM
    ri2 = lax.broadcasted_iota(jnp.int32, (CHUNK, 2 * HEAD_DIM), 0)
    ci2 = lax.broadcasted_iota(jnp.int32, (CHUNK, 2 * HEAD_DIM), 1) % CHUNK
    keep = [((ri2 // (2 * w)) == (ci2 // (2 * w))) & ((ri2 // w) % 2 == 1) & ((ci2 // w) % 2 == 0) for w in LEVELS]
    v_bd = [_block_diag(v_b[j][:, ln]) for j, ln in units]
    att = []
    for j, ln in units:
        pr = [_dg(z[:, ln], _block_diag_cols(zt[ln, :]), _NN) for z, zt in z_lv[j]]
        att.append(sum(jnp.where(keep[i], pr[i], 0.0) for i in range(n_lv)))
    s = [s_ref[u] for u in un]
    o_in = [_dg(q_in[j][:, ln], s[u].astype(BF16), _NT) for u, (j, ln) in enumerate(units)]
    o_at = [_dg(att[u].astype(BF16), v_bd[u], _NN) for u in un]
    vk = [_dg(v_b[j][:, ln], k_out[j][:, ln], _TN) for j, ln in units]
    own = ((lax.broadcasted_iota(jnp.int32, (2 * HEAD_DIM, 2 * HEAD_DIM), 0) < HEAD_DIM)
           == (lax.broadcasted_iota(jnp.int32, (2 * HEAD_DIM, 2 * HEAD_DIM), 1) < HEAD_DIM))
    for u, (j, ln) in enumerate(units):
        s_ref[u] = s[u] * dec[j][:, ln] + jnp.where(own, vk[u], 0.0)
    gate = [gate_ref[j, at, :].astype(F32) for j in range(rows)]
    for u, (j, ln) in enumerate(units):
        o = o_at[u] + o_in[u] + qk_diag[j][:, ln] * v_f[j][:, ln]
        o = o * lax.rsqrt(_pair_mean(o * o, low) + RMS_EPS) * ng_ref[:, ln]
        y_o[j, at, ln] = (o * gate[j][:, ln]).astype(y_o.dtype)


def _merge_kernel(x_ref, g_ref, wg_ref, ya_ref, yb_ref, yc_ref, pa_ref, pb_ref, pc_ref, wo_ref, o_ref):
    gates = _sigmoid(_norm_proj(x_ref[...], g_ref, wg_ref))
    m = (gates[:, 0:D_MODEL] * jnp.dot(ya_ref[...], pa_ref[...], preferred_element_type=F32)
         + gates[:, D_MODEL:2 * D_MODEL] * jnp.dot(yb_ref[...], pb_ref[...], preferred_element_type=F32)
         + gates[:, 2 * D_MODEL:3 * D_MODEL] * jnp.dot(yc_ref[...], pc_ref[...], preferred_element_type=F32))
    o_ref[...] = x_ref[...] + jnp.dot(m.astype(BF16), wo_ref[...], preferred_element_type=F32)


def _params(*sem):
    return pltpu.CompilerParams(dimension_semantics=sem, vmem_limit_bytes=VMEM_LIMIT)


def _full(shape):
    return pl.BlockSpec(shape, lambda *_: (0,) * len(shape))


def _edge_selectors():
    tri = np.tril(np.ones((CHUNK, CHUNK), np.float32))
    pick = np.zeros((len(LEVELS) * CHUNK, CHUNK), np.float32)
    for i, w in enumerate(LEVELS):
        for t in range(CHUNK):
            pick[i * CHUNK + t, (t // (2 * w)) * 2 * w + w - 1] = 1.0
    sel = np.concatenate([tri, pick @ tri], axis=0)
    return jnp.asarray(np.concatenate([sel, sel, sel], axis=1), BF16)


def _band_bias(rel_bias):
    nq = ATTN_Q_CHUNKS * CHUNK
    nk = (ATTN_Q_CHUNKS + PAST_CHUNKS) * CHUNK
    period = nk + nq
    j = np.arange(period)
    far = REL_CLIP + CHUNK - 1
    w = rel_bias.astype(F32)[:, np.where(j < nk, np.clip(PAD_ROWS - j, -(CHUNK - 1), REL_CLIP) + CHUNK - 1, far)]
    heads = w.shape[0]
    bias = jnp.tile(w, (1, nq))[:, :nq * (period - 1)].reshape(heads, nq, period - 1)[:, :, :nk] * LOG2E
    qc = np.arange(nq)[:, None] // CHUNK
    kc = np.arange(nk)[None, :] // CHUNK
    band = (kc >= qc) & (kc <= qc + PAST_CHUNKS)
    first_chunk = np.arange(0, PAST_CHUNKS + ATTN_Q_CHUNKS, ATTN_Q_CHUNKS)
    keep = band[None] & (kc[None] >= PAST_CHUNKS - first_chunk[:, None, None])
    return jnp.where(keep[:, None], bias[None], -jnp.inf)


def _layer(x, l, p, consts):
    bsz, seq, _ = x.shape
    tm = min(TOKEN_TILE, seq)
    n_t = seq // tm
    n_c = seq // CHUNK
    ones_bd, sel = consts
    g = p['norm_g'][l][None, :]

    tma = min(RWKV_TOKEN_TILE, seq)
    tok = lambda w: pl.BlockSpec((None, tm, w), lambda b, t: (b, t, 0))
    toka = lambda w: pl.BlockSpec((None, tma, w), lambda b, t: (b, t, 0))
    rows_a, rows_c = min(RWKV_ROWS, bsz), min(HGRN_ROWS, bsz)
    blk = min(SCAN_CHUNKS, n_c) * CHUNK
    qblk = min(ATTN_CHUNKS, n_c) * CHUNK
    chk = lambda w, rows: pl.BlockSpec((rows, blk, w), lambda b, c: (b, c, 0))
    qck = lambda w: pl.BlockSpec((None, qblk, w), lambda b, c: (b, c, 0))
    act = lambda dt: jax.ShapeDtypeStruct((bsz, seq, WIDTH), dt)

    w_in = p['w_in'][l]
    spans = ((0, WIDTH), (WIDTH + LORA, 2 * WIDTH + LORA), (2 * WIDTH + LORA, 3 * WIDTH + LORA),
             (3 * WIDTH + 2 * LORA, 4 * WIDTH + 2 * LORA), (WIDTH, WIDTH + LORA),
             (3 * WIDTH + LORA, 3 * WIDTH + 2 * LORA))
    reorder = lambda t: jnp.concatenate([t[..., a:b] for a, b in spans], axis=-1)
    pa = jnp.stack([p['rwkv_w0'][l], p['rwkv_a0'][l], p['rwkv_k_k'][l], p['rwkv_k_a'][l],
                    p['rwkv_r_k'][l].reshape(WIDTH)] + [jnp.zeros((WIDTH,), F32)] * 3)
    r, lw, k2, v, kk, bv, bonus, gate_a = pl.pallas_call(
        _rwkv_prep_kernel,
        grid=(bsz, seq // tma),
        in_specs=[toka(D_MODEL), _full((1, D_MODEL)), _full((D_MODEL, A_COLS)), _full((1, A_COLS)),
                  _full((8, WIDTH)), _full((LORA, WIDTH)), _full((LORA, WIDTH)), _full((WIDTH, WIDTH))],
        out_specs=[toka(WIDTH)] * 8,
        out_shape=[act(BF16), act(F32)] + [act(BF16)] * 6,
        scratch_shapes=[pltpu.VMEM((8, A_COLS), F32)],
        compiler_params=_params("parallel", "arbitrary"),
        name="rwkv_prep",
    )(x, g, reorder(w_in).astype(BF16), reorder(p['rwkv_mu'][l])[None, :], pa,
      p['rwkv_w2'][l], p['rwkv_a2'][l], ones_bd)

    ln = jnp.stack([p['rwkv_ln_w'][l], p['rwkv_ln_b'][l]] + [jnp.zeros((WIDTH,), F32)] * 6)
    y_a = pl.pallas_call(
        _rwkv_scan_kernel,
        grid=(bsz // rows_a, seq // blk),
        in_specs=[chk(WIDTH, rows_a)] * 8 + [_full((8, WIDTH)), _full((CHUNK, 3 * CHUNK))],
        out_specs=chk(WIDTH, rows_a),
        out_shape=act(BF16),
        scratch_shapes=[pltpu.VMEM((rows_a * HEADS // 2, 2 * HEAD_DIM, 2 * HEAD_DIM), F32)],
        compiler_params=_params("parallel", "arbitrary"),
        name="rwkv_scan",
    )(r, lw, k2, v, kk, bv, bonus, gate_a, ln, sel[:CHUNK])

    pad_tiles = PAD_ROWS // tm
    wbc = w_in[:, A_COLS:A_COLS + 8 * WIDTH].astype(BF16)
    qk = jnp.stack([jnp.tile(p['attn_q_norm'][l], HEADS), jnp.tile(p['attn_k_norm'][l], HEADS)]
                   + [jnp.zeros((WIDTH,), F32)] * 6)
    n_layers = p['hgrn_lb'].shape[0]
    shift = lambda b, i: (b, jnp.maximum(i - pad_tiles, 0), 0)
    late = pl.BlockSpec((None, tm, WIDTH), shift)
    q, k_pad, v_pad, gate_b, qc, kc, lf, vc, gate_c = pl.pallas_call(
        functools.partial(_attn_hgrn_prep_kernel, pad_tiles=pad_tiles, layer=l),
        grid=(bsz, n_t + pad_tiles),
        in_specs=[pl.BlockSpec((None, tm, D_MODEL), shift), _full((1, D_MODEL)),
                  _full((D_MODEL, 8 * WIDTH)), _full((8, WIDTH)), _full((WIDTH, WIDTH)),
                  _full((n_layers, WIDTH))],
        out_specs=[late, pl.BlockSpec((None, WIDTH, tm), lambda b, i: (b, 0, i)), tok(WIDTH), late,
                   late, late, late, late, late],
        out_shape=[act(BF16), jax.ShapeDtypeStruct((bsz, WIDTH, seq + PAD_ROWS), BF16),
                   jax.ShapeDtypeStruct((bsz, seq + PAD_ROWS, WIDTH), BF16), act(BF16),
                   act(BF16), act(BF16), act(F32), act(BF16), act(BF16)],
        compiler_params=_params("parallel", "arbitrary"),
        name="attn_hgrn_prep",
    )(x, g, wbc, qk, ones_bd, p['hgrn_lb'])

    band = pl.BlockSpec((None, seq + PAD_ROWS, WIDTH), lambda b, c: (b, 0, 0))
    band_t = pl.BlockSpec((None, WIDTH, seq + PAD_ROWS), lambda b, c: (b, 0, 0))
    bias = _band_bias(p['attn_rel_bias'][l])
    y_b = pl.pallas_call(
        _attn_band_kernel,
        grid=(bsz, seq // qblk),
        in_specs=[qck(WIDTH), band_t, band,
                  pl.BlockSpec(bias.shape, lambda b, c: (0, 0, 0, 0), pipeline_mode=pl.Buffered(1)),
                  qck(WIDTH)],
        out_specs=qck(WIDTH),
        out_shape=act(BF16),
        compiler_params=_params("parallel", "arbitrary"),
        name="attn_band",
    )(q, k_pad, v_pad, bias, gate_b)

    y_c = pl.pallas_call(
        _hgrn_scan_kernel,
        grid=(bsz // rows_c, seq // blk),
        in_specs=[chk(WIDTH, rows_c)] * 5 + [_full(sel.shape), _full((WIDTH, WIDTH)), _full((1, WIDTH))],
        out_specs=chk(WIDTH, rows_c),
        out_shape=act(BF16),
        scratch_shapes=[pltpu.VMEM((rows_c * HEADS // 2, 2 * HEAD_DIM, 2 * HEAD_DIM), F32)],
        compiler_params=_params("parallel", "arbitrary"),
        name="hgrn_scan",
    )(qc, kc, lf, vc, gate_c, sel, ones_bd, jnp.tile(p['hgrn_norm'][l], HEADS)[None, :])

    wg = w_in[:, A_COLS + 8 * WIDTH:].astype(BF16)
    return pl.pallas_call(
        _merge_kernel,
        grid=(bsz, n_t),
        in_specs=[tok(D_MODEL), _full((1, D_MODEL)), _full((D_MODEL, 3 * D_MODEL)),
                  tok(WIDTH), tok(WIDTH), tok(WIDTH),
                  _full((WIDTH, D_MODEL)), _full((WIDTH, D_MODEL)), _full((WIDTH, D_MODEL)),
                  _full((D_MODEL, D_MODEL))],
        out_specs=tok(D_MODEL),
        out_shape=jax.ShapeDtypeStruct(x.shape, x.dtype),
        compiler_params=_params("parallel", "parallel"),
        name="merge",
    )(x, g, wg, y_a, y_b, y_c, p['proj_a'][l].astype(BF16), p['proj_b'][l].astype(BF16),
      p['proj_c'][l].astype(BF16), p['w_out'][l].astype(BF16))


def kernel(x, norm_g, w_in, rwkv_mu, rwkv_w0, rwkv_w2, rwkv_a0, rwkv_a2, rwkv_k_k, rwkv_k_a, rwkv_r_k,
           rwkv_ln_w, rwkv_ln_b, attn_q_norm, attn_k_norm, attn_rel_bias, hgrn_lb, hgrn_norm,
           proj_a, proj_b, proj_c, w_out):
    p = dict(norm_g=norm_g, w_in=w_in, rwkv_mu=rwkv_mu, rwkv_w0=rwkv_w0, rwkv_w2=rwkv_w2,
             rwkv_a0=rwkv_a0, rwkv_a2=rwkv_a2, rwkv_k_k=rwkv_k_k, rwkv_k_a=rwkv_k_a, rwkv_r_k=rwkv_r_k,
             rwkv_ln_w=rwkv_ln_w, rwkv_ln_b=rwkv_ln_b, attn_q_norm=attn_q_norm, attn_k_norm=attn_k_norm,
             attn_rel_bias=attn_rel_bias, hgrn_lb=hgrn_lb, hgrn_norm=hgrn_norm,
             proj_a=proj_a, proj_b=proj_b, proj_c=proj_c, w_out=w_out)
    head = np.arange(WIDTH) // HEAD_DIM
    ones_bd = jnp.asarray(head[:, None] == head[None, :], BF16)
    consts = (ones_bd, _edge_selectors())
    for l in range(norm_g.shape[0]):
        x = _layer(x, l, p, consts)
    return x
```

```python
import functools

import numpy as np
import jax
import jax.numpy as jnp
from jax import lax
from jax.experimental import pallas as pl
from jax.experimental.pallas import tpu as pltpu

F32 = jnp.float32
BF16 = jnp.bfloat16

D_MODEL = 1024
CHUNK = 64
HEAD_DIM = 64
HEADS = 8
WIDTH = HEADS * HEAD_DIM
LORA = 64
A_COLS = 4 * WIDTH + 2 * LORA
PAST_CHUNKS = 8
PAD_ROWS = PAST_CHUNKS * CHUNK
REL_CLIP = 256
RMS_EPS = 1e-6
GN_EPS = 64e-5
LOG2E = 1.4426950408889634
LEVELS = (32, 16, 8, 4, 2, 1)

TOKEN_TILE = 512
RWKV_TOKEN_TILE = 1024
PREP_SUB_ROWS = 256
RWKV_ROWS = 8
HGRN_ROWS = 8
SCAN_CHUNKS = 4
ATTN_CHUNKS = 8
ATTN_Q_CHUNKS = 4
VMEM_LIMIT = 56 * 1024 * 1024

_NN = (((1,), (0,)), ((), ()))
_NT = (((1,), (1,)), ((), ()))
_TN = (((0,), (0,)), ((), ()))


def _dg(a, b, dims):
    return lax.dot_general(a, b, dims, preferred_element_type=F32)


def _hi_lo(a):
    hi = a.astype(BF16)
    lo = (a - hi.astype(F32)).astype(BF16)
    return hi, lo


def _mm3(a, b, dims=_NN):
    a1, a2 = _hi_lo(a)
    b1, b2 = _hi_lo(b)
    return _dg(a1, b1, dims) + (_dg(a1, b2, dims) + _dg(a2, b1, dims))


def _head_sum(a, ones_bd):
    return _dg(a.astype(BF16), ones_bd, _NN)


def _block_diag(x):
    low = lax.broadcasted_iota(jnp.int32, x.shape, 1) < HEAD_DIM
    zero = jnp.zeros_like(x)
    return jnp.concatenate([jnp.where(low, x, zero), jnp.where(low, zero, x)], axis=0)


def _block_diag_cols(xt):
    top = lax.broadcasted_iota(jnp.int32, xt.shape, 0) < HEAD_DIM
    zero = jnp.zeros_like(xt)
    return jnp.concatenate([jnp.where(top, xt, zero), jnp.where(top, zero, xt)], axis=1)


def _pair_mean(a, low):
    zero = jnp.zeros_like(a)
    lo_sum = jnp.sum(jnp.where(low, a, zero), axis=-1, keepdims=True)
    hi_sum = jnp.sum(jnp.where(low, zero, a), axis=-1, keepdims=True)
    return jnp.where(low, lo_sum, hi_sum) * (1.0 / HEAD_DIM)


def _sigmoid(x):
    return 0.5 * jnp.tanh(0.5 * x) + 0.5


def _split3(x):
    p1 = x.astype(BF16)
    rem = x - p1.astype(F32)
    p2 = rem.astype(BF16)
    p3 = (rem - p2.astype(F32)).astype(BF16)
    return jnp.concatenate([p1, p2, p3], axis=0)


def _chunk_loop(n_chunks, step, body):
    def it(i, carry):
        body(pl.multiple_of(i * (step * CHUNK), step * CHUNK))
        return carry
    lax.fori_loop(0, n_chunks // step, it, 0)


def _sub_blocks(n_rows):
    sub = min(PREP_SUB_ROWS, n_rows)
    return [slice(i * sub, (i + 1) * sub) for i in range(n_rows // sub)]


def _norm_proj(x, g_ref, w_ref):
    ms = jnp.mean(x * x, axis=-1, keepdims=True)
    h = x * lax.rsqrt(ms + RMS_EPS) * g_ref[...]
    return jnp.dot(h.astype(BF16), w_ref[...], preferred_element_type=F32)


def _rwkv_prep_kernel(x_ref, g_ref, w_ref, mu_ref, pa_ref, w2_ref, a2_ref, ones_ref,
                      r_o, lw_o, k_o, v_o, kk_o, b_o, bonus_o, gate_o, carry_ref):
    @pl.when(pl.program_id(1) == 0)
    def _():
        carry_ref[...] = jnp.zeros_like(carry_ref)

    w0, a0, k_k, k_a, r_k = (pa_ref[i:i + 1, :] for i in range(5))
    ones_bd = ones_ref[...]
    last = carry_ref[7:8, :]
    blocks = _sub_blocks(x_ref.shape[0])
    proj = [_norm_proj(x_ref[rs, :], g_ref, w_ref) for rs in blocks]
    for rs, za in zip(blocks, proj):
        sub = za.shape[0]
        row = lax.broadcasted_iota(jnp.int32, (sub, 1), 0)
        prev = jnp.where(row == 0, last, pltpu.roll(za, 1, 0))
        last = za[sub - 1:sub, :]
        tail = za[sub - 8:, :]
        xs = za + (prev - za) * mu_ref[...]

        r = xs[:, 0:WIDTH]
        k = xs[:, WIDTH:2 * WIDTH]
        v = xs[:, 2 * WIDTH:3 * WIDTH]
        gate = xs[:, 3 * WIDTH:4 * WIDTH]
        wl = xs[:, 4 * WIDTH:4 * WIDTH + LORA]
        al = xs[:, 4 * WIDTH + LORA:4 * WIDTH + 2 * LORA]

        u = -(w0 + _mm3(jnp.tanh(wl), w2_ref[...]))
        softplus = jnp.maximum(u, 0.0) + jnp.log(1.0 + jnp.exp(-jnp.abs(u)))
        lw = -jnp.exp(-softplus - 0.5)
        a = _sigmoid(a0 + _mm3(al, a2_ref[...]))

        kkp = k * k_k
        kk = kkp * lax.rsqrt(jnp.maximum(_head_sum(kkp * kkp, ones_bd), 1e-24))
        k2 = k * (1.0 + (a - 1.0) * k_a)

        r_o[rs, :] = r.astype(r_o.dtype)
        lw_o[rs, :] = lw
        k_o[rs, :] = k2.astype(k_o.dtype)
        v_o[rs, :] = v.astype(v_o.dtype)
        kk_o[rs, :] = kk.astype(kk_o.dtype)
        b_o[rs, :] = (kk * a).astype(b_o.dtype)
        bonus_o[rs, :] = (_head_sum(r * k2 * r_k, ones_bd) * v).astype(bonus_o.dtype)
        gate_o[rs, :] = (gate * _sigmoid(gate)).astype(gate_o.dtype)
    carry_ref[...] = tail


def _rwkv_scan_kernel(r_ref, *refs):
    s_ref = refs[-1]

    @pl.when(pl.program_id(1) == 0)
    def _():
        s_ref[...] = jnp.zeros_like(s_ref)

    _chunk_loop(r_ref.shape[1] // CHUNK, 1, functools.partial(_rwkv_chunk, r_ref, *refs))


def _rwkv_chunk(r_ref, lw_ref, k_ref, v_ref, kk_ref, b_ref, bonus_ref, gate_ref, ln_ref, tri_ref,
                y_o, s_ref, off):
    rows = r_ref.shape[0]
    at = pl.ds(off, CHUNK)
    a_t, b_t, k_t, r_f, r_t, b_h, k_h, v_b, pend = ([] for _ in range(9))
    for j in range(rows):
        lw = lw_ref[j, at, :]
        cum = _dg(tri_ref[...], _split3(lw), _NN)
        cend = cum[CHUNK - 1:CHUNK, :]
        pinv = jnp.exp(-cum)
        dend = jnp.exp(cend - cum)
        kk, bv, k2 = kk_ref[j, at, :].astype(F32), b_ref[j, at, :].astype(F32), k_ref[j, at, :].astype(F32)
        a_t.append((-kk * jnp.exp(cum - lw)).astype(BF16))
        b_t.append((bv * pinv).astype(BF16))
        k_t.append((k2 * pinv).astype(BF16))
        r_f.append(r_ref[j, at, :].astype(F32) * jnp.exp(cum))
        r_t.append(r_f[j].astype(BF16))
        b_h.append((bv * dend).astype(BF16))
        k_h.append((k2 * dend).astype(BF16))
        v_b.append(v_ref[j, at, :])
        pend.append(jnp.exp(cend))

    P2 = 2 * HEAD_DIM
    ri = lax.broadcasted_iota(jnp.int32, (CHUNK, P2), 0)
    ci = lax.broadcasted_iota(jnp.int32, (CHUNK, P2), 1) % CHUNK
    strict = ri > ci
    incl = ri >= ci
    low = lax.broadcasted_iota(jnp.int32, (CHUNK, P2), 1) < HEAD_DIM
    own = ((lax.broadcasted_iota(jnp.int32, (P2, P2), 0) < HEAD_DIM)
           == (lax.broadcasted_iota(jnp.int32, (P2, P2), 1) < HEAD_DIM))
    units = [(j, slice(p * P2, (p + 1) * P2)) for j in range(rows) for p in range(HEADS // 2)]
    un = range(len(units))
    top, bot = slice(0, CHUNK), slice(CHUNK, 2 * CHUNK)
    left, right = slice(0, P2), slice(P2, 2 * P2)

    sc = [_dg(jnp.concatenate([a_t[j][:, ln], r_t[j][:, ln]], axis=0),
              jnp.concatenate([_block_diag(b_t[j][:, ln]), _block_diag(k_t[j][:, ln])], axis=0), _NT)
          for j, ln in units]
    m = [jnp.where(strict, sc[u][top, left], 0.0) for u in un]
    m_ak = [jnp.where(strict, sc[u][top, right], 0.0).astype(BF16) for u in un]
    l_rb = [jnp.where(incl, sc[u][bot, left], 0.0).astype(BF16) for u in un]
    l_rk = [jnp.where(incl, sc[u][bot, right], 0.0).astype(BF16) for u in un]
    v_bd = [_block_diag(v_b[j][:, ln]) for j, ln in units]

    u0 = [_dg(m_ak[u], v_bd[u], _NN).astype(BF16) for u in un]
    t = [jnp.where(ri == ci, 1.0, m[u]) for u in un]
    mb = [m[u].astype(BF16) for u in un]
    mb = [_dg(mb[u], _block_diag(mb[u]), _NN).astype(BF16) for u in un]
    for i in range(1, 6):
        if i < 5:
            tm = [_dg(jnp.concatenate([t[u].astype(BF16), mb[u]], axis=0), _block_diag(mb[u]), _NN) for u in un]
            t = [t[u] + tm[u][top, :] for u in un]
            mb = [tm[u][bot, :].astype(BF16) for u in un]
        else:
            t = [t[u] + _dg(t[u].astype(BF16), _block_diag(mb[u]), _NN) for u in un]
    xb = [_dg(t[u].astype(BF16),
              jnp.concatenate([_block_diag(a_t[j][:, ln]), _block_diag(u0[u])], axis=1), _NN).astype(BF16)
          for u, (j, ln) in enumerate(units)]
    x_bd = [jnp.concatenate([_block_diag(xb[u][:, left]), _block_diag(xb[u][:, right])], axis=1) for u in un]

    s = [s_ref[u] for u in un]
    sb = [s[u].astype(BF16) for u in un]
    lx = [_dg(l_rb[u], x_bd[u], _NN) for u in un]
    lv = [_dg(l_rk[u], v_bd[u], _NN) for u in un]
    r2 = [(r_f[j][:, ln] + lx[u][:, left]).astype(BF16) for u, (j, ln) in enumerate(units)]
    y = [_dg(r2[u], sb[u], _NT) + lx[u][:, right] + lv[u] for u in un]

    q = [jnp.where(own, _dg(xb[u][:, left], b_h[j][:, ln], _TN), 0.0).astype(BF16)
         for u, (j, ln) in enumerate(units)]
    uv = [_dg(jnp.concatenate([xb[u][:, right], v_b[j][:, ln]], axis=0),
              jnp.concatenate([b_h[j][:, ln], k_h[j][:, ln]], axis=0), _TN) for u, (j, ln) in enumerate(units)]
    sq = [_dg(sb[u], q[u], _NN) for u in un]
    for u, (j, ln) in enumerate(units):
        s_ref[u] = s[u] * pend[j][:, ln] + sq[u] + jnp.where(own, uv[u], 0.0)

    bonus = [bonus_ref[j, at, :].astype(F32) for j in range(rows)]
    gate = [gate_ref[j, at, :].astype(F32) for j in range(rows)]
    for u, (j, ln) in enumerate(units):
        yc = y[u] - _pair_mean(y[u], low)
        yn = yc * lax.rsqrt(_pair_mean(yc * yc, low) + GN_EPS) * ln_ref[0:1, ln] + ln_ref[1:2, ln]
        y_o[j, at, ln] = ((yn + bonus[j][:, ln]) * gate[j][:, ln]).astype(y_o.dtype)


def _attn_hgrn_prep_kernel(x_ref, g_ref, w_ref, qk_ref, ones_ref, lb_ref,
                           q_o, k_o, v_o, gate_o, hq_o, hk_o, hlf_o, hv_o, hgate_o, *, pad_tiles, layer):
    i = pl.program_id(1)

    @pl.when(i < pad_tiles)
    def _():
        k_o[...] = jnp.zeros_like(k_o)
        v_o[...] = jnp.zeros_like(v_o)

    @pl.when(i >= pad_tiles)
    def _():
        z = _norm_proj(x_ref[...], g_ref, w_ref)
        ones_bd = ones_ref[...]
        q = z[:, 0:WIDTH]
        k = z[:, WIDTH:2 * WIDTH]
        gate = z[:, 3 * WIDTH:4 * WIDTH]
        qn = q * lax.rsqrt(_head_sum(q * q, ones_bd) * (1.0 / HEAD_DIM) + RMS_EPS) * qk_ref[0:1, :]
        kn = k * lax.rsqrt(_head_sum(k * k, ones_bd) * (1.0 / HEAD_DIM) + RMS_EPS) * qk_ref[1:2, :]
        q_o[...] = (qn * (HEAD_DIM ** -0.5 * LOG2E)).astype(q_o.dtype)
        k_o[...] = kn.T.astype(k_o.dtype)
        v_o[...] = z[:, 2 * WIDTH:3 * WIDTH].astype(v_o.dtype)
        gate_o[...] = (gate * _sigmoid(gate)).astype(gate_o.dtype)

        lbp = lb_ref[...]
        e = jnp.exp(lbp - jnp.max(lbp, axis=0, keepdims=True))
        sm = e / jnp.sum(e, axis=0, keepdims=True)
        lb = jnp.sum(sm[0:layer + 1], axis=0, keepdims=True) - sm[0:1]
        hq = z[:, 4 * WIDTH:5 * WIDTH]
        hgate = z[:, 7 * WIDTH:8 * WIDTH]
        fg = lb + (1.0 - lb) * (1.0 / (1.0 + jnp.exp(-z[:, 5 * WIDTH:6 * WIDTH])))
        hq_o[...] = (hq * _sigmoid(hq)).astype(hq_o.dtype)
        hk_o[...] = (1.0 - fg).astype(hk_o.dtype)
        hlf_o[...] = jnp.log(fg)
        hv_o[...] = z[:, 6 * WIDTH:7 * WIDTH].astype(hv_o.dtype)
        hgate_o[...] = (hgate * _sigmoid(hgate)).astype(hgate_o.dtype)


def _attn_band_kernel(q_ref, *refs):
    n_chunks = q_ref.shape[0] // CHUNK
    first = pl.program_id(1) * n_chunks
    _chunk_loop(n_chunks, ATTN_Q_CHUNKS, functools.partial(_attn_group, first, q_ref, *refs))


def _attn_group(first, q_ref, k_ref, v_ref, bias_ref, gate_ref, y_o, off):
    nq = ATTN_Q_CHUNKS * CHUNK
    nk = (ATTN_Q_CHUNKS + PAST_CHUNKS) * CHUNK
    at = pl.ds(off, nq)
    n = first + off // CHUNK
    start = pl.multiple_of(first * CHUNK + off, CHUNK)
    q = q_ref[at, :]
    kt = k_ref[:, pl.ds(pl.multiple_of(start, 2 * HEAD_DIM), nk)]
    vb = v_ref[pl.ds(start, nk), :]
    var = jnp.minimum(n // ATTN_Q_CHUNKS, bias_ref.shape[0] - 1)
    gate = gate_ref[at, :].astype(F32)
    low_q = lax.broadcasted_iota(jnp.int32, (nq, 2 * HEAD_DIM), 1) < HEAD_DIM
    for pair in range(HEADS // 2):
        lanes = slice(pair * 2 * HEAD_DIM, (pair + 1) * 2 * HEAD_DIM)
        v_bd = _block_diag(vb[:, lanes])
        s2 = _dg(q[:, lanes], _block_diag_cols(kt[lanes, :]), _NN)
        p, inv = [], []
        for i in range(2):
            s = s2[:, i * nk:(i + 1) * nk] + bias_ref[var, 2 * pair + i]
            e = jnp.exp2(s - jnp.max(s, axis=-1, keepdims=True))
            p.append(e.astype(BF16))
            inv.append(1.0 / jnp.sum(e, axis=-1, keepdims=True))
        o = _dg(jnp.concatenate(p, axis=1), v_bd, _NN)
        y_o[at, lanes] = (o * jnp.where(low_q, inv[0], inv[1]) * gate[:, lanes]).astype(y_o.dtype)


def _hgrn_scan_kernel(q_ref, *refs):
    s_ref = refs[-1]

    @pl.when(pl.program_id(1) == 0)
    def _():
        s_ref[...] = jnp.zeros_like(s_ref)

    _chunk_loop(q_ref.shape[1] // CHUNK, 1, functools.partial(_hgrn_chunk, q_ref, *refs))


def _hgrn_chunk(q_ref, k_ref, lf_ref, v_ref, gate_ref, sel_ref, ones_ref, ng_ref, y_o, s_ref, off):
    rows = q_ref.shape[0]
    at = pl.ds(off, CHUNK)
    n_lv = len(LEVELS)
    row = lax.broadcasted_iota(jnp.int32, (CHUNK, 1), 0)
    second = [((row // w) % 2) == 1 for w in LEVELS]
    sign = [jnp.where(sec, 1.0, -1.0) for sec in second]

    z_lv, qk_diag, q_in, k_out, dec, v_f, v_b = ([] for _ in range(7))
    for j in range(rows):
        q, k, v_b16 = q_ref[j, at, :].astype(F32), k_ref[j, at, :].astype(F32), v_ref[j, at, :]
        sums = _dg(sel_ref[...], _split3(lf_ref[j, at, :]), _NN)
        b = sums[0:CHUNK, :]
        bend = b[CHUNK - 1:CHUNK, :]
        zl = []
        for i in range(n_lv):
            edge = sums[(i + 1) * CHUNK:(i + 2) * CHUNK, :]
            e = jnp.exp((b - edge) * sign[i])
            z = jnp.where(second[i], q, k) * e
            zl.append((z.astype(BF16), z.T.astype(BF16)))
        z_lv.append(zl)
        qk_diag.append(_head_sum(q * k, ones_ref[...]))
        q_in.append((q * jnp.exp(b)).astype(BF16))
        k_out.append((k * jnp.exp(bend - b)).astype(BF16))
        dec.append(jnp.exp(bend))
        v_f.append(v_b16.astype(F32))
        v_b.append(v_b16)

    units = [(j, slice(p * 2 * HEAD_DIM, (p + 1) * 2 * HEAD_DIM)) for j in range(rows) for p in range(HEADS // 2)]
    un = range(len(units))
    low = lax.broadcasted_iota(jnp.int32, (CHUNK, 2 * HEAD_DIM), 1) < HEAD_DIM
    ri2 = lax.broadcasted_iota(jnp.int32, (CHUNK, 2 * HEAD_DIM), 0)
    ci2 = lax.broadcasted_iota(jnp.int32, (CHUNK, 2 * HEAD_DIM), 1) % CHUNK
    keep = [((ri2 // (2 * w)) == (ci2 // (2 * w))) & ((ri2 // w) % 2 == 1) & ((ci2 // w) % 2 == 0) for w in LEVELS]
    v_bd = [_block_diag(v_b[j][:, ln]) for j, ln in units]
    att = []
    for j, ln in units:
        pr = [_dg(z[:, ln], _block_diag_cols(zt[ln, :]), _NN) for z, zt in z_lv[j]]
        att.append(sum(jnp.where(keep[i], pr[i], 0.0) for i in range(n_lv)))
    s = [s_ref[u] for u in un]
    o_in = [_dg(q_in[j][:, ln], s[u].astype(BF16), _NT) for u, (j, ln) in enumerate(units)]
    o_at = [_dg(att[u].astype(BF16), v_bd[u], _NN) for u in un]
    vk = [_dg(v_b[j][:, ln], k_out[j][:, ln], _TN) for j, ln in units]
    own = ((lax.broadcasted_iota(jnp.int32, (2 * HEAD_DIM, 2 * HEAD_DIM), 0) < HEAD_DIM)
           == (lax.broadcasted_iota(jnp.int32, (2 * HEAD_DIM, 2 * HEAD_DIM), 1) < HEAD_DIM))
    for u, (j, ln) in enumerate(units):
        s_ref[u] = s[u] * dec[j][:, ln] + jnp.where(own, vk[u], 0.0)
    gate = [gate_ref[j, at, :].astype(F32) for j in range(rows)]
    for u, (j, ln) in enumerate(units):
        o = o_at[u] + o_in[u] + qk_diag[j][:, ln] * v_f[j][:, ln]
        o = o * lax.rsqrt(_pair_mean(o * o, low) + RMS_EPS) * ng_ref[:, ln]
        y_o[j, at, ln] = (o * gate[j][:, ln]).astype(y_o.dtype)


def _merge_kernel(x_ref, g_ref, wg_ref, ya_ref, yb_ref, yc_ref, pa_ref, pb_ref, pc_ref, wo_ref, o_ref):
    gates = _sigmoid(_norm_proj(x_ref[...], g_ref, wg_ref))
    m = (gates[:, 0:D_MODEL] * jnp.dot(ya_ref[...], pa_ref[...], preferred_element_type=F32)
         + gates[:, D_MODEL:2 * D_MODEL] * jnp.dot(yb_ref[...], pb_ref[...], preferred_element_type=F32)
         + gates[:, 2 * D_MODEL:3 * D_MODEL] * jnp.dot(yc_ref[...], pc_ref[...], preferred_element_type=F32))
    o_ref[...] = x_ref[...] + jnp.dot(m.astype(BF16), wo_ref[...], preferred_element_type=F32)


def _params(*sem):
    return pltpu.CompilerParams(dimension_semantics=sem, vmem_limit_bytes=VMEM_LIMIT)


def _full(shape):
    return pl.BlockSpec(shape, lambda *_: (0,) * len(shape))


def _edge_selectors():
    tri = np.tril(np.ones((CHUNK, CHUNK), np.float32))
    pick = np.zeros((len(LEVELS) * CHUNK, CHUNK), np.float32)
    for i, w in enumerate(LEVELS):
        for t in range(CHUNK):
            pick[i * CHUNK + t, (t // (2 * w)) * 2 * w + w - 1] = 1.0
    sel = np.concatenate([tri, pick @ tri], axis=0)
    return jnp.asarray(np.concatenate([sel, sel, sel], axis=1), BF16)


def _band_bias(rel_bias):
    nq = ATTN_Q_CHUNKS * CHUNK
    nk = (ATTN_Q_CHUNKS + PAST_CHUNKS) * CHUNK
    period = nk + nq
    j = np.arange(period)
    far = REL_CLIP + CHUNK - 1
    w = rel_bias.astype(F32)[:, np.where(j < nk, np.clip(PAD_ROWS - j, -(CHUNK - 1), REL_CLIP) + CHUNK - 1, far)]
    heads = w.shape[0]
    bias = jnp.tile(w, (1, nq))[:, :nq * (period - 1)].reshape(heads, nq, period - 1)[:, :, :nk] * LOG2E
    qc = np.arange(nq)[:, None] // CHUNK
    kc = np.arange(nk)[None, :] // CHUNK
    band = (kc >= qc) & (kc <= qc + PAST_CHUNKS)
    first_chunk = np.arange(0, PAST_CHUNKS + ATTN_Q_CHUNKS, ATTN_Q_CHUNKS)
    keep = band[None] & (kc[None] >= PAST_CHUNKS - first_chunk[:, None, None])
    return jnp.where(keep[:, None], bias[None], -jnp.inf)


def _layer(x, l, p, consts):
    bsz, seq, _ = x.shape
    tm = min(TOKEN_TILE, seq)
    n_t = seq // tm
    n_c = seq // CHUNK
    ones_bd, sel = consts
    g = p['norm_g'][l][None, :]

    tma = min(RWKV_TOKEN_TILE, seq)
    tok = lambda w: pl.BlockSpec((None, tm, w), lambda b, t: (b, t, 0))
    toka = lambda w: pl.BlockSpec((None, tma, w), lambda b, t: (b, t, 0))
    rows_a, rows_c = min(RWKV_ROWS, bsz), min(HGRN_ROWS, bsz)
    blk = min(SCAN_CHUNKS, n_c) * CHUNK
    qblk = min(ATTN_CHUNKS, n_c) * CHUNK
    chk = lambda w, rows: pl.BlockSpec((rows, blk, w), lambda b, c: (b, c, 0))
    qck = lambda w: pl.BlockSpec((None, qblk, w), lambda b, c: (b, c, 0))
    act = lambda dt: jax.ShapeDtypeStruct((bsz, seq, WIDTH), dt)

    w_in = p['w_in'][l]
    spans = ((0, WIDTH), (WIDTH + LORA, 2 * WIDTH + LORA), (2 * WIDTH + LORA, 3 * WIDTH + LORA),
             (3 * WIDTH + 2 * LORA, 4 * WIDTH + 2 * LORA), (WIDTH, WIDTH + LORA),
             (3 * WIDTH + LORA, 3 * WIDTH + 2 * LORA))
    reorder = lambda t: jnp.concatenate([t[..., a:b] for a, b in spans], axis=-1)
    pa = jnp.stack([p['rwkv_w0'][l], p['rwkv_a0'][l], p['rwkv_k_k'][l], p['rwkv_k_a'][l],
                    p['rwkv_r_k'][l].reshape(WIDTH)] + [jnp.zeros((WIDTH,), F32)] * 3)
    r, lw, k2, v, kk, bv, bonus, gate_a = pl.pallas_call(
        _rwkv_prep_kernel,
        grid=(bsz, seq // tma),
        in_specs=[toka(D_MODEL), _full((1, D_MODEL)), _full((D_MODEL, A_COLS)), _full((1, A_COLS)),
                  _full((8, WIDTH)), _full((LORA, WIDTH)), _full((LORA, WIDTH)), _full((WIDTH, WIDTH))],
        out_specs=[toka(WIDTH)] * 8,
        out_shape=[act(BF16), act(F32)] + [act(BF16)] * 6,
        scratch_shapes=[pltpu.VMEM((8, A_COLS), F32)],
        compiler_params=_params("parallel", "arbitrary"),
        name="rwkv_prep",
    )(x, g, reorder(w_in).astype(BF16), reorder(p['rwkv_mu'][l])[None, :], pa,
      p['rwkv_w2'][l], p['rwkv_a2'][l], ones_bd)

    ln = jnp.stack([p['rwkv_ln_w'][l], p['rwkv_ln_b'][l]] + [jnp.zeros((WIDTH,), F32)] * 6)
    y_a = pl.pallas_call(
        _rwkv_scan_kernel,
        grid=(bsz // rows_a, seq // blk),
        in_specs=[chk(WIDTH, rows_a)] * 8 + [_full((8, WIDTH)), _full((CHUNK, 3 * CHUNK))],
        out_specs=chk(WIDTH, rows_a),
        out_shape=act(BF16),
        scratch_shapes=[pltpu.VMEM((rows_a * HEADS // 2, 2 * HEAD_DIM, 2 * HEAD_DIM), F32)],
        compiler_params=_params("parallel", "arbitrary"),
        name="rwkv_scan",
    )(r, lw, k2, v, kk, bv, bonus, gate_a, ln, sel[:CHUNK])

    pad_tiles = PAD_ROWS // tm
    wbc = w_in[:, A_COLS:A_COLS + 8 * WIDTH].astype(BF16)
    qk = jnp.stack([jnp.tile(p['attn_q_norm'][l], HEADS), jnp.tile(p['attn_k_norm'][l], HEADS)]
                   + [jnp.zeros((WIDTH,), F32)] * 6)
    n_layers = p['hgrn_lb'].shape[0]
    shift = lambda b, i: (b, jnp.maximum(i - pad_tiles, 0), 0)
    late = pl.BlockSpec((None, tm, WIDTH), shift)
    q, k_pad, v_pad, gate_b, qc, kc, lf, vc, gate_c = pl.pallas_call(
        functools.partial(_attn_hgrn_prep_kernel, pad_tiles=pad_tiles, layer=l),
        grid=(bsz, n_t + pad_tiles),
        in_specs=[pl.BlockSpec((None, tm, D_MODEL), shift), _full((1, D_MODEL)),
                  _full((D_MODEL, 8 * WIDTH)), _full((8, WIDTH)), _full((WIDTH, WIDTH)),
                  _full((n_layers, WIDTH))],
        out_specs=[late, pl.BlockSpec((None, WIDTH, tm), lambda b, i: (b, 0, i)), tok(WIDTH), late,
                   late, late, late, late, late],
        out_shape=[act(BF16), jax.ShapeDtypeStruct((bsz, WIDTH, seq + PAD_ROWS), BF16),
                   jax.ShapeDtypeStruct((bsz, seq + PAD_ROWS, WIDTH), BF16), act(BF16),
                   act(BF16), act(BF16), act(F32), act(BF16), act(BF16)],
        compiler_params=_params("parallel", "arbitrary"),
        name="attn_hgrn_prep",
    )(x, g, wbc, qk, ones_bd, p['hgrn_lb'])

    band = pl.BlockSpec((None, seq + PAD_ROWS, WIDTH), lambda b, c: (b, 0, 0))
    band_t = pl.BlockSpec((None, WIDTH, seq + PAD_ROWS), lambda b, c: (b, 0, 0))
    bias = _band_bias(p['attn_rel_bias'][l])
    y_b = pl.pallas_call(
        _attn_band_kernel,
        grid=(bsz, seq // qblk),
        in_specs=[qck(WIDTH), band_t, band,
                  pl.BlockSpec(bias.shape, lambda b, c: (0, 0, 0, 0), pipeline_mode=pl.Buffered(1)),
                  qck(WIDTH)],
        out_specs=qck(WIDTH),
        out_shape=act(BF16),
        compiler_params=_params("parallel", "arbitrary"),
        name="attn_band",
    )(q, k_pad, v_pad, bias, gate_b)

    y_c = pl.pallas_call(
        _hgrn_scan_kernel,
        grid=(bsz // rows_c, seq // blk),
        in_specs=[chk(WIDTH, rows_c)] * 5 + [_full(sel.shape), _full((WIDTH, WIDTH)), _full((1, WIDTH))],
        out_specs=chk(WIDTH, rows_c),
        out_shape=act(BF16),
        scratch_shapes=[pltpu.VMEM((rows_c * HEADS // 2, 2 * HEAD_DIM, 2 * HEAD_DIM), F32)],
        compiler_params=_params("parallel", "arbitrary"),
        name="hgrn_scan",
    )(qc, kc, lf, vc, gate_c, sel, ones_bd, jnp.tile(p['hgrn_norm'][l], HEADS)[None, :])

    wg = w_in[:, A_COLS + 8 * WIDTH:].astype(BF16)
    return pl.pallas_call(
        _merge_kernel,
        grid=(bsz, n_t),
        in_specs=[tok(D_MODEL), _full((1, D_MODEL)), _full((D_MODEL, 3 * D_MODEL)),
                  tok(WIDTH), tok(WIDTH), tok(WIDTH),
                  _full((WIDTH, D_MODEL)), _full((WIDTH, D_MODEL)), _full((WIDTH, D_MODEL)),
                  _full((D_MODEL, D_MODEL))],
        out_specs=tok(D_MODEL),
        out_shape=jax.ShapeDtypeStruct(x.shape, x.dtype),
        compiler_params=_params("parallel", "parallel"),
        name="merge",
    )(x, g, wg, y_a, y_b, y_c, p['proj_a'][l].astype(BF16), p['proj_b'][l].astype(BF16),
      p['proj_c'][l].astype(BF16), p['w_out'][l].astype(BF16))


def kernel(x, norm_g, w_in, rwkv_mu, rwkv_w0, rwkv_w2, rwkv_a0, rwkv_a2, rwkv_k_k, rwkv_k_a, rwkv_r_k,
           rwkv_ln_w, rwkv_ln_b, attn_q_norm, attn_k_norm, attn_rel_bias, hgrn_lb, hgrn_norm,
           proj_a, proj_b, proj_c, w_out):
    p = dict(norm_g=norm_g, w_in=w_in, rwkv_mu=rwkv_mu, rwkv_w0=rwkv_w0, rwkv_w2=rwkv_w2,
             rwkv_a0=rwkv_a0, rwkv_a2=rwkv_a2, rwkv_k_k=rwkv_k_k, rwkv_k_a=rwkv_k_a, rwkv_r_k=rwkv_r_k,
             rwkv_ln_w=rwkv_ln_w, rwkv_ln_b=rwkv_ln_b, attn_q_norm=attn_q_norm, attn_k_norm=attn_k_norm,
             attn_rel_bias=attn_rel_bias, hgrn_lb=hgrn_lb, hgrn_norm=hgrn_norm,
             proj_a=proj_a, proj_b=proj_b, proj_c=proj_c, w_out=w_out)
    head = np.arange(WIDTH) // HEAD_DIM
    ones_bd = jnp.asarray(head[:, None] == head[None, :], BF16)
    consts = (ones_bd, _edge_selectors())
    for l in range(norm_g.shape[0]):
        x = _layer(x, l, p, consts)
    return x
```
